```python
import math
import jax
import jax.numpy as jnp
from jax import lax
import numpy as np


D_MODEL = 1024
BATCH = 1
SEQ = 16384
DEPTH = 4

N_MIXERS = 3
N_LAYERS_A = (DEPTH + 2) // 3
N_LAYERS_B = (DEPTH + 1) // 3
N_LAYERS_C = DEPTH // 3
DEEPNORM_ALPHA = (2.0 * DEPTH) ** 0.25
DEEPNORM_BETA = (8.0 * DEPTH) ** -0.25
LN_EPS = 1e-5

RW_HEAD = 64
RW_HEADS = D_MODEL // RW_HEAD
RW_DECAY_LORA = 64
RW_AAA_LORA = 64
RW_GATE_LORA = 160
RW_GN_EPS = 64e-5

MB_DI = 2 * D_MODEL
MB_HEADDIM = 64
MB_HEADS = MB_DI // MB_HEADDIM
MB_GROUPS = 4
MB_HPG = MB_HEADS // MB_GROUPS
MB_STATE = 128
MB_CONV = 4
MB_CHUNK = 128
MB_CONV_DIM = MB_DI + 2 * MB_GROUPS * MB_STATE
MB_PROJ = MB_DI + MB_CONV_DIM + MB_HEADS
MB_NORM_EPS = 1e-5

GL_HEADS = 4
GL_KD = D_MODEL // 2
GL_VD = D_MODEL
GL_DK = GL_KD // GL_HEADS
GL_DV = GL_VD // GL_HEADS
GL_GATE_LORA = 16
GL_GATE_NORM = 16.0
GL_CHUNK = 64
GL_PROJ = 2 * GL_KD + 2 * GL_VD + GL_GATE_LORA
GL_NORM_EPS = 1e-5

MOE_GROUPS = 4
MOE_PER_GROUP = 8
MOE_EXPERTS = MOE_GROUPS * MOE_PER_GROUP
MOE_TOPK = 2
MOE_FF = 512
MOE_BLOCK = 128

kernel_name = 'hybrid_rwkv7_mamba2_gla_hmoe_deepnorm'


def layer_norm(x, g, b):
    xf = x.astype(jnp.float32)
    mu = jnp.mean(xf, -1, keepdims=True)
    var = jnp.mean(jnp.square(xf - mu), -1, keepdims=True)
    return ((xf - mu) * lax.rsqrt(var + LN_EPS) * g + b).astype(x.dtype)


def token_shift(x):
    return jnp.pad(x, ((0, 0), (1, 0), (0, 0)))[:, :-1]


def causal_depthwise_conv(x, w, b):
    width, ch = w.shape
    y = lax.conv_general_dilated(x, w[:, None, :].astype(x.dtype), window_strides=(1,),
                                 padding=[(width - 1, 0)],
                                 dimension_numbers=('NWC', 'WIO', 'NWC'),
                                 feature_group_count=ch)
    return y + b


def rwkv7_mix(x, mu, w_rkv, w0, w1, w2, a0, a1, a2, g1, g2, k_k, k_a, r_k, lnx_g, lnx_b, w_o):
    f32 = jnp.float32
    bsz, seq, dm = x.shape
    H, N = RW_HEADS, RW_HEAD
    xx = token_shift(x) - x
    xs = x[None] + xx[None] * mu[:, None, None, :]
    r, k, v = jnp.einsum('pbsd,pde->pbse', xs[:3], w_rkv)
    xw, xa, xg = xs[3], xs[4], xs[5]
    w_log = -jax.nn.softplus(-(w0 + jnp.tanh(xw @ w1) @ w2)) - 0.5
    decay = jnp.exp(-jnp.exp(w_log.astype(f32)))
    a = jax.nn.sigmoid((a0 + (xa @ a1) @ a2).astype(f32))
    g = jax.nn.sigmoid(xg @ g1) @ g2

    def heads(t):
        return t.astype(f32).reshape(bsz, seq, H, N)

    kk = heads(k * k_k)
    kk = kk / jnp.maximum(jnp.sqrt(jnp.sum(kk * kk, -1, keepdims=True)), 1e-12)
    k_h = heads(k.astype(f32) * (1.0 + (a - 1.0) * k_a.astype(f32)))
    r_h, v_h, a_h, w_h = heads(r), heads(v), heads(a), heads(decay)

    def step(state, inp):
        r_t, w_t, k_t, v_t, kk_t, a_t = inp
        sa = jnp.einsum('bhvk,bhk->bhv', state, -kk_t)
        state = (state * w_t[:, :, None, :] + sa[..., None] * (kk_t * a_t)[:, :, None, :]
                 + v_t[..., None] * k_t[:, :, None, :])
        return state, jnp.einsum('bhvk,bhk->bhv', state, r_t)

    seq_in = tuple(jnp.moveaxis(t, 1, 0) for t in (r_h, w_h, k_h, v_h, kk, a_h))
    _, y = lax.scan(step, jnp.zeros((bsz, H, N, N), f32), seq_in)
    y = jnp.moveaxis(y, 0, 1)
    mu_y = jnp.mean(y, -1, keepdims=True)
    var_y = jnp.mean(jnp.square(y - mu_y), -1, keepdims=True)
    y = ((y - mu_y) * lax.rsqrt(var_y + RW_GN_EPS)).reshape(bsz, seq, dm) * lnx_g + lnx_b
    bonus = jnp.sum(r_h * k_h * r_k.astype(f32), -1, keepdims=True) * v_h
    y = (y + bonus.reshape(bsz, seq, dm)) * g
    return (y.astype(x.dtype) @ w_o).astype(x.dtype)


def mamba2_mix(x, w_in, conv_w, conv_b, dt_bias, a_log, d_skip, norm_g, w_out):
    f32 = jnp.float32
    bsz, seq, _ = x.shape
    G, E, P, N, L = MB_GROUPS, MB_HPG, MB_HEADDIM, MB_STATE, MB_CHUNK
    nc = seq // L
    zxbcdt = x @ w_in
    z, xbc, dt = jnp.split(zxbcdt, [MB_DI, MB_DI + MB_CONV_DIM], axis=-1)
    xbc = jax.nn.silu(causal_depthwise_conv(xbc, conv_w, conv_b))
    xs, bm, cm = jnp.split(xbc, [MB_DI, MB_DI + G * N], axis=-1)
    dt = jax.nn.softplus((dt + dt_bias).astype(f32))
    a = -jnp.exp(a_log.astype(f32)).reshape(G, E)

    xs_c = xs.astype(f32).reshape(bsz, nc, L, G, E, P)
    b_c = bm.astype(f32).reshape(bsz, nc, L, G, N)
    c_c = cm.astype(f32).reshape(bsz, nc, L, G, N)
    dt_c = dt.reshape(bsz, nc, L, G, E)
    da_cs = jnp.cumsum(dt_c * a, axis=2)
    xdt = xs_c * dt_c[..., None]

    causal = jnp.tril(jnp.ones((L, L), bool))[:, :, None, None]
    seg = da_cs[:, :, :, None] - da_cs[:, :, None, :]
    decay_ls = jnp.exp(jnp.where(causal, seg, -jnp.inf))
    scores = jnp.einsum('bclgn,bcsgn->bclsg', c_c, b_c)
    y_diag = jnp.einsum('bclsg,bclsge,bcsgep->bclgep', scores, decay_ls, xdt)

    decay_to_end = jnp.exp(da_cs[:, :, -1:] - da_cs)
    chunk_states = jnp.einsum('bclgn,bclge,bclgep->bcgepn', b_c, decay_to_end, xdt)
    chunk_decay = jnp.exp(da_cs[:, :, -1])

    def carry(h, inp):
        s_c, dec_c = inp
        return h * dec_c[..., None, None] + s_c, h

    _, prev = lax.scan(carry, jnp.zeros((bsz, G, E, P, N), f32),
                       (jnp.moveaxis(chunk_states, 1, 0), jnp.moveaxis(chunk_decay, 1, 0)))
    prev = jnp.moveaxis(prev, 0, 1)
    y_off = jnp.einsum('bclgn,bcgepn,bclge->bclgep', c_c, prev, jnp.exp(da_cs))
    y = y_diag + y_off + xs_c * d_skip.astype(f32).reshape(G, E)[:, :, None]
    y = y.reshape(bsz, seq, MB_DI) * jax.nn.silu(z.astype(f32))
    yg = y.reshape(bsz, seq, G, MB_DI // G)
    yg = yg * lax.rsqrt(jnp.mean(yg * yg, -1, keepdims=True) + MB_NORM_EPS)
    y = yg.reshape(bsz, seq, MB_DI) * norm_g
    return (y.astype(x.dtype) @ w_out).astype(x.dtype)


def gla_mix(x, w_in, gk_w2, gk_b, norm_g, w_out):
    f32 = jnp.float32
    bsz, seq, _ = x.shape
    H, DK, DV, C = GL_HEADS, GL_DK, GL_DV, GL_CHUNK
    n = seq // C
    proj = x @ w_in
    q, k, v, g, gk_lr = jnp.split(proj, [GL_KD, 2 * GL_KD, 2 * GL_KD + GL_VD, 2 * GL_KD + 2 * GL_VD], axis=-1)
    log_alpha = jax.nn.log_sigmoid((gk_lr @ gk_w2 + gk_b).astype(f32)) / GL_GATE_NORM
    q = q.astype(f32).reshape(bsz, n, C, H, DK) * (DK ** -0.5)
    k = k.astype(f32).reshape(bsz, n, C, H, DK)
    v = v.astype(f32).reshape(bsz, n, C, H, DV)
    b_cs = jnp.cumsum(log_alpha.reshape(bsz, n, C, H, DK), axis=2)
    q_in = q * jnp.exp(b_cs)
    k_in = k * jnp.exp(-b_cs)
    k_end = k * jnp.exp(b_cs[:, :, -1:] - b_cs)
    mask = jnp.tril(jnp.ones((C, C), bool))
    attn = jnp.where(mask, jnp.einsum('bnihk,bnjhk->bnhij', q_in, k_in), 0.0)
    o_intra = jnp.einsum('bnhij,bnjhv->bnihv', attn, v)
    chunk_kv = jnp.einsum('bnjhk,bnjhv->bnhkv', k_end, v)
    chunk_decay = jnp.exp(b_cs[:, :, -1])

    def carry(s, inp):
        kv_c, dec_c = inp
        return s * dec_c[..., None] + kv_c, s

    _, prev = lax.scan(carry, jnp.zeros((bsz, H, DK, DV), f32),
                       (jnp.moveaxis(chunk_kv, 1, 0), jnp.moveaxis(chunk_decay, 1, 0)))
    prev = jnp.moveaxis(prev, 0, 1)
    o = o_intra + jnp.einsum('bnihk,bnhkv->bnihv', q_in, prev)
    o = o.reshape(bsz, seq, H, DV)
    o = o * lax.rsqrt(jnp.mean(o * o, -1, keepdims=True) + GL_NORM_EPS) * norm_g
    o = o.reshape(bsz, seq, GL_VD) * jax.nn.silu(g.astype(f32))
    return (o.astype(x.dtype) @ w_out).astype(x.dtype)


def hier_moe(x, w_group, b_group, w_route, b_route, w_in, w_down):
    f32 = jnp.float32
    bsz, seq, dm = x.shape
    T = bsz * seq
    E, EPG, K, BLK = MOE_EXPERTS, MOE_PER_GROUP, MOE_TOPK, MOE_BLOCK
    xt = x.reshape(T, dm)
    group_probs = jax.nn.softmax((xt @ w_group + b_group).astype(f32), -1)
    p_group, g_idx = lax.top_k(group_probs, 1)
    exp_logits = (xt @ w_route + b_route).astype(f32).reshape(T, MOE_GROUPS, EPG)
    sel_logits = jnp.take_along_axis(exp_logits, g_idx[:, :, None], axis=1)[:, 0]
    p_exp, e_local = lax.top_k(jax.nn.softmax(sel_logits, -1), K)
    gates = p_group * p_exp / jnp.sum(p_exp, -1, keepdims=True)
    expert_id = g_idx * EPG + e_local

    A = T * K
    flat_e = expert_id.reshape(A)
    flat_tok = jnp.repeat(jnp.arange(T, dtype=jnp.int32), K)
    flat_gate = gates.reshape(A)
    order = jnp.argsort(flat_e)
    se = flat_e[order]
    tok_sorted = flat_tok[order]
    counts = jnp.bincount(flat_e, length=E)
    starts = jnp.cumsum(counts) - counts
    padded = (counts + BLK - 1) // BLK * BLK
    pad_end = jnp.cumsum(padded)
    pad_start = pad_end - padded
    dest = pad_start[se] + (jnp.arange(A) - starts[se])
    n_blocks = (A + E * (BLK - 1) + BLK - 1) // BLK
    buf = jnp.zeros((n_blocks * BLK, dm), x.dtype).at[dest].set(xt[tok_sorted])
    block_expert = jnp.minimum(jnp.searchsorted(pad_end, jnp.arange(n_blocks) * BLK, side='right'), E - 1)

    def expert_block(blk):
        xb, e = blk
        hg, hu = jnp.split(xb @ w_in[e], 2, axis=-1)
        return (jax.nn.silu(hg) * hu) @ w_down[e]

    y_buf = lax.map(expert_block, (buf.reshape(n_blocks, BLK, dm), block_expert)).reshape(n_blocks * BLK, dm)
    y_assign = (y_buf[dest] * flat_gate[order][:, None]).astype(x.dtype)
    y = jnp.zeros((T, dm), x.dtype).at[tok_sorted].add(y_assign)
    return y.reshape(bsz, seq, dm)


def setup_inputs(seed: int = 0) -> dict:
    key = jax.random.key(seed)
    ks = iter(jax.random.split(key, 48))
    f32 = jnp.float32
    D = D_MODEL
    NA, NB, NC = N_LAYERS_A, N_LAYERS_B, N_LAYERS_C

    def nrm(shape, scale):
        return jax.random.normal(next(ks), shape, f32) * scale

    def unif(shape, lo, hi):
        return jax.random.uniform(next(ks), shape, f32, lo, hi)

    inp = {}
    inp['x'] = nrm((BATCH, SEQ, D), 1.0)
    inp['ln_g'] = 1.0 + nrm((DEPTH, 2, D), 0.01)
    inp['ln_b'] = nrm((DEPTH, 2, D), 0.01)
    inp['rw_mu'] = unif((NA, 6, D), 0.0, 1.0)
    inp['rw_w_rkv'] = nrm((NA, 3, D, D), D ** -0.5)
    inp['rw_w0'] = unif((NA, D), -6.0, -1.0)
    inp['rw_w1'] = nrm((NA, D, RW_DECAY_LORA), D ** -0.5)
    inp['rw_w2'] = nrm((NA, RW_DECAY_LORA, D), 0.1 * RW_DECAY_LORA ** -0.5)
    inp['rw_a0'] = nrm((NA, D), 0.1)
    inp['rw_a1'] = nrm((NA, D, RW_AAA_LORA), D ** -0.5)
    inp['rw_a2'] = nrm((NA, RW_AAA_LORA, D), 0.1 * RW_AAA_LORA ** -0.5)
    inp['rw_g1'] = nrm((NA, D, RW_GATE_LORA), D ** -0.5)
    inp['rw_g2'] = nrm((NA, RW_GATE_LORA, D), RW_GATE_LORA ** -0.5)
    inp['rw_k_k'] = 0.85 + nrm((NA, D), 0.01)
    inp['rw_k_a'] = 1.0 + nrm((NA, D), 0.01)
    inp['rw_r_k'] = nrm((NA, RW_HEADS, RW_HEAD), 0.1)
    inp['rw_lnx_g'] = 1.0 + nrm((NA, D), 0.01)
    inp['rw_lnx_b'] = nrm((NA, D), 0.01)
    inp['rw_w_o'] = nrm((NA, D, D), D ** -0.5 * DEEPNORM_BETA)
    inp['mb_w_in'] = nrm((NB, D, MB_PROJ), D ** -0.5)
    inp['mb_conv_w'] = nrm((NB, MB_CONV, MB_CONV_DIM), MB_CONV ** -0.5)
    inp['mb_conv_b'] = nrm((NB, MB_CONV_DIM), 0.01)
    dt0 = jnp.exp(unif((NB, MB_HEADS), math.log(1e-3), math.log(1e-1)))
    inp['mb_dt_bias'] = dt0 + jnp.log(-jnp.expm1(-dt0))
    inp['mb_a_log'] = jnp.log(unif((NB, MB_HEADS), 1.0, 16.0))
    inp['mb_d'] = 1.0 + nrm((NB, MB_HEADS), 0.01)
    inp['mb_norm_g'] = 1.0 + nrm((NB, MB_DI), 0.01)
    inp['mb_w_out'] = nrm((NB, MB_DI, D), MB_DI ** -0.5 * DEEPNORM_BETA)
    inp['gl_w_in'] = nrm((NC, D, GL_PROJ), D ** -0.5)
    inp['gl_gk_w2'] = nrm((NC, GL_GATE_LORA, GL_KD), GL_GATE_LORA ** -0.5)
    inp['gl_gk_b'] = nrm((NC, GL_KD), 0.1)
    inp['gl_norm_g'] = 1.0 + nrm((NC, GL_DV), 0.01)
    inp['gl_w_out'] = nrm((NC, GL_VD, D), GL_VD ** -0.5 * DEEPNORM_BETA)
    inp['moe_w_group'] = nrm((DEPTH, D, MOE_GROUPS), D ** -0.5)
    inp['moe_b_group'] = nrm((DEPTH, MOE_GROUPS), 0.01)
    inp['moe_w_route'] = nrm((DEPTH, D, MOE_EXPERTS), D ** -0.5)
    inp['moe_b_route'] = nrm((DEPTH, MOE_EXPERTS), 0.01)
    inp['moe_w_in'] = nrm((DEPTH, MOE_EXPERTS, D, 2 * MOE_FF), D ** -0.5)
    inp['moe_w_down'] = nrm((DEPTH, MOE_EXPERTS, MOE_FF, D), MOE_FF ** -0.5 * DEEPNORM_BETA)
    return inp


def reference(x, ln_g, ln_b, rw_mu, rw_w_rkv, rw_w0, rw_w1, rw_w2, rw_a0, rw_a1, rw_a2,
              rw_g1, rw_g2, rw_k_k, rw_k_a, rw_r_k, rw_lnx_g, rw_lnx_b, rw_w_o,
              mb_w_in, mb_conv_w, mb_conv_b, mb_dt_bias, mb_a_log, mb_d, mb_norm_g, mb_w_out,
              gl_w_in, gl_gk_w2, gl_gk_b, gl_norm_g, gl_w_out,
              moe_w_group, moe_b_group, moe_w_route, moe_b_route, moe_w_in, moe_w_down):
    h = x
    for i in range(DEPTH):
        kind, j = i % N_MIXERS, i // N_MIXERS
        if kind == 0:
            m = rwkv7_mix(h, rw_mu[j], rw_w_rkv[j], rw_w0[j], rw_w1[j], rw_w2[j], rw_a0[j], rw_a1[j],
                          rw_a2[j], rw_g1[j], rw_g2[j], rw_k_k[j], rw_k_a[j], rw_r_k[j],
                          rw_lnx_g[j], rw_lnx_b[j], rw_w_o[j])
        elif kind == 1:
            m = mamba2_mix(h, mb_w_in[j], mb_conv_w[j], mb_conv_b[j], mb_dt_bias[j], mb_a_log[j],
                           mb_d[j], mb_norm_g[j], mb_w_out[j])
        else:
            m = gla_mix(h, gl_w_in[j], gl_gk_w2[j], gl_gk_b[j], gl_norm_g[j], gl_w_out[j])
        h = layer_norm(DEEPNORM_ALPHA * h + m, ln_g[i, 0], ln_b[i, 0])
        f = hier_moe(h, moe_w_group[i], moe_b_group[i], moe_w_route[i], moe_b_route[i],
                     moe_w_in[i], moe_w_down[i])
        h = layer_norm(DEEPNORM_ALPHA * h + f, ln_g[i, 1], ln_b[i, 1])
    return h
```

```python
import functools

import jax
import jax.numpy as jnp
from jax import lax
from jax.experimental import pallas as pl
from jax.experimental.pallas import tpu as pltpu

F32 = jnp.float32
BF16 = jnp.bfloat16

LANES = 128
SUBLANES = 8
VMEM_LIMIT = 56 * 1024 * 1024

DEPTH = 4
ALPHA = (2.0 * DEPTH) ** 0.25
LN_EPS = 1e-5

RW_HEAD = 64
RW_CHUNK = 64
RW_GN_EPS = 64e-5

MB_HEADDIM = 64
MB_GROUPS = 4
MB_STATE = 128
MB_CHUNK = 128
MB_CONV = 4
MB_NORM_EPS = 1e-5

GL_HEADS = 4
GL_CHUNK = 64
GL_GATE_NORM = 16.0
GL_NORM_EPS = 1e-5

MOE_GROUPS = 4
MOE_PER_GROUP = 8
MOE_TOPK = 2
MOE_BLOCK = 256


def _dot(a, b):
    return jnp.dot(a.astype(BF16), b.astype(BF16), preferred_element_type=F32)


def _dot_nt(a, b):
    return lax.dot_general(a.astype(BF16), b.astype(BF16), (((1,), (1,)), ((), ())),
                           preferred_element_type=F32)


def _dot_tn(a, b):
    return lax.dot_general(a.astype(BF16), b.astype(BF16), (((0,), (0,)), ((), ())),
                           preferred_element_type=F32)


def _dot_f32(a, b):
    return jnp.dot(a, b, preferred_element_type=F32, precision=lax.Precision.HIGHEST)


def _sigmoid(x):
    return 1.0 / (1.0 + jnp.exp(-x))


def _softplus(x):
    return jnp.maximum(x, 0.0) + jnp.log(1.0 + jnp.exp(-jnp.abs(x)))


def _silu(x):
    return x * _sigmoid(x)


def _tri(n, dtype=F32):
    r = lax.broadcasted_iota(jnp.int32, (n, n), 0)
    c = lax.broadcasted_iota(jnp.int32, (n, n), 1)
    return (c <= r).astype(dtype)


def _params(sem):
    return pltpu.CompilerParams(dimension_semantics=sem, vmem_limit_bytes=VMEM_LIMIT)


def _pad_cols(w, n):
    return jnp.pad(w, ((0, 0), (0, n - w.shape[1])))


def _pad_rows(w, n):
    return jnp.pad(w, ((0, n - w.shape[0]), (0, 0)))


def _row(v):
    return v.reshape(1, -1).astype(F32)


def _prev_rows_spec(tm, width):
    per = tm // SUBLANES
    return pl.BlockSpec((SUBLANES, width), lambda i: (jnp.maximum(i * per - 1, 0), 0))


def _mm_kernel(x_ref, w_ref, o_ref):
    o_ref[...] = _dot(x_ref[...], w_ref[...])


def _mm(x, w, tm=512, tn=None, name="mm"):
    t, k = x.shape
    n = w.shape[1]
    tn = tn or min(n, 1024)
    return pl.pallas_call(
        _mm_kernel,
        grid=(t // tm, n // tn),
        in_specs=[pl.BlockSpec((tm, k), lambda i, j: (i, 0)),
                  pl.BlockSpec((k, tn), lambda i, j: (0, j))],
        out_specs=pl.BlockSpec((tm, tn), lambda i, j: (i, j)),
        out_shape=jax.ShapeDtypeStruct((t, n), F32),
        compiler_params=_params(("parallel", "parallel")),
        name=name,
    )(x, w)


def _layer_norm(z, g, b):
    mu = jnp.mean(z, axis=-1, keepdims=True)
    zc = z - mu
    var = jnp.mean(zc * zc, axis=-1, keepdims=True)
    return zc * lax.rsqrt(var + LN_EPS) * g + b


def _mm_res_ln_kernel(x_ref, w_ref, res_ref, g_ref, b_ref, wr_ref, br_ref, o_ref, lg_ref):
    y = _dot(x_ref[...], w_ref[...])
    h = _layer_norm(ALPHA * res_ref[...] + y, g_ref[...], b_ref[...])
    o_ref[...] = h
    lg_ref[...] = _dot_f32(h, wr_ref[...]) + br_ref[...]


def _mm_res_ln(x, w, res, g, b, wr, br, tm=512):
    t, k = x.shape
    d = w.shape[1]
    nr = wr.shape[1]
    return pl.pallas_call(
        _mm_res_ln_kernel,
        grid=(t // tm,),
        in_specs=[pl.BlockSpec((tm, k), lambda i: (i, 0)),
                  pl.BlockSpec((k, d), lambda i: (0, 0)),
                  pl.BlockSpec((tm, d), lambda i: (i, 0)),
                  pl.BlockSpec((1, d), lambda i: (0, 0)),
                  pl.BlockSpec((1, d), lambda i: (0, 0)),
                  pl.BlockSpec((d, nr), lambda i: (0, 0)),
                  pl.BlockSpec((1, nr), lambda i: (0, 0))],
        out_specs=[pl.BlockSpec((tm, d), lambda i: (i, 0)),
                   pl.BlockSpec((tm, nr), lambda i: (i, 0))],
        out_shape=[jax.ShapeDtypeStruct((t, d), F32), jax.ShapeDtypeStruct((t, nr), F32)],
        compiler_params=_params(("parallel",)),
        name="mm_res_ln",
    )(x, w, res, g, b, wr, br)


def _combine_ln_kernel(y0_ref, y1_ref, gt_ref, res_ref, g_ref, b_ref, o_ref):
    gt = gt_ref[...]
    y = y0_ref[...] * gt[:, 0:1] + y1_ref[...] * gt[:, 1:2]
    o_ref[...] = _layer_norm(ALPHA * res_ref[...] + y, g_ref[...], b_ref[...])


def _combine_ln(y0, y1, gates, res, g, b, tm=512):
    t, d = res.shape
    row = pl.BlockSpec((tm, d), lambda i: (i, 0))
    vec = pl.BlockSpec((1, d), lambda i: (0, 0))
    return pl.pallas_call(
        _combine_ln_kernel,
        grid=(t // tm,),
        in_specs=[row, row, pl.BlockSpec((tm, MOE_TOPK), lambda i: (i, 0)), row, vec, vec],
        out_specs=row,
        out_shape=jax.ShapeDtypeStruct((t, d), F32),
        compiler_params=_params(("parallel",)),
        name="combine_ln",
    )(y0, y1, gates, res, g, b)


def _rwkv_proj_kernel(x_ref, xp_ref, mu_ref, wrkv_ref, w1_ref, w2_ref, w0_ref, a1_ref, a2_ref, a0_ref,
                      g1_ref, g2_ref, r_ref, k_ref, v_ref, lw_ref, a_ref, g_ref):
    i = pl.program_id(0)
    x = x_ref[...]
    prev = jnp.where(i > 0, xp_ref[SUBLANES - 1:SUBLANES, :], 0.0)
    rows = lax.broadcasted_iota(jnp.int32, x.shape, 0)
    shifted = jnp.where(rows == 0, prev, pltpu.roll(x, 1, 0))
    xx = shifted - x

    def mix(p):
        return x + xx * mu_ref[p:p + 1, :]

    r_ref[...] = _dot(mix(0), wrkv_ref[0])
    k_ref[...] = _dot(mix(1), wrkv_ref[1])
    v_ref[...] = _dot(mix(2), wrkv_ref[2])
    w_pre = w0_ref[...] + _dot(jnp.tanh(_dot(mix(3), w1_ref[...])), w2_ref[...])
    lw_ref[...] = -jnp.exp(-_softplus(-w_pre) - 0.5)
    a_ref[...] = _sigmoid(a0_ref[...] + _dot(_dot(mix(4), a1_ref[...]), a2_ref[...]))
    g_ref[...] = _dot(_sigmoid(_dot(mix(5), g1_ref[...])), g2_ref[...])


def _rwkv_proj(x, mu, w_rkv, w1, w2, w0, a1, a2, a0, g1, g2, tm=256):
    t, d = x.shape
    full = lambda a: pl.BlockSpec(a.shape, lambda i: (0,) * a.ndim)
    row = pl.BlockSpec((tm, d), lambda i: (i, 0))
    args = (mu, w_rkv, w1, w2, w0, a1, a2, a0, g1, g2)
    return pl.pallas_call(
        _rwkv_proj_kernel,
        grid=(t // tm,),
        in_specs=[row, _prev_rows_spec(tm, d)] + [full(a) for a in args],
        out_specs=[row] * 6,
        out_shape=[jax.ShapeDtypeStruct((t, d), F32)] * 6,
        compiler_params=_params(("parallel",)),
        name="rwkv_proj",
    )(x, x, *args)


def _rwkv_rec_kernel(r_ref, k_ref, v_ref, lw_ref, a_ref, g_ref, kk_ref, ka_ref, rk_ref, lg_ref, lb_ref,
                     o_ref, s_ref):
    c = pl.program_id(0)

    @pl.when(c == 0)
    def _():
        s_ref[...] = jnp.zeros_like(s_ref)

    n = RW_CHUNK
    cs_all = _dot_f32(_tri(n), lw_ref[...])
    lane = lax.broadcasted_iota(jnp.int32, (n, LANES), 1)
    first = lane < RW_HEAD
    r2 = lax.broadcasted_iota(jnp.int32, (2 * n, LANES), 0)
    c2 = lax.broadcasted_iota(jnp.int32, (2 * n, LANES), 1)
    own = (r2 < n) == (c2 < RW_HEAD)
    strict = c2 < r2
    incl = c2 <= r2
    eye = (c2 == r2).astype(F32)

    def stack(x):
        return jnp.where(own, jnp.concatenate([x, x], axis=0), 0.0)

    def head_sum(x):
        sa = jnp.sum(jnp.where(first, x, 0.0), axis=1, keepdims=True)
        sb = jnp.sum(jnp.where(first, 0.0, x), axis=1, keepdims=True)
        return jnp.where(first, sa, sb)

    for p in range(r_ref.shape[1] // LANES):
        sl = slice(LANES * p, LANES * (p + 1))
        r = r_ref[:, sl]
        k = k_ref[:, sl]
        v = v_ref[:, sl]
        lw = lw_ref[:, sl]
        a = a_ref[:, sl]
        cs = cs_all[:, sl]
        cs_end = cs[n - 1:n, :]

        kk = k * kk_ref[:, sl]
        kk = kk / jnp.maximum(jnp.sqrt(head_sum(kk * kk)), 1e-12)
        kh = k * (1.0 + (a - 1.0) * ka_ref[:, sl])
        kka = kk * a
        g_inv = jnp.exp(-cs)
        g_end = jnp.exp(cs_end - cs)

        at = stack(-kk * jnp.exp(cs - lw))
        bt = stack(kka * g_inv)
        kt = stack(kh * g_inv)
        rt = stack(r * jnp.exp(cs))
        bh = stack(kka * g_end)
        khat = stack(kh * g_end)
        vs = stack(v)

        nm = jnp.where(strict, _dot_nt(at, bt), 0.0)
        aak = jnp.where(strict, _dot_nt(at, kt), 0.0)
        arb = jnp.where(incl, _dot_nt(rt, bt), 0.0)
        ark = jnp.where(incl, _dot_nt(rt, kt), 0.0)

        tm = eye + nm
        pw = nm
        for _ in range(5):
            pw = _dot(pw, pw)
            tm = tm + _dot(tm, pw)

        s = s_ref[p]
        u = _dot(tm, _dot_nt(at, s) + _dot(aak, vs))
        y = _dot_nt(rt, s) + _dot(arb, u) + _dot(ark, vs)
        s_ref[p] = s * jnp.exp(cs_end) + _dot_tn(u, bh) + _dot_tn(vs, khat)

        mu = jnp.sum(y, axis=1, keepdims=True) * (1.0 / RW_HEAD)
        yc = jnp.where(own, y - mu, 0.0)
        var = jnp.sum(yc * yc, axis=1, keepdims=True) * (1.0 / RW_HEAD)
        yn = yc * lax.rsqrt(var + RW_GN_EPS)
        yn = yn[:n] + yn[n:]
        bonus = head_sum(r * kh * rk_ref[:, sl]) * v
        o_ref[:, sl] = (yn * lg_ref[:, sl] + lb_ref[:, sl] + bonus) * g_ref[:, sl]


def _rwkv_rec(r, k, v, lw, a, g, k_k, k_a, r_k, lnx_g, lnx_b):
    t, d = r.shape
    row = pl.BlockSpec((RW_CHUNK, d), lambda c: (c, 0))
    vec = pl.BlockSpec((1, d), lambda c: (0, 0))
    return pl.pallas_call(
        _rwkv_rec_kernel,
        grid=(t // RW_CHUNK,),
        in_specs=[row] * 6 + [vec] * 5,
        out_specs=row,
        out_shape=jax.ShapeDtypeStruct((t, d), F32),
        scratch_shapes=[pltpu.VMEM((d // LANES, LANES, LANES), F32)],
        compiler_params=_params(("arbitrary",)),
        name="rwkv_rec",
    )(r, k, v, lw, a, g, k_k, k_a, r_k, lnx_g, lnx_b)


def _rwkv7_mix(h, ln_g, ln_b, mu, w_rkv, w0, w1, w2, a0, a1, a2, g1, g2, k_k, k_a, r_k, lnx_g, lnx_b, w_o,
               wr, br):
    t, d = h.shape
    lora = w1.shape[1]
    lp = -(-lora // LANES) * LANES
    gl = g1.shape[1]
    gp = -(-gl // LANES) * LANES
    r, k, v, lw, a, g = _rwkv_proj(
        h, _pad_rows(mu, SUBLANES), w_rkv.astype(BF16),
        _pad_cols(w1, lp).astype(BF16), _pad_rows(w2, lp).astype(BF16), _row(w0),
        _pad_cols(a1, lp).astype(BF16), _pad_rows(a2, lp).astype(BF16), _row(a0),
        _pad_cols(g1, gp).astype(BF16), _pad_rows(g2, gp).astype(BF16))
    y = _rwkv_rec(r, k, v, lw, a, g, _row(k_k), _row(k_a), _row(r_k), _row(lnx_g), _row(lnx_b))
    return _mm_res_ln(y, w_o.astype(BF16), h, _row(ln_g), _row(ln_b), wr, br)


def _mb_conv_kernel(x_ref, xp_ref, w_ref, b_ref, xs_ref, bm_ref, cm_ref, buf_ref):
    i = pl.program_id(0)
    tm = x_ref.shape[0]
    x = x_ref[...]
    buf_ref[0:SUBLANES, :] = jnp.where(i > 0, xp_ref[...], 0.0)
    buf_ref[SUBLANES:, :] = x
    acc = b_ref[...] + x * w_ref[MB_CONV - 1:MB_CONV, :]
    for s in range(1, MB_CONV):
        acc = acc + buf_ref[pl.ds(SUBLANES - s, tm), :] * w_ref[MB_CONV - 1 - s:MB_CONV - s, :]
    y = _silu(acc)
    di = xs_ref.shape[1]
    gn = bm_ref.shape[1]
    xs_ref[...] = y[:, :di]
    bm_ref[...] = y[:, di:di + gn]
    cm_ref[...] = y[:, di + gn:]


def _mb_conv(xbc, conv_w, conv_b, di, gn, tm=256):
    t, cd = xbc.shape
    return pl.pallas_call(
        _mb_conv_kernel,
        grid=(t // tm,),
        in_specs=[pl.BlockSpec((tm, cd), lambda i: (i, 0)),
                  _prev_rows_spec(tm, cd),
                  pl.BlockSpec((SUBLANES, cd), lambda i: (0, 0)),
                  pl.BlockSpec((1, cd), lambda i: (0, 0))],
        out_specs=[pl.BlockSpec((tm, di), lambda i: (i, 0)),
                   pl.BlockSpec((tm, gn), lambda i: (i, 0)),
                   pl.BlockSpec((tm, gn), lambda i: (i, 0))],
        out_shape=[jax.ShapeDtypeStruct((t, di), F32),
                   jax.ShapeDtypeStruct((t, gn), F32),
                   jax.ShapeDtypeStruct((t, gn), F32)],
        scratch_shapes=[pltpu.VMEM((tm + SUBLANES, cd), F32)],
        compiler_params=_params(("parallel",)),
        name="mb_conv",
    )(xbc, xbc, _pad_rows(conv_w, SUBLANES), _row(conv_b))


def _mb_ssd_kernel(xs_ref, bm_ref, cm_ref, dt_ref, z_ref, dtb_ref, alog_ref, dsk_ref, ng_ref, o_ref, s_ref):
    c = pl.program_id(0)

    @pl.when(c == 0)
    def _():
        s_ref[...] = jnp.zeros_like(s_ref)

    n = MB_CHUNK
    p_dim = MB_HEADDIM
    groups = s_ref.shape[0]
    hpg = s_ref.shape[1] // p_dim
    dt = _softplus(dt_ref[...] + dtb_ref[...])
    da = dt * (-jnp.exp(alog_ref[...]))
    cs = _dot_f32(_tri(n), da)
    cs_t = cs.T
    cs_end = cs[n - 1:n, :]
    to_end = jnp.exp(cs_end - cs)
    from_start = jnp.exp(cs)
    chunk_decay = jnp.exp(cs_end)
    r2 = lax.broadcasted_iota(jnp.int32, (n, n), 0)
    c2 = lax.broadcasted_iota(jnp.int32, (n, n), 1)
    incl = c2 <= r2

    gw = hpg * p_dim
    for g in range(groups):
        bg = bm_ref[:, MB_STATE * g:MB_STATE * (g + 1)]
        cg = cm_ref[:, MB_STATE * g:MB_STATE * (g + 1)]
        scores = _dot_nt(cg, bg)
        s_g = s_ref[g]
        y_off = _dot_nt(cg, s_g)
        y_parts = []
        xe_parts = []
        cd_parts = []
        for e in range(hpg):
            hd = g * hpg + e
            x_h = xs_ref[:, p_dim * hd:p_dim * (hd + 1)]
            seg = jnp.minimum(cs[:, hd:hd + 1] - cs_t[hd:hd + 1, :], 0.0)
            m = scores * jnp.where(incl, jnp.exp(seg), 0.0)
            xdt = x_h * dt[:, hd:hd + 1]
            y_h = (_dot(m, xdt) + y_off[:, p_dim * e:p_dim * (e + 1)] * from_start[:, hd:hd + 1]
                   + x_h * dsk_ref[:, p_dim * hd:p_dim * (hd + 1)])
            y_parts.append(y_h)
            xe_parts.append(xdt * to_end[:, hd:hd + 1])
            cd_parts.append(jnp.broadcast_to(chunk_decay[:, hd:hd + 1], (p_dim, 1)))
        s_ref[g] = s_g * jnp.concatenate(cd_parts, axis=0) + _dot_tn(jnp.concatenate(xe_parts, axis=1), bg)
        y_g = jnp.concatenate(y_parts, axis=1) * _silu(z_ref[:, gw * g:gw * (g + 1)])
        ms = jnp.mean(y_g * y_g, axis=-1, keepdims=True)
        o_ref[:, gw * g:gw * (g + 1)] = y_g * lax.rsqrt(ms + MB_NORM_EPS) * ng_ref[:, gw * g:gw * (g + 1)]


def _mb_ssd(xs, bm, cm, dt, z, dt_bias, a_log, d_skip, norm_g):
    t, di = xs.shape
    gn = bm.shape[1]
    groups = gn // MB_STATE
    blk = lambda w: pl.BlockSpec((MB_CHUNK, w), lambda c: (c, 0))
    vec = lambda w: pl.BlockSpec((1, w), lambda c: (0, 0))
    return pl.pallas_call(
        _mb_ssd_kernel,
        grid=(t // MB_CHUNK,),
        in_specs=[blk(di), blk(gn), blk(gn), blk(LANES), blk(di), vec(LANES), vec(LANES), vec(di), vec(di)],
        out_specs=blk(di),
        out_shape=jax.ShapeDtypeStruct((t, di), F32),
        scratch_shapes=[pltpu.VMEM((groups, di // groups, MB_STATE), F32)],
        compiler_params=_params(("arbitrary",)),
        name="mb_ssd",
    )(xs, bm, cm, dt, z, dt_bias, a_log, d_skip, norm_g)


def _mamba2_mix(h, ln_g, ln_b, w_in, conv_w, conv_b, dt_bias, a_log, d_skip, norm_g, w_out, wr, br):
    t, d = h.shape
    heads = dt_bias.shape[0]
    di = heads * MB_HEADDIM
    gn = MB_GROUPS * MB_STATE
    cd = di + 2 * gn
    w_bf = w_in.astype(BF16)
    z = _mm(h, w_bf[:, :di], name="mb_in_z")
    xbc = _mm(h, w_bf[:, di:di + cd], name="mb_in_xbc")
    dt = _mm(h, _pad_cols(w_bf[:, di + cd:], LANES), name="mb_in_dt")
    xs, bm, cm = _mb_conv(xbc, conv_w, conv_b, di, gn)
    pad_h = lambda v_: _pad_cols(_row(v_), LANES)
    y = _mb_ssd(xs, bm, cm, dt, z, pad_h(dt_bias), pad_h(a_log),
                _row(jnp.repeat(d_skip, MB_HEADDIM)), _row(norm_g))
    return _mm_res_ln(y, w_out.astype(BF16), h, _row(ln_g), _row(ln_b), wr, br)


def _gla_kernel(qk_ref, vg_ref, lr_ref, w2_ref, b2_ref, ng_ref, o_ref, s_ref, *, chunks):
    c = pl.program_id(0)

    @pl.when(c == 0)
    def _():
        s_ref[...] = jnp.zeros_like(s_ref)

    n = GL_CHUNK
    heads, dv, dk = s_ref.shape
    kd = heads * dk
    vd = heads * dv
    tri = _tri(n)
    r2 = lax.broadcasted_iota(jnp.int32, (n, n), 0)
    c2 = lax.broadcasted_iota(jnp.int32, (n, n), 1)
    incl = c2 <= r2
    for j in range(chunks):
        rs = slice(n * j, n * (j + 1))
        pre = _dot(lr_ref[rs, :], w2_ref[...]) + b2_ref[...]
        log_alpha = (jnp.minimum(pre, 0.0) - jnp.log(1.0 + jnp.exp(-jnp.abs(pre)))) * (1.0 / GL_GATE_NORM)
        b_all = _dot_f32(tri, log_alpha)
        for hd in range(heads):
            q = qk_ref[rs, dk * hd:dk * (hd + 1)] * (dk ** -0.5)
            k = qk_ref[rs, kd + dk * hd:kd + dk * (hd + 1)]
            v = vg_ref[rs, dv * hd:dv * (hd + 1)]
            gate = vg_ref[rs, vd + dv * hd:vd + dv * (hd + 1)]
            b = b_all[:, dk * hd:dk * (hd + 1)]
            b_end = b[n - 1:n, :]
            q_in = q * jnp.exp(b)
            k_in = k * jnp.exp(-b)
            k_end = k * jnp.exp(b_end - b)
            attn = jnp.where(incl, _dot_nt(q_in, k_in), 0.0)
            s = s_ref[hd]
            o = _dot(attn, v) + _dot_nt(q_in, s)
            s_ref[hd] = s * jnp.exp(b_end) + _dot_tn(v, k_end)
            o = o * lax.rsqrt(jnp.mean(o * o, axis=-1, keepdims=True) + GL_NORM_EPS) * ng_ref[...]
            o_ref[rs, dv * hd:dv * (hd + 1)] = o * _silu(gate)


def _gla(qk, vg, lr, w2, b2, norm_g, chunks=4):
    t = qk.shape[0]
    kd = qk.shape[1] // 2
    vd = vg.shape[1] // 2
    tm = GL_CHUNK * chunks
    blk = lambda w: pl.BlockSpec((tm, w), lambda c: (c, 0))
    full = lambda a: pl.BlockSpec(a.shape, lambda c: (0, 0))
    return pl.pallas_call(
        functools.partial(_gla_kernel, chunks=chunks),
        grid=(t // tm,),
        in_specs=[blk(2 * kd), blk(2 * vd), blk(LANES), full(w2), full(b2), full(norm_g)],
        out_specs=blk(vd),
        out_shape=jax.ShapeDtypeStruct((t, vd), F32),
        scratch_shapes=[pltpu.VMEM((GL_HEADS, vd // GL_HEADS, kd // GL_HEADS), F32)],
        compiler_params=_params(("arbitrary",)),
        name="gla",
    )(qk, vg, lr, w2, b2, norm_g)


def _gla_mix(h, ln_g, ln_b, w_in, gk_w2, gk_b, norm_g, w_out, wr, br):
    kd = gk_w2.shape[1]
    vd = w_out.shape[0]
    w_bf = w_in.astype(BF16)
    qk = _mm(h, w_bf[:, :2 * kd], name="gl_in_qk")
    vg = _mm(h, w_bf[:, 2 * kd:2 * kd + 2 * vd], name="gl_in_vg")
    lr = _mm(h, _pad_cols(w_bf[:, 2 * kd + 2 * vd:], LANES), name="gl_in_lr")
    y = _gla(qk, vg, lr, _pad_rows(gk_w2, LANES).astype(BF16), _row(gk_b), _row(norm_g))
    return _mm_res_ln(y, w_out.astype(BF16), h, _row(ln_g), _row(ln_b), wr, br)


def _moe_ffn_kernel(be_ref, nb_ref, x_ref, win_ref, wdn_ref, o_ref):
    b = pl.program_id(0)

    @pl.when(b < nb_ref[0])
    def _():
        hcat = _dot(x_ref[...], win_ref[0])
        ff = hcat.shape[1] // 2
        act = _silu(hcat[:, :ff]) * hcat[:, ff:]
        o_ref[...] = _dot(act, wdn_ref[0])

    @pl.when(b >= nb_ref[0])
    def _():
        o_ref[...] = jnp.zeros_like(o_ref)


def _moe_ffn(block_expert, n_used, xg, w_in, w_down):
    p, d = xg.shape
    ff2 = w_in.shape[2]
    grid_spec = pltpu.PrefetchScalarGridSpec(
        num_scalar_prefetch=2,
        grid=(p // MOE_BLOCK,),
        in_specs=[pl.BlockSpec((MOE_BLOCK, d), lambda b, be, nb: (b, 0)),
                  pl.BlockSpec((1, d, ff2), lambda b, be, nb: (be[b], 0, 0)),
                  pl.BlockSpec((1, ff2 // 2, d), lambda b, be, nb: (be[b], 0, 0))],
        out_specs=pl.BlockSpec((MOE_BLOCK, d), lambda b, be, nb: (b, 0)),
    )
    return pl.pallas_call(
        _moe_ffn_kernel,
        grid_spec=grid_spec,
        out_shape=jax.ShapeDtypeStruct((p, d), F32),
        compiler_params=_params(("arbitrary",)),
        name="moe_ffn",
    )(block_expert, n_used, xg, w_in, w_down)


def _hier_moe(h, logits, ln_g, ln_b, w_in, w_down):
    t, d = h.shape
    n_exp = w_in.shape[0]
    k_top = MOE_TOPK
    group_probs = jax.nn.softmax(logits[:, :MOE_GROUPS], -1)
    p_group, g_idx = lax.top_k(group_probs, 1)
    exp_logits = logits[:, MOE_GROUPS:MOE_GROUPS + n_exp].reshape(t, MOE_GROUPS, MOE_PER_GROUP)
    sel_logits = jnp.take_along_axis(exp_logits, g_idx[:, :, None], axis=1)[:, 0]
    p_exp, e_local = lax.top_k(jax.nn.softmax(sel_logits, -1), k_top)
    gates = p_group * p_exp / jnp.sum(p_exp, -1, keepdims=True)
    expert_id = g_idx * MOE_PER_GROUP + e_local

    n_assign = t * k_top
    flat_e = expert_id.reshape(n_assign).astype(jnp.int32)
    order = jnp.argsort(flat_e)
    se = flat_e[order]
    counts = jnp.bincount(flat_e, length=n_exp).astype(jnp.int32)
    starts = jnp.cumsum(counts) - counts
    padded = (counts + MOE_BLOCK - 1) // MOE_BLOCK * MOE_BLOCK
    pad_end = jnp.cumsum(padded)
    pad_start = pad_end - padded
    dest_sorted = pad_start[se] + (jnp.arange(n_assign, dtype=jnp.int32) - starts[se])
    n_blocks = (n_assign + n_exp * (MOE_BLOCK - 1) + MOE_BLOCK - 1) // MOE_BLOCK
    tok_of_row = jnp.zeros((n_blocks * MOE_BLOCK,), jnp.int32).at[dest_sorted].set(order // k_top)
    dest = jnp.zeros((n_assign,), jnp.int32).at[order].set(dest_sorted).reshape(t, k_top)
    block_expert = jnp.minimum(
        jnp.searchsorted(pad_end, jnp.arange(n_blocks, dtype=jnp.int32) * MOE_BLOCK, side='right'),
        n_exp - 1).astype(jnp.int32)
    n_used = (pad_end[-1:] // MOE_BLOCK).astype(jnp.int32)

    xg = jnp.take(h, tok_of_row, axis=0)
    y_buf = _moe_ffn(block_expert, n_used, xg, w_in, w_down)
    y0 = jnp.take(y_buf, dest[:, 0], axis=0)
    y1 = jnp.take(y_buf, dest[:, 1], axis=0)
    return _combine_ln(y0, y1, gates.astype(F32), h, _row(ln_g), _row(ln_b))


def kernel(x, ln_g, ln_b, rw_mu, rw_w_rkv, rw_w0, rw_w1, rw_w2, rw_a0, rw_a1, rw_a2, rw_g1, rw_g2, rw_k_k, rw_k_a, rw_r_k, rw_lnx_g, rw_lnx_b, rw_w_o, mb_w_in, mb_conv_w, mb_conv_b, mb_dt_bias, mb_a_log, mb_d, mb_norm_g, mb_w_out, gl_w_in, gl_gk_w2, gl_gk_b, gl_norm_g, gl_w_out, moe_w_group, moe_b_group, moe_w_route, moe_b_route, moe_w_in, moe_w_down):
    bsz, seq, d = x.shape
    assert bsz == 1
    h = x.reshape(seq, d)
    depth = ln_g.shape[0]
    for i in range(depth):
        kind, j = i % 3, i // 3
        wr = _pad_cols(jnp.concatenate([moe_w_group[i], moe_w_route[i]], axis=1), LANES)
        br = _pad_cols(_row(jnp.concatenate([moe_b_group[i], moe_b_route[i]])), LANES)
        if kind == 0:
            h, logits = _rwkv7_mix(h, ln_g[i, 0], ln_b[i, 0], rw_mu[j], rw_w_rkv[j], rw_w0[j], rw_w1[j],
                                   rw_w2[j], rw_a0[j], rw_a1[j], rw_a2[j], rw_g1[j], rw_g2[j], rw_k_k[j],
                                   rw_k_a[j], rw_r_k[j], rw_lnx_g[j], rw_lnx_b[j], rw_w_o[j], wr, br)
        elif kind == 1:
            h, logits = _mamba2_mix(h, ln_g[i, 0], ln_b[i, 0], mb_w_in[j], mb_conv_w[j], mb_conv_b[j],
                                    mb_dt_bias[j], mb_a_log[j], mb_d[j], mb_norm_g[j], mb_w_out[j], wr, br)
        else:
            h, logits = _gla_mix(h, ln_g[i, 0], ln_b[i, 0], gl_w_in[j], gl_gk_w2[j], gl_gk_b[j],
                                 gl_norm_g[j], gl_w_out[j], wr, br)
        h = _hier_moe(h, logits, ln_g[i, 1], ln_b[i, 1], moe_w_in[i], moe_w_down[i])
    return h.reshape(bsz, seq, d)
```

```python
import functools

import jax
import jax.numpy as jnp
from jax import lax
from jax.experimental import pallas as pl
from jax.experimental.pallas import tpu as pltpu

F32 = jnp.float32
BF16 = jnp.bfloat16

LANES = 128
SUBLANES = 8
VMEM_LIMIT = 56 * 1024 * 1024

DEPTH = 4
ALPHA = (2.0 * DEPTH) ** 0.25
LN_EPS = 1e-5

RW_HEAD = 64
RW_CHUNK = 64
RW_GN_EPS = 64e-5

MB_HEADDIM = 64
MB_GROUPS = 4
MB_STATE = 128
MB_CHUNK = 128
MB_CONV = 4
MB_NORM_EPS = 1e-5

GL_HEADS = 4
GL_CHUNK = 64
GL_GATE_NORM = 16.0
GL_NORM_EPS = 1e-5

MOE_GROUPS = 4
MOE_PER_GROUP = 8
MOE_TOPK = 2
MOE_BLOCK = 256


def _dot(a, b):
    return jnp.dot(a.astype(BF16), b.astype(BF16), preferred_element_type=F32)


def _dot_nt(a, b):
    return lax.dot_general(a.astype(BF16), b.astype(BF16), (((1,), (1,)), ((), ())),
                           preferred_element_type=F32)


def _dot_tn(a, b):
    return lax.dot_general(a.astype(BF16), b.astype(BF16), (((0,), (0,)), ((), ())),
                           preferred_element_type=F32)


def _dot_f32(a, b):
    return jnp.dot(a, b, preferred_element_type=F32, precision=lax.Precision.HIGHEST)


def _sigmoid(x):
    return 1.0 / (1.0 + jnp.exp(-x))


def _softplus(x):
    return jnp.maximum(x, 0.0) + jnp.log(1.0 + jnp.exp(-jnp.abs(x)))


def _silu(x):
    return x * _sigmoid(x)


def _tri(n, dtype=F32):
    r = lax.broadcasted_iota(jnp.int32, (n, n), 0)
    c = lax.broadcasted_iota(jnp.int32, (n, n), 1)
    return (c <= r).astype(dtype)


def _params(sem):
    return pltpu.CompilerParams(dimension_semantics=sem, vmem_limit_bytes=VMEM_LIMIT)


def _pad_cols(w, n):
    return jnp.pad(w, ((0, 0), (0, n - w.shape[1])))


def _pad_rows(w, n):
    return jnp.pad(w, ((0, n - w.shape[0]), (0, 0)))


def _row(v):
    return v.reshape(1, -1).astype(F32)


def _prev_rows_spec(tm, width):
    per = tm // SUBLANES
    return pl.BlockSpec((SUBLANES, width), lambda i: (jnp.maximum(i * per - 1, 0), 0))


def _mm_kernel(x_ref, w_ref, o_ref):
    o_ref[...] = _dot(x_ref[...], w_ref[...])


def _mm(x, w, tm=512, tn=None, name="mm"):
    t, k = x.shape
    n = w.shape[1]
    tn = tn or min(n, 1024)
    return pl.pallas_call(
        _mm_kernel,
        grid=(t // tm, n // tn),
        in_specs=[pl.BlockSpec((tm, k), lambda i, j: (i, 0)),
                  pl.BlockSpec((k, tn), lambda i, j: (0, j))],
        out_specs=pl.BlockSpec((tm, tn), lambda i, j: (i, j)),
        out_shape=jax.ShapeDtypeStruct((t, n), F32),
        compiler_params=_params(("parallel", "parallel")),
        name=name,
    )(x, w)


def _layer_norm(z, g, b):
    mu = jnp.mean(z, axis=-1, keepdims=True)
    zc = z - mu
    var = jnp.mean(zc * zc, axis=-1, keepdims=True)
    return zc * lax.rsqrt(var + LN_EPS) * g + b


def _mm_res_ln_kernel(x_ref, w_ref, res_ref, g_ref, b_ref, wr_ref, br_ref, o_ref, lg_ref):
    y = _dot(x_ref[...], w_ref[...])
    h = _layer_norm(ALPHA * res_ref[...] + y, g_ref[...], b_ref[...])
    o_ref[...] = h
    lg_ref[...] = lax.dot_general(wr_ref[...], h, (((1,), (1,)), ((), ())), preferred_element_type=F32,
                                  precision=lax.Precision.HIGHEST) + br_ref[...]


def _mm_res_ln(x, w, res, g, b, wr, br, tm=512):
    t, k = x.shape
    d = w.shape[1]
    nr = wr.shape[0]
    return pl.pallas_call(
        _mm_res_ln_kernel,
        grid=(t // tm,),
        in_specs=[pl.BlockSpec((tm, k), lambda i: (i, 0)),
                  pl.BlockSpec((k, d), lambda i: (0, 0)),
                  pl.BlockSpec((tm, d), lambda i: (i, 0)),
                  pl.BlockSpec((1, d), lambda i: (0, 0)),
                  pl.BlockSpec((1, d), lambda i: (0, 0)),
                  pl.BlockSpec((nr, d), lambda i: (0, 0)),
                  pl.BlockSpec((nr, 1), lambda i: (0, 0))],
        out_specs=[pl.BlockSpec((tm, d), lambda i: (i, 0)),
                   pl.BlockSpec((nr, tm), lambda i: (0, i))],
        out_shape=[jax.ShapeDtypeStruct((t, d), F32), jax.ShapeDtypeStruct((nr, t), F32)],
        compiler_params=_params(("parallel",)),
        name="mm_res_ln",
    )(x, w, res, g, b, wr, br)


def _rwkv_proj_kernel(x_ref, xp_ref, mu_ref, wrkv_ref, w1_ref, w2_ref, w0_ref, a1_ref, a2_ref, a0_ref,
                      g1_ref, g2_ref, r_ref, k_ref, v_ref, lw_ref, a_ref, g_ref):
    i = pl.program_id(0)
    x = x_ref[...]
    prev = jnp.where(i > 0, xp_ref[SUBLANES - 1:SUBLANES, :], 0.0)
    rows = lax.broadcasted_iota(jnp.int32, x.shape, 0)
    shifted = jnp.where(rows == 0, prev, pltpu.roll(x, 1, 0))
    xx = shifted - x

    def mix(p):
        return x + xx * mu_ref[p:p + 1, :]

    r_ref[...] = _dot(mix(0), wrkv_ref[0])
    k_ref[...] = _dot(mix(1), wrkv_ref[1])
    v_ref[...] = _dot(mix(2), wrkv_ref[2])
    w_pre = w0_ref[...] + _dot(jnp.tanh(_dot(mix(3), w1_ref[...])), w2_ref[...])
    lw_ref[...] = -jnp.exp(-_softplus(-w_pre) - 0.5)
    a_ref[...] = _sigmoid(a0_ref[...] + _dot(_dot(mix(4), a1_ref[...]), a2_ref[...]))
    g_ref[...] = _dot(_sigmoid(_dot(mix(5), g1_ref[...])), g2_ref[...])


def _rwkv_proj(x, mu, w_rkv, w1, w2, w0, a1, a2, a0, g1, g2, tm=256):
    t, d = x.shape
    full = lambda a: pl.BlockSpec(a.shape, lambda i: (0,) * a.ndim)
    row = pl.BlockSpec((tm, d), lambda i: (i, 0))
    args = (mu, w_rkv, w1, w2, w0, a1, a2, a0, g1, g2)
    return pl.pallas_call(
        _rwkv_proj_kernel,
        grid=(t // tm,),
        in_specs=[row, _prev_rows_spec(tm, d)] + [full(a) for a in args],
        out_specs=[row] * 6,
        out_shape=[jax.ShapeDtypeStruct((t, d), F32)] * 6,
        compiler_params=_params(("parallel",)),
        name="rwkv_proj",
    )(x, x, *args)


def _rwkv_rec_kernel(r_ref, k_ref, v_ref, lw_ref, a_ref, g_ref, kk_ref, ka_ref, rk_ref, lg_ref, lb_ref,
                     o_ref, s_ref):
    c = pl.program_id(0)

    @pl.when(c == 0)
    def _():
        s_ref[...] = jnp.zeros_like(s_ref)

    n = RW_CHUNK
    cs_all = _dot_f32(_tri(n), lw_ref[...])
    lane = lax.broadcasted_iota(jnp.int32, (n, LANES), 1)
    first = lane < RW_HEAD
    r2 = lax.broadcasted_iota(jnp.int32, (2 * n, LANES), 0)
    c2 = lax.broadcasted_iota(jnp.int32, (2 * n, LANES), 1)
    own = (r2 < n) == (c2 < RW_HEAD)
    strict = c2 < r2
    incl = c2 <= r2
    eye = (c2 == r2).astype(F32)

    def stack(x):
        return jnp.where(own, jnp.concatenate([x, x], axis=0), 0.0)

    def head_sum(x):
        sa = jnp.sum(jnp.where(first, x, 0.0), axis=1, keepdims=True)
        sb = jnp.sum(jnp.where(first, 0.0, x), axis=1, keepdims=True)
        return jnp.where(first, sa, sb)

    for p in range(r_ref.shape[1] // LANES):
        sl = slice(LANES * p, LANES * (p + 1))
        r = r_ref[:, sl]
        k = k_ref[:, sl]
        v = v_ref[:, sl]
        lw = lw_ref[:, sl]
        a = a_ref[:, sl]
        cs = cs_all[:, sl]
        cs_end = cs[n - 1:n, :]

        kk = k * kk_ref[:, sl]
        kk = kk / jnp.maximum(jnp.sqrt(head_sum(kk * kk)), 1e-12)
        kh = k * (1.0 + (a - 1.0) * ka_ref[:, sl])
        kka = kk * a
        g_inv = jnp.exp(-cs)
        g_end = jnp.exp(cs_end - cs)

        at = stack(-kk * jnp.exp(cs - lw))
        bt = stack(kka * g_inv)
        kt = stack(kh * g_inv)
        rt = stack(r * jnp.exp(cs))
        bh = stack(kka * g_end)
        khat = stack(kh * g_end)
        vs = stack(v)

        nm = jnp.where(strict, _dot_nt(at, bt), 0.0)
        aak = jnp.where(strict, _dot_nt(at, kt), 0.0)
        arb = jnp.where(incl, _dot_nt(rt, bt), 0.0)
        ark = jnp.where(incl, _dot_nt(rt, kt), 0.0)

        tm = eye + nm
        pw = nm
        for _ in range(5):
            pw = _dot(pw, pw)
            tm = tm + _dot(tm, pw)

        s = s_ref[p]
        u = _dot(tm, _dot_nt(at, s) + _dot(aak, vs))
        y = _dot_nt(rt, s) + _dot(arb, u) + _dot(ark, vs)
        s_ref[p] = s * jnp.exp(cs_end) + _dot_tn(u, bh) + _dot_tn(vs, khat)

        mu = jnp.sum(y, axis=1, keepdims=True) * (1.0 / RW_HEAD)
        yc = jnp.where(own, y - mu, 0.0)
        var = jnp.sum(yc * yc, axis=1, keepdims=True) * (1.0 / RW_HEAD)
        yn = yc * lax.rsqrt(var + RW_GN_EPS)
        yn = yn[:n] + yn[n:]
        bonus = head_sum(r * kh * rk_ref[:, sl]) * v
        o_ref[:, sl] = (yn * lg_ref[:, sl] + lb_ref[:, sl] + bonus) * g_ref[:, sl]


def _rwkv_rec(r, k, v, lw, a, g, k_k, k_a, r_k, lnx_g, lnx_b):
    t, d = r.shape
    row = pl.BlockSpec((RW_CHUNK, d), lambda c: (c, 0))
    vec = pl.BlockSpec((1, d), lambda c: (0, 0))
    return pl.pallas_call(
        _rwkv_rec_kernel,
        grid=(t // RW_CHUNK,),
        in_specs=[row] * 6 + [vec] * 5,
        out_specs=row,
        out_shape=jax.ShapeDtypeStruct((t, d), F32),
        scratch_shapes=[pltpu.VMEM((d // LANES, LANES, LANES), F32)],
        compiler_params=_params(("arbitrary",)),
        name="rwkv_rec",
    )(r, k, v, lw, a, g, k_k, k_a, r_k, lnx_g, lnx_b)


def _rwkv7_mix(h, ln_g, ln_b, mu, w_rkv, w0, w1, w2, a0, a1, a2, g1, g2, k_k, k_a, r_k, lnx_g, lnx_b, w_o,
               wr, br):
    t, d = h.shape
    lora = w1.shape[1]
    lp = -(-lora // LANES) * LANES
    gl = g1.shape[1]
    gp = -(-gl // LANES) * LANES
    r, k, v, lw, a, g = _rwkv_proj(
        h, _pad_rows(mu, SUBLANES), w_rkv.astype(BF16),
        _pad_cols(w1, lp).astype(BF16), _pad_rows(w2, lp).astype(BF16), _row(w0),
        _pad_cols(a1, lp).astype(BF16), _pad_rows(a2, lp).astype(BF16), _row(a0),
        _pad_cols(g1, gp).astype(BF16), _pad_rows(g2, gp).astype(BF16))
    y = _rwkv_rec(r, k, v, lw, a, g, _row(k_k), _row(k_a), _row(r_k), _row(lnx_g), _row(lnx_b))
    return _mm_res_ln(y, w_o.astype(BF16), h, _row(ln_g), _row(ln_b), wr, br)


def _mb_conv_kernel(x_ref, xp_ref, w_ref, b_ref, xs_ref, bm_ref, cm_ref, buf_ref):
    i = pl.program_id(0)
    tm = x_ref.shape[0]
    x = x_ref[...]
    buf_ref[0:SUBLANES, :] = jnp.where(i > 0, xp_ref[...], 0.0)
    buf_ref[SUBLANES:, :] = x
    acc = b_ref[...] + x * w_ref[MB_CONV - 1:MB_CONV, :]
    for s in range(1, MB_CONV):
        acc = acc + buf_ref[pl.ds(SUBLANES - s, tm), :] * w_ref[MB_CONV - 1 - s:MB_CONV - s, :]
    y = _silu(acc)
    di = xs_ref.shape[1]
    gn = bm_ref.shape[1]
    xs_ref[...] = y[:, :di]
    bm_ref[...] = y[:, di:di + gn]
    cm_ref[...] = y[:, di + gn:]


def _mb_conv(xbc, conv_w, conv_b, di, gn, tm=256):
    t, cd = xbc.shape
    return pl.pallas_call(
        _mb_conv_kernel,
        grid=(t // tm,),
        in_specs=[pl.BlockSpec((tm, cd), lambda i: (i, 0)),
                  _prev_rows_spec(tm, cd),
                  pl.BlockSpec((SUBLANES, cd), lambda i: (0, 0)),
                  pl.BlockSpec((1, cd), lambda i: (0, 0))],
        out_specs=[pl.BlockSpec((tm, di), lambda i: (i, 0)),
                   pl.BlockSpec((tm, gn), lambda i: (i, 0)),
                   pl.BlockSpec((tm, gn), lambda i: (i, 0))],
        out_shape=[jax.ShapeDtypeStruct((t, di), F32),
                   jax.ShapeDtypeStruct((t, gn), F32),
                   jax.ShapeDtypeStruct((t, gn), F32)],
        scratch_shapes=[pltpu.VMEM((tm + SUBLANES, cd), F32)],
        compiler_params=_params(("parallel",)),
        name="mb_conv",
    )(xbc, xbc, _pad_rows(conv_w, SUBLANES), _row(conv_b))


def _mb_ssd_kernel(xs_ref, bm_ref, cm_ref, dt_ref, z_ref, dtb_ref, alog_ref, dsk_ref, ng_ref, o_ref, s_ref):
    c = pl.program_id(0)

    @pl.when(c == 0)
    def _():
        s_ref[...] = jnp.zeros_like(s_ref)

    n = MB_CHUNK
    p_dim = MB_HEADDIM
    groups = s_ref.shape[0]
    hpg = s_ref.shape[1] // p_dim
    dt = _softplus(dt_ref[...] + dtb_ref[...])
    da = dt * (-jnp.exp(alog_ref[...]))
    cs = _dot_f32(_tri(n), da)
    cs_t = cs.T
    cs_end = cs[n - 1:n, :]
    to_end = jnp.exp(cs_end - cs)
    from_start = jnp.exp(cs)
    chunk_decay = jnp.exp(cs_end)
    r2 = lax.broadcasted_iota(jnp.int32, (n, n), 0)
    c2 = lax.broadcasted_iota(jnp.int32, (n, n), 1)
    incl = c2 <= r2

    gw = hpg * p_dim
    for g in range(groups):
        bg = bm_ref[:, MB_STATE * g:MB_STATE * (g + 1)]
        cg = cm_ref[:, MB_STATE * g:MB_STATE * (g + 1)]
        scores = _dot_nt(cg, bg)
        s_g = s_ref[g]
        y_off = _dot_nt(cg, s_g)
        y_parts = []
        xe_parts = []
        cd_parts = []
        for e in range(hpg):
            hd = g * hpg + e
            x_h = xs_ref[:, p_dim * hd:p_dim * (hd + 1)]
            seg = jnp.minimum(cs[:, hd:hd + 1] - cs_t[hd:hd + 1, :], 0.0)
            m = scores * jnp.where(incl, jnp.exp(seg), 0.0)
            xdt = x_h * dt[:, hd:hd + 1]
            y_h = (_dot(m, xdt) + y_off[:, p_dim * e:p_dim * (e + 1)] * from_start[:, hd:hd + 1]
                   + x_h * dsk_ref[:, p_dim * hd:p_dim * (hd + 1)])
            y_parts.append(y_h)
            xe_parts.append(xdt * to_end[:, hd:hd + 1])
            cd_parts.append(jnp.broadcast_to(chunk_decay[:, hd:hd + 1], (p_dim, 1)))
        s_ref[g] = s_g * jnp.concatenate(cd_parts, axis=0) + _dot_tn(jnp.concatenate(xe_parts, axis=1), bg)
        y_g = jnp.concatenate(y_parts, axis=1) * _silu(z_ref[:, gw * g:gw * (g + 1)])
        ms = jnp.mean(y_g * y_g, axis=-1, keepdims=True)
        o_ref[:, gw * g:gw * (g + 1)] = y_g * lax.rsqrt(ms + MB_NORM_EPS) * ng_ref[:, gw * g:gw * (g + 1)]


def _mb_ssd(xs, bm, cm, dt, z, dt_bias, a_log, d_skip, norm_g):
    t, di = xs.shape
    gn = bm.shape[1]
    groups = gn // MB_STATE
    blk = lambda w: pl.BlockSpec((MB_CHUNK, w), lambda c: (c, 0))
    vec = lambda w: pl.BlockSpec((1, w), lambda c: (0, 0))
    return pl.pallas_call(
        _mb_ssd_kernel,
        grid=(t // MB_CHUNK,),
        in_specs=[blk(di), blk(gn), blk(gn), blk(LANES), blk(di), vec(LANES), vec(LANES), vec(di), vec(di)],
        out_specs=blk(di),
        out_shape=jax.ShapeDtypeStruct((t, di), F32),
        scratch_shapes=[pltpu.VMEM((groups, di // groups, MB_STATE), F32)],
        compiler_params=_params(("arbitrary",)),
        name="mb_ssd",
    )(xs, bm, cm, dt, z, dt_bias, a_log, d_skip, norm_g)


def _mamba2_mix(h, ln_g, ln_b, w_in, conv_w, conv_b, dt_bias, a_log, d_skip, norm_g, w_out, wr, br):
    t, d = h.shape
    heads = dt_bias.shape[0]
    di = heads * MB_HEADDIM
    gn = MB_GROUPS * MB_STATE
    cd = di + 2 * gn
    w_bf = w_in.astype(BF16)
    z = _mm(h, w_bf[:, :di], name="mb_in_z")
    xbc = _mm(h, w_bf[:, di:di + cd], name="mb_in_xbc")
    dt = _mm(h, _pad_cols(w_bf[:, di + cd:], LANES), name="mb_in_dt")
    xs, bm, cm = _mb_conv(xbc, conv_w, conv_b, di, gn)
    pad_h = lambda v_: _pad_cols(_row(v_), LANES)
    y = _mb_ssd(xs, bm, cm, dt, z, pad_h(dt_bias), pad_h(a_log),
                _row(jnp.repeat(d_skip, MB_HEADDIM)), _row(norm_g))
    return _mm_res_ln(y, w_out.astype(BF16), h, _row(ln_g), _row(ln_b), wr, br)


def _gla_kernel(qk_ref, vg_ref, lr_ref, w2_ref, b2_ref, ng_ref, o_ref, s_ref, *, chunks):
    c = pl.program_id(0)

    @pl.when(c == 0)
    def _():
        s_ref[...] = jnp.zeros_like(s_ref)

    n = GL_CHUNK
    heads, dv, dk = s_ref.shape
    kd = heads * dk
    vd = heads * dv
    tri = _tri(n)
    r2 = lax.broadcasted_iota(jnp.int32, (n, n), 0)
    c2 = lax.broadcasted_iota(jnp.int32, (n, n), 1)
    incl = c2 <= r2
    for j in range(chunks):
        rs = slice(n * j, n * (j + 1))
        pre = _dot(lr_ref[rs, :], w2_ref[...]) + b2_ref[...]
        log_alpha = (jnp.minimum(pre, 0.0) - jnp.log(1.0 + jnp.exp(-jnp.abs(pre)))) * (1.0 / GL_GATE_NORM)
        b_all = _dot_f32(tri, log_alpha)
        for hd in range(heads):
            q = qk_ref[rs, dk * hd:dk * (hd + 1)] * (dk ** -0.5)
            k = qk_ref[rs, kd + dk * hd:kd + dk * (hd + 1)]
            v = vg_ref[rs, dv * hd:dv * (hd + 1)]
            gate = vg_ref[rs, vd + dv * hd:vd + dv * (hd + 1)]
            b = b_all[:, dk * hd:dk * (hd + 1)]
            b_end = b[n - 1:n, :]
            q_in = q * jnp.exp(b)
            k_in = k * jnp.exp(-b)
            k_end = k * jnp.exp(b_end - b)
            attn = jnp.where(incl, _dot_nt(q_in, k_in), 0.0)
            s = s_ref[hd]
            o = _dot(attn, v) + _dot_nt(q_in, s)
            s_ref[hd] = s * jnp.exp(b_end) + _dot_tn(v, k_end)
            o = o * lax.rsqrt(jnp.mean(o * o, axis=-1, keepdims=True) + GL_NORM_EPS) * ng_ref[...]
            o_ref[rs, dv * hd:dv * (hd + 1)] = o * _silu(gate)


def _gla(qk, vg, lr, w2, b2, norm_g, chunks=4):
    t = qk.shape[0]
    kd = qk.shape[1] // 2
    vd = vg.shape[1] // 2
    tm = GL_CHUNK * chunks
    blk = lambda w: pl.BlockSpec((tm, w), lambda c: (c, 0))
    full = lambda a: pl.BlockSpec(a.shape, lambda c: (0, 0))
    return pl.pallas_call(
        functools.partial(_gla_kernel, chunks=chunks),
        grid=(t // tm,),
        in_specs=[blk(2 * kd), blk(2 * vd), blk(LANES), full(w2), full(b2), full(norm_g)],
        out_specs=blk(vd),
        out_shape=jax.ShapeDtypeStruct((t, vd), F32),
        scratch_shapes=[pltpu.VMEM((GL_HEADS, vd // GL_HEADS, kd // GL_HEADS), F32)],
        compiler_params=_params(("arbitrary",)),
        name="gla",
    )(qk, vg, lr, w2, b2, norm_g)


def _gla_mix(h, ln_g, ln_b, w_in, gk_w2, gk_b, norm_g, w_out, wr, br):
    kd = gk_w2.shape[1]
    vd = w_out.shape[0]
    w_bf = w_in.astype(BF16)
    qk = _mm(h, w_bf[:, :2 * kd], name="gl_in_qk")
    vg = _mm(h, w_bf[:, 2 * kd:2 * kd + 2 * vd], name="gl_in_vg")
    lr = _mm(h, _pad_cols(w_bf[:, 2 * kd + 2 * vd:], LANES), name="gl_in_lr")
    y = _gla(qk, vg, lr, _pad_rows(gk_w2, LANES).astype(BF16), _row(gk_b), _row(norm_g))
    return _mm_res_ln(y, w_out.astype(BF16), h, _row(ln_g), _row(ln_b), wr, br)


def _moe_ffn_kernel(be_ref, nb_ref, x_ref, win_ref, wdn_ref, o_ref):
    b = pl.program_id(0)

    @pl.when(b < nb_ref[0])
    def _():
        hcat = _dot(x_ref[...], win_ref[0])
        ff = hcat.shape[1] // 2
        act = _silu(hcat[:, :ff]) * hcat[:, ff:]
        o_ref[...] = _dot(act, wdn_ref[0])

    @pl.when(b >= nb_ref[0])
    def _():
        o_ref[...] = jnp.zeros_like(o_ref)


def _moe_ffn(block_expert, n_used, xg, w_in, w_down):
    p, d = xg.shape
    ff2 = w_in.shape[2]
    grid_spec = pltpu.PrefetchScalarGridSpec(
        num_scalar_prefetch=2,
        grid=(p // MOE_BLOCK,),
        in_specs=[pl.BlockSpec((MOE_BLOCK, d), lambda b, be, nb: (b, 0)),
                  pl.BlockSpec((1, d, ff2), lambda b, be, nb: (be[b], 0, 0)),
                  pl.BlockSpec((1, ff2 // 2, d), lambda b, be, nb: (be[b], 0, 0))],
        out_specs=pl.BlockSpec((MOE_BLOCK, d), lambda b, be, nb: (b, 0)),
    )
    return pl.pallas_call(
        _moe_ffn_kernel,
        grid_spec=grid_spec,
        out_shape=jax.ShapeDtypeStruct((p, d), F32),
        compiler_params=_params(("arbitrary",)),
        name="moe_ffn",
    )(block_expert, n_used, xg, w_in, w_down)


def _first_argmax(x, rows):
    m = jnp.max(x, axis=0, keepdims=True)
    idx = jnp.min(jnp.where(x == m, rows, x.shape[0]), axis=0, keepdims=True)
    return m, idx


def _softmax_rows(x):
    e = jnp.exp(x - jnp.max(x, axis=0, keepdims=True))
    return e / jnp.sum(e, axis=0, keepdims=True)


def _route_kernel(lg_ref, ids_ref, gt_ref, cnt_ref, carry_ref):
    i = pl.program_id(0)

    @pl.when(i == 0)
    def _():
        carry_ref[...] = jnp.zeros_like(carry_ref)

    tm = lg_ref.shape[1]
    epg = MOE_PER_GROUP
    n_exp = MOE_GROUPS * epg
    rows8 = lax.broadcasted_iota(jnp.int32, (SUBLANES, tm), 0)
    group_probs = _softmax_rows(lg_ref[0:SUBLANES, :])
    p_group, g_idx = _first_argmax(group_probs, rows8)
    sel = jnp.zeros((epg, tm), F32)
    for gi in range(MOE_GROUPS):
        sel = sel + jnp.where(g_idx == gi, lg_ref[SUBLANES + epg * gi:SUBLANES + epg * (gi + 1), :], 0.0)
    p_exp = _softmax_rows(sel)
    p1, i1 = _first_argmax(p_exp, rows8)
    p2, i2 = _first_argmax(jnp.where(rows8 == i1, -1.0, p_exp), rows8)
    denom = p1 + p2
    e0 = g_idx * epg + i1
    e1 = g_idx * epg + i2

    rows_e = lax.broadcasted_iota(jnp.int32, (n_exp, tm), 0)
    oh0 = (rows_e == e0).astype(F32)
    oh1 = (rows_e == e1).astype(F32)
    oh = oh0 + oh1
    tr = lax.broadcasted_iota(jnp.int32, (tm, tm), 0)
    tc = lax.broadcasted_iota(jnp.int32, (tm, tm), 1)
    before = _dot(oh, (tr < tc).astype(F32))
    base = carry_ref[:, 0:1] + before
    rank0 = jnp.sum(oh0 * base, axis=0, keepdims=True)
    rank1 = jnp.sum(oh1 * base, axis=0, keepdims=True)
    carry_ref[...] = carry_ref[...] + jnp.sum(oh, axis=1, keepdims=True)
    cnt_ref[...] = carry_ref[...]

    zi = jnp.zeros((SUBLANES - 4, tm), jnp.int32)
    ids_ref[...] = jnp.concatenate([e0, e1, rank0.astype(jnp.int32), rank1.astype(jnp.int32), zi], axis=0)
    zf = jnp.zeros((SUBLANES - 2, tm), F32)
    gt_ref[...] = jnp.concatenate([p_group * p1 / denom, p_group * p2 / denom, zf], axis=0)


def _route(logits_t, tm=512):
    nr, t = logits_t.shape
    n_exp = MOE_GROUPS * MOE_PER_GROUP
    return pl.pallas_call(
        _route_kernel,
        grid=(t // tm,),
        in_specs=[pl.BlockSpec((nr, tm), lambda i: (0, i))],
        out_specs=[pl.BlockSpec((SUBLANES, tm), lambda i: (0, i)),
                   pl.BlockSpec((SUBLANES, tm), lambda i: (0, i)),
                   pl.BlockSpec((n_exp, LANES), lambda i: (0, 0))],
        out_shape=[jax.ShapeDtypeStruct((SUBLANES, t), jnp.int32),
                   jax.ShapeDtypeStruct((SUBLANES, t), F32),
                   jax.ShapeDtypeStruct((n_exp, LANES), F32)],
        scratch_shapes=[pltpu.VMEM((n_exp, LANES), F32)],
        compiler_params=_params(("arbitrary",)),
        name="moe_route",
    )(logits_t)


def _row_copy(src_ref, src_row, dst_ref, dst_row, sem):
    return pltpu.make_async_copy(src_ref.at[pl.ds(src_row, 1)], dst_ref.at[pl.ds(dst_row, 1)], sem)


def _dispatch_kernel(idx_ref, x_ref, buf_in_ref, buf_ref, sem):
    del buf_in_ref
    tm = x_ref.shape[0]

    def issue(t, carry):
        _row_copy(x_ref, t, buf_ref, idx_ref[0, 0, t], sem).start()
        _row_copy(x_ref, t, buf_ref, idx_ref[0, 0, tm + t], sem).start()
        return carry

    def drain(t, carry):
        _row_copy(x_ref, 0, buf_ref, 0, sem).wait()
        return carry

    lax.fori_loop(0, tm, issue, 0, unroll=8)
    lax.fori_loop(0, MOE_TOPK * tm, drain, 0, unroll=8)


def _dispatch(idx, h, n_rows, tm):
    t, d = h.shape
    return pl.pallas_call(
        _dispatch_kernel,
        grid=(t // tm,),
        in_specs=[pl.BlockSpec((1, 1, MOE_TOPK * tm), lambda i: (i, 0, 0), memory_space=pltpu.SMEM),
                  pl.BlockSpec((tm, d), lambda i: (i, 0)),
                  pl.BlockSpec(memory_space=pl.ANY)],
        out_specs=pl.BlockSpec(memory_space=pl.ANY),
        out_shape=jax.ShapeDtypeStruct((n_rows, d), F32),
        scratch_shapes=[pltpu.SemaphoreType.DMA(())],
        input_output_aliases={2: 0},
        compiler_params=_params(("arbitrary",)),
        name="moe_dispatch",
    )(idx, h, jnp.zeros((n_rows, d), F32))


def _combine_ln_kernel(idx_ref, y_ref, gt_ref, res_ref, g_ref, b_ref, o_ref, y0_ref, y1_ref, sem):
    tm = res_ref.shape[0]

    def issue(t, carry):
        _row_copy(y_ref, idx_ref[0, 0, t], y0_ref, t, sem).start()
        _row_copy(y_ref, idx_ref[0, 0, tm + t], y1_ref, t, sem).start()
        return carry

    def drain(t, carry):
        _row_copy(y_ref, 0, y0_ref, 0, sem).wait()
        return carry

    lax.fori_loop(0, tm, issue, 0, unroll=8)
    gates = jnp.concatenate([gt_ref[...], jnp.zeros((LANES - SUBLANES, tm), F32)], axis=0).T
    lax.fori_loop(0, MOE_TOPK * tm, drain, 0, unroll=8)
    y = y0_ref[...] * gates[:, 0:1] + y1_ref[...] * gates[:, 1:2]
    o_ref[...] = _layer_norm(ALPHA * res_ref[...] + y, g_ref[...], b_ref[...])


def _combine_ln(idx, y_buf, gates, res, g, b, tm):
    t, d = res.shape
    row = pl.BlockSpec((tm, d), lambda i: (i, 0))
    vec = pl.BlockSpec((1, d), lambda i: (0, 0))
    return pl.pallas_call(
        _combine_ln_kernel,
        grid=(t // tm,),
        in_specs=[pl.BlockSpec((1, 1, MOE_TOPK * tm), lambda i: (i, 0, 0), memory_space=pltpu.SMEM),
                  pl.BlockSpec(memory_space=pl.ANY),
                  pl.BlockSpec((SUBLANES, tm), lambda i: (0, i)),
                  row, vec, vec],
        out_specs=row,
        out_shape=jax.ShapeDtypeStruct((t, d), F32),
        scratch_shapes=[pltpu.VMEM((tm, d), F32), pltpu.VMEM((tm, d), F32), pltpu.SemaphoreType.DMA(())],
        compiler_params=_params(("arbitrary",)),
        name="moe_combine_ln",
    )(idx, y_buf, gates, res, g, b)


def _hier_moe(h, logits_t, ln_g, ln_b, w_in, w_down, tm=512):
    t, d = h.shape
    n_exp = w_in.shape[0]
    ids, gates, counts = _route(logits_t)

    counts = counts[:, 0].astype(jnp.int32)
    padded = (counts + MOE_BLOCK - 1) // MOE_BLOCK * MOE_BLOCK
    pad_end = jnp.cumsum(padded)
    pad_start = pad_end - padded
    n_blocks = (t * MOE_TOPK + n_exp * (MOE_BLOCK - 1) + MOE_BLOCK - 1) // MOE_BLOCK
    block_expert = jnp.minimum(
        jnp.searchsorted(pad_end, jnp.arange(n_blocks, dtype=jnp.int32) * MOE_BLOCK, side='right'),
        n_exp - 1).astype(jnp.int32)
    n_used = (pad_end[-1:] // MOE_BLOCK).astype(jnp.int32)
    dest = pad_start[ids[:MOE_TOPK]] + ids[MOE_TOPK:2 * MOE_TOPK]
    idx = dest.reshape(MOE_TOPK, t // tm, tm).transpose(1, 0, 2).reshape(t // tm, 1, MOE_TOPK * tm)

    xg = _dispatch(idx, h, n_blocks * MOE_BLOCK, tm)
    y_buf = _moe_ffn(block_expert, n_used, xg, w_in, w_down)
    return _combine_ln(idx, y_buf, gates, h, _row(ln_g), _row(ln_b), tm)


def kernel(x, ln_g, ln_b, rw_mu, rw_w_rkv, rw_w0, rw_w1, rw_w2, rw_a0, rw_a1, rw_a2, rw_g1, rw_g2, rw_k_k, rw_k_a, rw_r_k, rw_lnx_g, rw_lnx_b, rw_w_o, mb_w_in, mb_conv_w, mb_conv_b, mb_dt_bias, mb_a_log, mb_d, mb_norm_g, mb_w_out, gl_w_in, gl_gk_w2, gl_gk_b, gl_norm_g, gl_w_out, moe_w_group, moe_b_group, moe_w_route, moe_b_route, moe_w_in, moe_w_down):
    bsz, seq, d = x.shape
    assert bsz == 1
    h = x.reshape(seq, d)
    depth = ln_g.shape[0]
    for i in range(depth):
        kind, j = i % 3, i // 3
        spare = SUBLANES - MOE_GROUPS
        wr = jnp.concatenate([moe_w_group[i].T, jnp.zeros((spare, d), F32), moe_w_route[i].T], axis=0)
        br = jnp.concatenate([moe_b_group[i], jnp.full((spare,), -1e30, F32), moe_b_route[i]]).reshape(-1, 1)
        if kind == 0:
            h, logits = _rwkv7_mix(h, ln_g[i, 0], ln_b[i, 0], rw_mu[j], rw_w_rkv[j], rw_w0[j], rw_w1[j],
                                   rw_w2[j], rw_a0[j], rw_a1[j], rw_a2[j], rw_g1[j], rw_g2[j], rw_k_k[j],
                                   rw_k_a[j], rw_r_k[j], rw_lnx_g[j], rw_lnx_b[j], rw_w_o[j], wr, br)
        elif kind == 1:
            h, logits = _mamba2_mix(h, ln_g[i, 0], ln_b[i, 0], mb_w_in[j], mb_conv_w[j], mb_conv_b[j],
                                    mb_dt_bias[j], mb_a_log[j], mb_d[j], mb_norm_g[j], mb_w_out[j], wr, br)
        else:
            h, logits = _gla_mix(h, ln_g[i, 0], ln_b[i, 0], gl_w_in[j], gl_gk_w2[j], gl_gk_b[j],
                                 gl_norm_g[j], gl_w_out[j], wr, br)
        h = _hier_moe(h, logits, ln_g[i, 1], ln_b[i, 1], moe_w_in[i], moe_w_down[i])
    return h.reshape(bsz, seq, d)
```

```python
import functools

import jax
import jax.numpy as jnp
from jax import lax
from jax.experimental import pallas as pl
from jax.experimental.pallas import tpu as pltpu

F32 = jnp.float32
BF16 = jnp.bfloat16

LANES = 128
SUBLANES = 8
VMEM_LIMIT = 56 * 1024 * 1024

DEPTH = 4
ALPHA = (2.0 * DEPTH) ** 0.25
LN_EPS = 1e-5

RW_HEAD = 64
RW_CHUNK = 64
RW_GN_EPS = 64e-5

MB_HEADDIM = 64
MB_GROUPS = 4
MB_STATE = 128
MB_CHUNK = 128
MB_CONV = 4
MB_NORM_EPS = 1e-5

GL_HEADS = 4
GL_CHUNK = 64
GL_GATE_NORM = 16.0
GL_NORM_EPS = 1e-5

MOE_GROUPS = 4
MOE_PER_GROUP = 8
MOE_TOPK = 2
MOE_BLOCK = 256


def _dot(a, b):
    return jnp.dot(a.astype(BF16), b.astype(BF16), preferred_element_type=F32)


def _dot_nt(a, b):
    return lax.dot_general(a.astype(BF16), b.astype(BF16), (((1,), (1,)), ((), ())),
                           preferred_element_type=F32)


def _dot_tn(a, b):
    return lax.dot_general(a.astype(BF16), b.astype(BF16), (((0,), (0,)), ((), ())),
                           preferred_element_type=F32)


def _dot_f32(a, b):
    return jnp.dot(a, b, preferred_element_type=F32, precision=lax.Precision.HIGHEST)


def _sigmoid(x):
    return 1.0 / (1.0 + jnp.exp(-x))


def _softplus(x):
    return jnp.maximum(x, 0.0) + jnp.log(1.0 + jnp.exp(-jnp.abs(x)))


def _silu(x):
    return x * _sigmoid(x)


def _tri(n, dtype=F32):
    r = lax.broadcasted_iota(jnp.int32, (n, n), 0)
    c = lax.broadcasted_iota(jnp.int32, (n, n), 1)
    return (c <= r).astype(dtype)


def _params(sem):
    return pltpu.CompilerParams(dimension_semantics=sem, vmem_limit_bytes=VMEM_LIMIT)


def _pad_cols(w, n):
    return jnp.pad(w, ((0, 0), (0, n - w.shape[1])))


def _pad_rows(w, n):
    return jnp.pad(w, ((0, n - w.shape[0]), (0, 0)))


def _row(v):
    return v.reshape(1, -1).astype(F32)


def _prev_rows_spec(tm, width):
    per = tm // SUBLANES
    return pl.BlockSpec((SUBLANES, width), lambda i: (jnp.maximum(i * per - 1, 0), 0))


def _mm_kernel(x_ref, w_ref, o_ref):
    o_ref[...] = _dot(x_ref[...], w_ref[...])


def _mm(x, w, tm=512, tn=None, name="mm"):
    t, k = x.shape
    n = w.shape[1]
    tn = tn or min(n, 1024)
    return pl.pallas_call(
        _mm_kernel,
        grid=(t // tm, n // tn),
        in_specs=[pl.BlockSpec((tm, k), lambda i, j: (i, 0)),
                  pl.BlockSpec((k, tn), lambda i, j: (0, j))],
        out_specs=pl.BlockSpec((tm, tn), lambda i, j: (i, j)),
        out_shape=jax.ShapeDtypeStruct((t, n), F32),
        compiler_params=_params(("parallel", "parallel")),
        name=name,
    )(x, w)


def _layer_norm(z, g, b):
    mu = jnp.mean(z, axis=-1, keepdims=True)
    zc = z - mu
    var = jnp.mean(zc * zc, axis=-1, keepdims=True)
    return zc * lax.rsqrt(var + LN_EPS) * g + b


def _mm_res_ln_kernel(x_ref, w_ref, res_ref, g_ref, b_ref, wr_ref, br_ref, o_ref, lg_ref):
    y = _dot(x_ref[...], w_ref[...])
    h = _layer_norm(ALPHA * res_ref[...] + y, g_ref[...], b_ref[...])
    o_ref[...] = h
    lg_ref[...] = lax.dot_general(wr_ref[...], h, (((1,), (1,)), ((), ())), preferred_element_type=F32,
                                  precision=lax.Precision.HIGHEST) + br_ref[...]


def _mm_res_ln(x, w, res, g, b, wr, br, tm=512):
    t, k = x.shape
    d = w.shape[1]
    nr = wr.shape[0]
    return pl.pallas_call(
        _mm_res_ln_kernel,
        grid=(t // tm,),
        in_specs=[pl.BlockSpec((tm, k), lambda i: (i, 0)),
                  pl.BlockSpec((k, d), lambda i: (0, 0)),
                  pl.BlockSpec((tm, d), lambda i: (i, 0)),
                  pl.BlockSpec((1, d), lambda i: (0, 0)),
                  pl.BlockSpec((1, d), lambda i: (0, 0)),
                  pl.BlockSpec((nr, d), lambda i: (0, 0)),
                  pl.BlockSpec((nr, 1), lambda i: (0, 0))],
        out_specs=[pl.BlockSpec((tm, d), lambda i: (i, 0)),
                   pl.BlockSpec((nr, tm), lambda i: (0, i))],
        out_shape=[jax.ShapeDtypeStruct((t, d), F32), jax.ShapeDtypeStruct((nr, t), F32)],
        compiler_params=_params(("parallel",)),
        name="mm_res_ln",
    )(x, w, res, g, b, wr, br)


def _rwkv_proj_kernel(x_ref, xp_ref, mu_ref, wrkv_ref, w1_ref, w2_ref, w0_ref, a1_ref, a2_ref, a0_ref,
                      g1_ref, g2_ref, r_ref, k_ref, v_ref, lw_ref, a_ref, g_ref):
    i = pl.program_id(0)
    x = x_ref[...]
    prev = jnp.where(i > 0, xp_ref[SUBLANES - 1:SUBLANES, :], 0.0)
    rows = lax.broadcasted_iota(jnp.int32, x.shape, 0)
    shifted = jnp.where(rows == 0, prev, pltpu.roll(x, 1, 0))
    xx = shifted - x

    def mix(p):
        return x + xx * mu_ref[p:p + 1, :]

    r_ref[...] = _dot(mix(0), wrkv_ref[0])
    k_ref[...] = _dot(mix(1), wrkv_ref[1])
    v_ref[...] = _dot(mix(2), wrkv_ref[2])
    w_pre = w0_ref[...] + _dot(jnp.tanh(_dot(mix(3), w1_ref[...])), w2_ref[...])
    lw_ref[...] = -jnp.exp(-_softplus(-w_pre) - 0.5)
    a_ref[...] = _sigmoid(a0_ref[...] + _dot(_dot(mix(4), a1_ref[...]), a2_ref[...]))
    g_ref[...] = _dot(_sigmoid(_dot(mix(5), g1_ref[...])), g2_ref[...])


def _rwkv_proj(x, mu, w_rkv, w1, w2, w0, a1, a2, a0, g1, g2, tm=256):
    t, d = x.shape
    full = lambda a: pl.BlockSpec(a.shape, lambda i: (0,) * a.ndim)
    row = pl.BlockSpec((tm, d), lambda i: (i, 0))
    args = (mu, w_rkv, w1, w2, w0, a1, a2, a0, g1, g2)
    return pl.pallas_call(
        _rwkv_proj_kernel,
        grid=(t // tm,),
        in_specs=[row, _prev_rows_spec(tm, d)] + [full(a) for a in args],
        out_specs=[row] * 6,
        out_shape=[jax.ShapeDtypeStruct((t, d), F32)] * 6,
        compiler_params=_params(("parallel",)),
        name="rwkv_proj",
    )(x, x, *args)


def _rwkv_rec_kernel(r_ref, k_ref, v_ref, lw_ref, a_ref, g_ref, kk_ref, ka_ref, rk_ref, lg_ref, lb_ref,
                     o_ref, s_ref):
    c = pl.program_id(0)

    @pl.when(c == 0)
    def _():
        s_ref[...] = jnp.zeros_like(s_ref)

    n = RW_CHUNK
    m = 2 * n
    pairs = range(r_ref.shape[1] // LANES)
    cs_all = _dot_f32(_tri(n), lw_ref[...])
    lane = lax.broadcasted_iota(jnp.int32, (n, LANES), 1)
    first = lane < RW_HEAD
    r2 = lax.broadcasted_iota(jnp.int32, (m, LANES), 0)
    c2 = lax.broadcasted_iota(jnp.int32, (m, LANES), 1)
    own = (r2 < n) == (c2 < RW_HEAD)
    eye = (c2 == r2).astype(F32)
    r4 = lax.broadcasted_iota(jnp.int32, (2 * m, 2 * m), 0)
    c4 = lax.broadcasted_iota(jnp.int32, (2 * m, 2 * m), 1)
    rr = jnp.where(r4 < m, r4, r4 - m)
    cc = jnp.where(c4 < m, c4, c4 - m)
    keep = cc < rr + jnp.where(r4 < m, 0, 1)

    def stack(x):
        return jnp.where(own, jnp.concatenate([x, x], axis=0), 0.0)

    def head_sum(x):
        sa = jnp.sum(jnp.where(first, x, 0.0), axis=1, keepdims=True)
        sb = jnp.sum(jnp.where(first, 0.0, x), axis=1, keepdims=True)
        return jnp.where(first, sa, sb)

    ar, bk, bkh, vs, g_chunk, bonus = [], [], [], [], [], []
    for p in pairs:
        sl = slice(LANES * p, LANES * (p + 1))
        r = r_ref[:, sl]
        k = k_ref[:, sl]
        v = v_ref[:, sl]
        lw = lw_ref[:, sl]
        a = a_ref[:, sl]
        cs = cs_all[:, sl]
        cs_end = cs[n - 1:n, :]
        kk = k * kk_ref[:, sl]
        kk = kk / jnp.maximum(jnp.sqrt(head_sum(kk * kk)), 1e-12)
        kh = k * (1.0 + (a - 1.0) * ka_ref[:, sl])
        kka = kk * a
        g_inv = jnp.exp(-cs)
        g_end = jnp.exp(cs_end - cs)
        ar.append(jnp.concatenate([stack(-kk * jnp.exp(cs - lw)), stack(r * jnp.exp(cs))], axis=0).astype(BF16))
        bk.append(jnp.concatenate([stack(kka * g_inv), stack(kh * g_inv)], axis=0).astype(BF16))
        bkh.append(jnp.concatenate([stack(kka * g_end), stack(kh * g_end)], axis=0).astype(BF16))
        vs.append(stack(v))
        g_chunk.append(jnp.exp(cs_end))
        bonus.append(head_sum(r * kh * rk_ref[:, sl]) * v)

    gm = [jnp.where(keep, _dot_nt(ar[p], bk[p]), 0.0) for p in pairs]
    nm = [gm[p][:m, :m] for p in pairs]
    akv = [_dot(gm[p][:m, m:], vs[p]) for p in pairs]
    arbk = [gm[p][m:, :].astype(BF16) for p in pairs]

    tm = [eye + nm[p] for p in pairs]
    pw = [_dot(nm[p], nm[p]) for p in pairs]
    for _ in range(4):
        both = [_dot(jnp.concatenate([tm[p], pw[p]], axis=0), pw[p]) for p in pairs]
        tm = [tm[p] + both[p][:m] for p in pairs]
        pw = [both[p][m:] for p in pairs]
    tm = [tm[p] + _dot(tm[p], pw[p]) for p in pairs]

    ars = [_dot_nt(ar[p], s_ref[p]) for p in pairs]
    u = [_dot(tm[p], ars[p][:m] + akv[p]) for p in pairs]
    uv = [jnp.concatenate([u[p], vs[p]], axis=0).astype(BF16) for p in pairs]
    y = [ars[p][m:] + _dot(arbk[p], uv[p]) for p in pairs]
    for p in pairs:
        s_ref[p] = s_ref[p] * g_chunk[p] + _dot_tn(uv[p], bkh[p])

    for p in pairs:
        sl = slice(LANES * p, LANES * (p + 1))
        mu = jnp.sum(y[p], axis=1, keepdims=True) * (1.0 / RW_HEAD)
        yc = jnp.where(own, y[p] - mu, 0.0)
        var = jnp.sum(yc * yc, axis=1, keepdims=True) * (1.0 / RW_HEAD)
        yn = yc * lax.rsqrt(var + RW_GN_EPS)
        yn = yn[:n] + yn[n:]
        o_ref[:, sl] = (yn * lg_ref[:, sl] + lb_ref[:, sl] + bonus[p]) * g_ref[:, sl]


def _rwkv_rec(r, k, v, lw, a, g, k_k, k_a, r_k, lnx_g, lnx_b):
    t, d = r.shape
    row = pl.BlockSpec((RW_CHUNK, d), lambda c: (c, 0))
    vec = pl.BlockSpec((1, d), lambda c: (0, 0))
    return pl.pallas_call(
        _rwkv_rec_kernel,
        grid=(t // RW_CHUNK,),
        in_specs=[row] * 6 + [vec] * 5,
        out_specs=row,
        out_shape=jax.ShapeDtypeStruct((t, d), F32),
        scratch_shapes=[pltpu.VMEM((d // LANES, LANES, LANES), F32)],
        compiler_params=_params(("arbitrary",)),
        name="rwkv_rec",
    )(r, k, v, lw, a, g, k_k, k_a, r_k, lnx_g, lnx_b)


def _rwkv7_mix(h, ln_g, ln_b, mu, w_rkv, w0, w1, w2, a0, a1, a2, g1, g2, k_k, k_a, r_k, lnx_g, lnx_b, w_o,
               wr, br):
    t, d = h.shape
    lora = w1.shape[1]
    lp = -(-lora // LANES) * LANES
    gl = g1.shape[1]
    gp = -(-gl // LANES) * LANES
    r, k, v, lw, a, g = _rwkv_proj(
        h, _pad_rows(mu, SUBLANES), w_rkv.astype(BF16),
        _pad_cols(w1, lp).astype(BF16), _pad_rows(w2, lp).astype(BF16), _row(w0),
        _pad_cols(a1, lp).astype(BF16), _pad_rows(a2, lp).astype(BF16), _row(a0),
        _pad_cols(g1, gp).astype(BF16), _pad_rows(g2, gp).astype(BF16))
    y = _rwkv_rec(r, k, v, lw, a, g, _row(k_k), _row(k_a), _row(r_k), _row(lnx_g), _row(lnx_b))
    return _mm_res_ln(y, w_o.astype(BF16), h, _row(ln_g), _row(ln_b), wr, br)


def _mb_conv_kernel(x_ref, xp_ref, w_ref, b_ref, xs_ref, bm_ref, cm_ref, buf_ref):
    i = pl.program_id(0)
    tm = x_ref.shape[0]
    x = x_ref[...]
    buf_ref[0:SUBLANES, :] = jnp.where(i > 0, xp_ref[...], 0.0)
    buf_ref[SUBLANES:, :] = x
    acc = b_ref[...] + x * w_ref[MB_CONV - 1:MB_CONV, :]
    for s in range(1, MB_CONV):
        acc = acc + buf_ref[pl.ds(SUBLANES - s, tm), :] * w_ref[MB_CONV - 1 - s:MB_CONV - s, :]
    y = _silu(acc)
    di = xs_ref.shape[1]
    gn = bm_ref.shape[1]
    xs_ref[...] = y[:, :di]
    bm_ref[...] = y[:, di:di + gn]
    cm_ref[...] = y[:, di + gn:]


def _mb_conv(xbc, conv_w, conv_b, di, gn, tm=256):
    t, cd = xbc.shape
    return pl.pallas_call(
        _mb_conv_kernel,
        grid=(t // tm,),
        in_specs=[pl.BlockSpec((tm, cd), lambda i: (i, 0)),
                  _prev_rows_spec(tm, cd),
                  pl.BlockSpec((SUBLANES, cd), lambda i: (0, 0)),
                  pl.BlockSpec((1, cd), lambda i: (0, 0))],
        out_specs=[pl.BlockSpec((tm, di), lambda i: (i, 0)),
                   pl.BlockSpec((tm, gn), lambda i: (i, 0)),
                   pl.BlockSpec((tm, gn), lambda i: (i, 0))],
        out_shape=[jax.ShapeDtypeStruct((t, di), F32),
                   jax.ShapeDtypeStruct((t, gn), F32),
                   jax.ShapeDtypeStruct((t, gn), F32)],
        scratch_shapes=[pltpu.VMEM((tm + SUBLANES, cd), F32)],
        compiler_params=_params(("parallel",)),
        name="mb_conv",
    )(xbc, xbc, _pad_rows(conv_w, SUBLANES), _row(conv_b))


def _mb_ssd_kernel(xs_ref, bm_ref, cm_ref, dt_ref, z_ref, dtb_ref, alog_ref, dsk_ref, ng_ref, o_ref, s_ref):
    c = pl.program_id(0)

    @pl.when(c == 0)
    def _():
        s_ref[...] = jnp.zeros_like(s_ref)

    n = MB_CHUNK
    p_dim = MB_HEADDIM
    groups = s_ref.shape[0]
    hpg = s_ref.shape[1] // p_dim
    dt = _softplus(dt_ref[...] + dtb_ref[...])
    da = dt * (-jnp.exp(alog_ref[...]))
    cs = _dot_f32(_tri(n), da)
    cs_t = cs.T
    cs_end = cs[n - 1:n, :]
    to_end = jnp.exp(cs_end - cs)
    from_start = jnp.exp(cs)
    chunk_decay = jnp.exp(cs_end)
    r2 = lax.broadcasted_iota(jnp.int32, (n, n), 0)
    c2 = lax.broadcasted_iota(jnp.int32, (n, n), 1)
    incl = c2 <= r2

    gw = hpg * p_dim
    grange = range(groups)
    heads = range(groups * hpg)
    bgs = [bm_ref[:, MB_STATE * g:MB_STATE * (g + 1)].astype(BF16) for g in grange]
    cgs = [cm_ref[:, MB_STATE * g:MB_STATE * (g + 1)].astype(BF16) for g in grange]
    scores = [_dot_nt(cgs[g], bgs[g]) for g in grange]
    y_off = [_dot_nt(cgs[g], s_ref[g]) for g in grange]
    x_hs = [xs_ref[:, p_dim * hd:p_dim * (hd + 1)] for hd in heads]
    xdt = [x_hs[hd] * dt[:, hd:hd + 1] for hd in heads]
    y_diag = []
    for hd in heads:
        seg = jnp.minimum(cs[:, hd:hd + 1] - cs_t[hd:hd + 1, :], 0.0)
        y_diag.append(_dot(scores[hd // hpg] * jnp.where(incl, jnp.exp(seg), 0.0), xdt[hd]))
    for g in grange:
        hs = range(g * hpg, (g + 1) * hpg)
        x_end = jnp.concatenate([xdt[hd] * to_end[:, hd:hd + 1] for hd in hs], axis=1)
        decay = jnp.concatenate([jnp.broadcast_to(chunk_decay[:, hd:hd + 1], (p_dim, 1)) for hd in hs], axis=0)
        s_ref[g] = s_ref[g] * decay + _dot_tn(x_end, bgs[g])
    for g in grange:
        y_g = jnp.concatenate(
            [y_diag[hd] + y_off[g][:, p_dim * (hd - g * hpg):p_dim * (hd - g * hpg + 1)] * from_start[:, hd:hd + 1]
             + x_hs[hd] * dsk_ref[:, p_dim * hd:p_dim * (hd + 1)] for hd in range(g * hpg, (g + 1) * hpg)], axis=1)
        y_g = y_g * _silu(z_ref[:, gw * g:gw * (g + 1)])
        ms = jnp.mean(y_g * y_g, axis=-1, keepdims=True)
        o_ref[:, gw * g:gw * (g + 1)] = y_g * lax.rsqrt(ms + MB_NORM_EPS) * ng_ref[:, gw * g:gw * (g + 1)]


def _mb_ssd(xs, bm, cm, dt, z, dt_bias, a_log, d_skip, norm_g):
    t, di = xs.shape
    gn = bm.shape[1]
    groups = gn // MB_STATE
    blk = lambda w: pl.BlockSpec((MB_CHUNK, w), lambda c: (c, 0))
    vec = lambda w: pl.BlockSpec((1, w), lambda c: (0, 0))
    return pl.pallas_call(
        _mb_ssd_kernel,
        grid=(t // MB_CHUNK,),
        in_specs=[blk(di), blk(gn), blk(gn), blk(LANES), blk(di), vec(LANES), vec(LANES), vec(di), vec(di)],
        out_specs=blk(di),
        out_shape=jax.ShapeDtypeStruct((t, di), F32),
        scratch_shapes=[pltpu.VMEM((groups, di // groups, MB_STATE), F32)],
        compiler_params=_params(("arbitrary",)),
        name="mb_ssd",
    )(xs, bm, cm, dt, z, dt_bias, a_log, d_skip, norm_g)


def _mamba2_mix(h, ln_g, ln_b, w_in, conv_w, conv_b, dt_bias, a_log, d_skip, norm_g, w_out, wr, br):
    t, d = h.shape
    heads = dt_bias.shape[0]
    di = heads * MB_HEADDIM
    gn = MB_GROUPS * MB_STATE
    cd = di + 2 * gn
    w_bf = w_in.astype(BF16)
    z = _mm(h, w_bf[:, :di], name="mb_in_z")
    xbc = _mm(h, w_bf[:, di:di + cd], name="mb_in_xbc")
    dt = _mm(h, _pad_cols(w_bf[:, di + cd:], LANES), name="mb_in_dt")
    xs, bm, cm = _mb_conv(xbc, conv_w, conv_b, di, gn)
    pad_h = lambda v_: _pad_cols(_row(v_), LANES)
    y = _mb_ssd(xs, bm, cm, dt, z, pad_h(dt_bias), pad_h(a_log),
                _row(jnp.repeat(d_skip, MB_HEADDIM)), _row(norm_g))
    return _mm_res_ln(y, w_out.astype(BF16), h, _row(ln_g), _row(ln_b), wr, br)


def _gla_kernel(qk_ref, vg_ref, lr_ref, w2_ref, b2_ref, ng_ref, o_ref, s_ref, *, chunks):
    c = pl.program_id(0)

    @pl.when(c == 0)
    def _():
        s_ref[...] = jnp.zeros_like(s_ref)

    n = GL_CHUNK
    heads, dv, dk = s_ref.shape
    kd = heads * dk
    vd = heads * dv
    tri = _tri(n)
    r2 = lax.broadcasted_iota(jnp.int32, (n, n), 0)
    c2 = lax.broadcasted_iota(jnp.int32, (n, n), 1)
    incl = c2 <= r2
    hrange = range(heads)
    pre = _dot(lr_ref[...], w2_ref[...]) + b2_ref[...]
    log_alpha = (jnp.minimum(pre, 0.0) - jnp.log(1.0 + jnp.exp(-jnp.abs(pre)))) * (1.0 / GL_GATE_NORM)
    q_in, k_end, v_bf, decay, intra = {}, {}, {}, {}, {}
    for j in range(chunks):
        rs = slice(n * j, n * (j + 1))
        b_all = _dot_f32(tri, log_alpha[rs, :])
        for hd in hrange:
            q = qk_ref[rs, dk * hd:dk * (hd + 1)] * (dk ** -0.5)
            k = qk_ref[rs, kd + dk * hd:kd + dk * (hd + 1)]
            b = b_all[:, dk * hd:dk * (hd + 1)]
            b_end = b[n - 1:n, :]
            q_in[j, hd] = (q * jnp.exp(b)).astype(BF16)
            k_end[j, hd] = (k * jnp.exp(b_end - b)).astype(BF16)
            v_bf[j, hd] = vg_ref[rs, dv * hd:dv * (hd + 1)].astype(BF16)
            decay[j, hd] = jnp.exp(b_end)
            attn = jnp.where(incl, _dot_nt(q_in[j, hd], k * jnp.exp(-b)), 0.0)
            intra[j, hd] = _dot(attn, v_bf[j, hd])
    for j in range(chunks):
        rs = slice(n * j, n * (j + 1))
        inter = [_dot_nt(q_in[j, hd], s_ref[hd]) for hd in hrange]
        for hd in hrange:
            s_ref[hd] = s_ref[hd] * decay[j, hd] + _dot_tn(v_bf[j, hd], k_end[j, hd])
        for hd in hrange:
            o = intra[j, hd] + inter[hd]
            o = o * lax.rsqrt(jnp.mean(o * o, axis=-1, keepdims=True) + GL_NORM_EPS) * ng_ref[...]
            o_ref[rs, dv * hd:dv * (hd + 1)] = o * _silu(vg_ref[rs, vd + dv * hd:vd + dv * (hd + 1)])


def _gla(qk, vg, lr, w2, b2, norm_g, chunks=4):
    t = qk.shape[0]
    kd = qk.shape[1] // 2
    vd = vg.shape[1] // 2
    tm = GL_CHUNK * chunks
    blk = lambda w: pl.BlockSpec((tm, w), lambda c: (c, 0))
    full = lambda a: pl.BlockSpec(a.shape, lambda c: (0, 0))
    return pl.pallas_call(
        functools.partial(_gla_kernel, chunks=chunks),
        grid=(t // tm,),
        in_specs=[blk(2 * kd), blk(2 * vd), blk(LANES), full(w2), full(b2), full(norm_g)],
        out_specs=blk(vd),
        out_shape=jax.ShapeDtypeStruct((t, vd), F32),
        scratch_shapes=[pltpu.VMEM((GL_HEADS, vd // GL_HEADS, kd // GL_HEADS), F32)],
        compiler_params=_params(("arbitrary",)),
        name="gla",
    )(qk, vg, lr, w2, b2, norm_g)


def _gla_mix(h, ln_g, ln_b, w_in, gk_w2, gk_b, norm_g, w_out, wr, br):
    kd = gk_w2.shape[1]
    vd = w_out.shape[0]
    w_bf = w_in.astype(BF16)
    qk = _mm(h, w_bf[:, :2 * kd], name="gl_in_qk")
    vg = _mm(h, w_bf[:, 2 * kd:2 * kd + 2 * vd], name="gl_in_vg")
    lr = _mm(h, _pad_cols(w_bf[:, 2 * kd + 2 * vd:], LANES), name="gl_in_lr")
    y = _gla(qk, vg, lr, _pad_rows(gk_w2, LANES).astype(BF16), _row(gk_b), _row(norm_g))
    return _mm_res_ln(y, w_out.astype(BF16), h, _row(ln_g), _row(ln_b), wr, br)


def _moe_ffn_kernel(be_ref, nb_ref, x_ref, win_ref, wdn_ref, o_ref):
    b = pl.program_id(0)

    @pl.when(b < nb_ref[0])
    def _():
        hcat = _dot(x_ref[...], win_ref[0, 0])
        ff = hcat.shape[1] // 2
        act = _silu(hcat[:, :ff]) * hcat[:, ff:]
        o_ref[...] = _dot(act, wdn_ref[0, 0])

    @pl.when(b >= nb_ref[0])
    def _():
        o_ref[...] = jnp.zeros_like(o_ref)


def _moe_ffn(block_expert, n_used, xg, w_in, w_down, layer):
    p, d = xg.shape
    ff2 = w_in.shape[3]
    grid_spec = pltpu.PrefetchScalarGridSpec(
        num_scalar_prefetch=2,
        grid=(p // MOE_BLOCK,),
        in_specs=[pl.BlockSpec((MOE_BLOCK, d), lambda b, be, nb: (b, 0)),
                  pl.BlockSpec((1, 1, d, ff2), lambda b, be, nb: (layer, be[b], 0, 0)),
                  pl.BlockSpec((1, 1, ff2 // 2, d), lambda b, be, nb: (layer, be[b], 0, 0))],
        out_specs=pl.BlockSpec((MOE_BLOCK, d), lambda b, be, nb: (b, 0)),
    )
    return pl.pallas_call(
        _moe_ffn_kernel,
        grid_spec=grid_spec,
        out_shape=jax.ShapeDtypeStruct((p, d), F32),
        compiler_params=_params(("arbitrary",)),
        name="moe_ffn",
    )(block_expert, n_used, xg, w_in, w_down)


def _first_argmax(x, rows):
    m = jnp.max(x, axis=0, keepdims=True)
    idx = jnp.min(jnp.where(x == m, rows, x.shape[0]), axis=0, keepdims=True)
    return m, idx


def _softmax_rows(x):
    e = jnp.exp(x - jnp.max(x, axis=0, keepdims=True))
    return e / jnp.sum(e, axis=0, keepdims=True)


def _route_kernel(lg_ref, ids_ref, gt_ref, cnt_ref, carry_ref):
    i = pl.program_id(0)

    @pl.when(i == 0)
    def _():
        carry_ref[...] = jnp.zeros_like(carry_ref)

    tm = lg_ref.shape[1]
    epg = MOE_PER_GROUP
    n_exp = MOE_GROUPS * epg
    rows8 = lax.broadcasted_iota(jnp.int32, (SUBLANES, tm), 0)
    group_probs = _softmax_rows(lg_ref[0:SUBLANES, :])
    p_group, g_idx = _first_argmax(group_probs, rows8)
    sel = jnp.zeros((epg, tm), F32)
    for gi in range(MOE_GROUPS):
        sel = sel + jnp.where(g_idx == gi, lg_ref[SUBLANES + epg * gi:SUBLANES + epg * (gi + 1), :], 0.0)
    p_exp = _softmax_rows(sel)
    p1, i1 = _first_argmax(p_exp, rows8)
    p2, i2 = _first_argmax(jnp.where(rows8 == i1, -1.0, p_exp), rows8)
    denom = p1 + p2
    e0 = g_idx * epg + i1
    e1 = g_idx * epg + i2

    rows_e = lax.broadcasted_iota(jnp.int32, (n_exp, tm), 0)
    oh0 = (rows_e == e0).astype(F32)
    oh1 = (rows_e == e1).astype(F32)
    oh = oh0 + oh1
    tr = lax.broadcasted_iota(jnp.int32, (tm, tm), 0)
    tc = lax.broadcasted_iota(jnp.int32, (tm, tm), 1)
    before = _dot(oh, (tr < tc).astype(F32))
    base = carry_ref[:, 0:1] + before
    rank0 = jnp.sum(oh0 * base, axis=0, keepdims=True)
    rank1 = jnp.sum(oh1 * base, axis=0, keepdims=True)
    carry_ref[...] = carry_ref[...] + jnp.sum(oh, axis=1, keepdims=True)
    cnt_ref[...] = carry_ref[...]

    zi = jnp.zeros((SUBLANES - 4, tm), jnp.int32)
    ids_ref[...] = jnp.concatenate([e0, e1, rank0.astype(jnp.int32), rank1.astype(jnp.int32), zi], axis=0)
    zf = jnp.zeros((SUBLANES - 2, tm), F32)
    gt_ref[...] = jnp.concatenate([p_group * p1 / denom, p_group * p2 / denom, zf], axis=0)


def _route(logits_t, tm=512):
    nr, t = logits_t.shape
    n_exp = MOE_GROUPS * MOE_PER_GROUP
    return pl.pallas_call(
        _route_kernel,
        grid=(t // tm,),
        in_specs=[pl.BlockSpec((nr, tm), lambda i: (0, i))],
        out_specs=[pl.BlockSpec((SUBLANES, tm), lambda i: (0, i)),
                   pl.BlockSpec((SUBLANES, tm), lambda i: (0, i)),
                   pl.BlockSpec((n_exp, LANES), lambda i: (0, 0))],
        out_shape=[jax.ShapeDtypeStruct((SUBLANES, t), jnp.int32),
                   jax.ShapeDtypeStruct((SUBLANES, t), F32),
                   jax.ShapeDtypeStruct((n_exp, LANES), F32)],
        scratch_shapes=[pltpu.VMEM((n_exp, LANES), F32)],
        compiler_params=_params(("arbitrary",)),
        name="moe_route",
    )(logits_t)


def _row_copy(src_ref, src_row, dst_ref, dst_row, sem):
    return pltpu.make_async_copy(src_ref.at[pl.ds(src_row, 1)], dst_ref.at[pl.ds(dst_row, 1)], sem)


def _dispatch_kernel(idx_ref, x_ref, buf_in_ref, buf_ref, sem):
    del buf_in_ref
    tm = x_ref.shape[0]

    def issue(t, carry):
        _row_copy(x_ref, t, buf_ref, idx_ref[0, 0, t], sem).start(priority=0)
        _row_copy(x_ref, t, buf_ref, idx_ref[0, 0, tm + t], sem).start(priority=1)
        return carry

    def drain(t, carry):
        _row_copy(x_ref, 0, buf_ref, 0, sem).wait()
        return carry

    lax.fori_loop(0, tm, issue, 0, unroll=8)
    lax.fori_loop(0, MOE_TOPK * tm, drain, 0, unroll=8)


def _dispatch(idx, h, n_rows, tm):
    t, d = h.shape
    return pl.pallas_call(
        _dispatch_kernel,
        grid=(t // tm,),
        in_specs=[pl.BlockSpec((1, 1, MOE_TOPK * tm), lambda i: (i, 0, 0), memory_space=pltpu.SMEM),
                  pl.BlockSpec((tm, d), lambda i: (i, 0)),
                  pl.BlockSpec(memory_space=pl.ANY)],
        out_specs=pl.BlockSpec(memory_space=pl.ANY),
        out_shape=jax.ShapeDtypeStruct((n_rows, d), F32),
        scratch_shapes=[pltpu.SemaphoreType.DMA(())],
        input_output_aliases={2: 0},
        compiler_params=_params(("arbitrary",)),
        name="moe_dispatch",
    )(idx, h, jnp.zeros((n_rows, d), F32))


def _combine_ln_kernel(idx_ref, y_ref, gt_ref, res_ref, g_ref, b_ref, o_ref, y0_ref, y1_ref, sem):
    tm = res_ref.shape[0]

    def issue(t, carry):
        _row_copy(y_ref, idx_ref[0, 0, t], y0_ref, t, sem).start(priority=0)
        _row_copy(y_ref, idx_ref[0, 0, tm + t], y1_ref, t, sem).start(priority=1)
        return carry

    def drain(t, carry):
        _row_copy(y_ref, 0, y0_ref, 0, sem).wait()
        return carry

    lax.fori_loop(0, tm, issue, 0, unroll=8)
    gates = jnp.concatenate([gt_ref[...], jnp.zeros((LANES - SUBLANES, tm), F32)], axis=0).T
    lax.fori_loop(0, MOE_TOPK * tm, drain, 0, unroll=8)
    y = y0_ref[...] * gates[:, 0:1] + y1_ref[...] * gates[:, 1:2]
    o_ref[...] = _layer_norm(ALPHA * res_ref[...] + y, g_ref[...], b_ref[...])


def _combine_ln(idx, y_buf, gates, res, g, b, tm):
    t, d = res.shape
    row = pl.BlockSpec((tm, d), lambda i: (i, 0))
    vec = pl.BlockSpec((1, d), lambda i: (0, 0))
    return pl.pallas_call(
        _combine_ln_kernel,
        grid=(t // tm,),
        in_specs=[pl.BlockSpec((1, 1, MOE_TOPK * tm), lambda i: (i, 0, 0), memory_space=pltpu.SMEM),
                  pl.BlockSpec(memory_space=pl.ANY),
                  pl.BlockSpec((SUBLANES, tm), lambda i: (0, i)),
                  row, vec, vec],
        out_specs=row,
        out_shape=jax.ShapeDtypeStruct((t, d), F32),
        scratch_shapes=[pltpu.VMEM((tm, d), F32), pltpu.VMEM((tm, d), F32), pltpu.SemaphoreType.DMA(())],
        compiler_params=_params(("arbitrary",)),
        name="moe_combine_ln",
    )(idx, y_buf, gates, res, g, b)


def _hier_moe(h, logits_t, ln_g, ln_b, w_in, w_down, layer, tm=512):
    t, d = h.shape
    n_exp = w_in.shape[1]
    ids, gates, counts = _route(logits_t)

    counts = counts[:, 0].astype(jnp.int32)
    padded = (counts + MOE_BLOCK - 1) // MOE_BLOCK * MOE_BLOCK
    pad_end = jnp.cumsum(padded)
    pad_start = pad_end - padded
    n_blocks = (t * MOE_TOPK + n_exp * (MOE_BLOCK - 1) + MOE_BLOCK - 1) // MOE_BLOCK
    block_first_row = jnp.arange(n_blocks, dtype=jnp.int32) * MOE_BLOCK
    block_expert = jnp.minimum(jnp.sum(pad_end[None, :] <= block_first_row[:, None], axis=1),
                               n_exp - 1).astype(jnp.int32)
    n_used = (pad_end[-1:] // MOE_BLOCK).astype(jnp.int32)
    experts = jnp.arange(n_exp, dtype=jnp.int32)[:, None, None]
    seg_start = jnp.sum(jnp.where(ids[None, :MOE_TOPK] == experts, pad_start[:, None, None], 0), axis=0)
    dest = seg_start + ids[MOE_TOPK:2 * MOE_TOPK]
    idx = dest.reshape(MOE_TOPK, t // tm, tm).transpose(1, 0, 2).reshape(t // tm, 1, MOE_TOPK * tm)

    xg = _dispatch(idx, h, n_blocks * MOE_BLOCK, tm)
    y_buf = _moe_ffn(block_expert, n_used, xg, w_in, w_down, layer)
    return _combine_ln(idx, y_buf, gates, h, _row(ln_g), _row(ln_b), tm)


def kernel(x, ln_g, ln_b, rw_mu, rw_w_rkv, rw_w0, rw_w1, rw_w2, rw_a0, rw_a1, rw_a2, rw_g1, rw_g2, rw_k_k, rw_k_a, rw_r_k, rw_lnx_g, rw_lnx_b, rw_w_o, mb_w_in, mb_conv_w, mb_conv_b, mb_dt_bias, mb_a_log, mb_d, mb_norm_g, mb_w_out, gl_w_in, gl_gk_w2, gl_gk_b, gl_norm_g, gl_w_out, moe_w_group, moe_b_group, moe_w_route, moe_b_route, moe_w_in, moe_w_down):
    bsz, seq, d = x.shape
    assert bsz == 1
    h = x.reshape(seq, d)
    depth = ln_g.shape[0]
    for i in range(depth):
        kind, j = i % 3, i // 3
        spare = SUBLANES - MOE_GROUPS
        wr = jnp.concatenate([moe_w_group[i].T, jnp.zeros((spare, d), F32), moe_w_route[i].T], axis=0)
        br = jnp.concatenate([moe_b_group[i], jnp.full((spare,), -1e30, F32), moe_b_route[i]]).reshape(-1, 1)
        if kind == 0:
            h, logits = _rwkv7_mix(h, ln_g[i, 0], ln_b[i, 0], rw_mu[j], rw_w_rkv[j], rw_w0[j], rw_w1[j],
                                   rw_w2[j], rw_a0[j], rw_a1[j], rw_a2[j], rw_g1[j], rw_g2[j], rw_k_k[j],
                                   rw_k_a[j], rw_r_k[j], rw_lnx_g[j], rw_lnx_b[j], rw_w_o[j], wr, br)
        elif kind == 1:
            h, logits = _mamba2_mix(h, ln_g[i, 0], ln_b[i, 0], mb_w_in[j], mb_conv_w[j], mb_conv_b[j],
                                    mb_dt_bias[j], mb_a_log[j], mb_d[j], mb_norm_g[j], mb_w_out[j], wr, br)
        else:
            h, logits = _gla_mix(h, ln_g[i, 0], ln_b[i, 0], gl_w_in[j], gl_gk_w2[j], gl_gk_b[j],
                                 gl_norm_g[j], gl_w_out[j], wr, br)
        h = _hier_moe(h, logits, ln_g[i, 1], ln_b[i, 1], moe_w_in, moe_w_down, i)
    return h.reshape(bsz, seq, d)
```

```python
import functools

import jax
import jax.numpy as jnp
from jax import lax
from jax.experimental import pallas as pl
from jax.experimental.pallas import tpu as pltpu

F32 = jnp.float32
BF16 = jnp.bfloat16

LANES = 128
SUBLANES = 8
VMEM_LIMIT = 56 * 1024 * 1024

DEPTH = 4
ALPHA = (2.0 * DEPTH) ** 0.25
LN_EPS = 1e-5

RW_HEAD = 64
RW_CHUNK = 64
RW_GN_EPS = 64e-5

MB_HEADDIM = 64
MB_GROUPS = 4
MB_STATE = 128
MB_CHUNK = 128
MB_CONV = 4
MB_NORM_EPS = 1e-5

GL_HEADS = 4
GL_CHUNK = 64
GL_GATE_NORM = 16.0
GL_NORM_EPS = 1e-5

MOE_GROUPS = 4
MOE_PER_GROUP = 8
MOE_TOPK = 2
MOE_BLOCK = 256
MOE_PIECE = 256


def _dot(a, b):
    return jnp.dot(a.astype(BF16), b.astype(BF16), preferred_element_type=F32)


def _dot_nt(a, b):
    return lax.dot_general(a.astype(BF16), b.astype(BF16), (((1,), (1,)), ((), ())),
                           preferred_element_type=F32)


def _dot_tn(a, b):
    return lax.dot_general(a.astype(BF16), b.astype(BF16), (((0,), (0,)), ((), ())),
                           preferred_element_type=F32)


def _dot_f32(a, b):
    return jnp.dot(a, b, preferred_element_type=F32, precision=lax.Precision.HIGHEST)


def _sigmoid(x):
    return 1.0 / (1.0 + jnp.exp(-x))


def _softplus(x):
    return jnp.maximum(x, 0.0) + jnp.log(1.0 + jnp.exp(-jnp.abs(x)))


def _silu(x):
    return x * _sigmoid(x)


def _tri(n, dtype=F32):
    r = lax.broadcasted_iota(jnp.int32, (n, n), 0)
    c = lax.broadcasted_iota(jnp.int32, (n, n), 1)
    return (c <= r).astype(dtype)


def _params(sem):
    return pltpu.CompilerParams(dimension_semantics=sem, vmem_limit_bytes=VMEM_LIMIT)


def _pad_cols(w, n):
    return jnp.pad(w, ((0, 0), (0, n - w.shape[1])))


def _pad_rows(w, n):
    return jnp.pad(w, ((0, n - w.shape[0]), (0, 0)))


def _row(v):
    return v.reshape(1, -1).astype(F32)


def _prev_rows_spec(tm, width):
    per = tm // SUBLANES
    return pl.BlockSpec((SUBLANES, width), lambda i: (jnp.maximum(i * per - 1, 0), 0))


def _mm_kernel(x_ref, w_ref, o_ref):
    o_ref[...] = _dot(x_ref[...], w_ref[...])


def _mm(x, w, tm=512, tn=None, name="mm"):
    t, k = x.shape
    n = w.shape[1]
    tn = tn or min(n, 1024)
    return pl.pallas_call(
        _mm_kernel,
        grid=(t // tm, n // tn),
        in_specs=[pl.BlockSpec((tm, k), lambda i, j: (i, 0)),
                  pl.BlockSpec((k, tn), lambda i, j: (0, j))],
        out_specs=pl.BlockSpec((tm, tn), lambda i, j: (i, j)),
        out_shape=jax.ShapeDtypeStruct((t, n), F32),
        compiler_params=_params(("parallel", "parallel")),
        name=name,
    )(x, w)


def _layer_norm(z, g, b):
    mu = jnp.mean(z, axis=-1, keepdims=True)
    zc = z - mu
    var = jnp.mean(zc * zc, axis=-1, keepdims=True)
    return zc * lax.rsqrt(var + LN_EPS) * g + b


def _mm_res_ln_kernel(x_ref, w_ref, res_ref, g_ref, b_ref, wr_ref, br_ref, o_ref, lg_ref):
    y = _dot(x_ref[...], w_ref[...])
    h = _layer_norm(ALPHA * res_ref[...] + y, g_ref[...], b_ref[...])
    o_ref[...] = h
    lg_ref[...] = lax.dot_general(wr_ref[...], h, (((1,), (1,)), ((), ())), preferred_element_type=F32,
                                  precision=lax.Precision.HIGHEST) + br_ref[...]


def _mm_res_ln(x, w, res, g, b, wr, br, tm=512):
    t, k = x.shape
    d = w.shape[1]
    nr = wr.shape[0]
    return pl.pallas_call(
        _mm_res_ln_kernel,
        grid=(t // tm,),
        in_specs=[pl.BlockSpec((tm, k), lambda i: (i, 0)),
                  pl.BlockSpec((k, d), lambda i: (0, 0)),
                  pl.BlockSpec((tm, d), lambda i: (i, 0)),
                  pl.BlockSpec((1, d), lambda i: (0, 0)),
                  pl.BlockSpec((1, d), lambda i: (0, 0)),
                  pl.BlockSpec((nr, d), lambda i: (0, 0)),
                  pl.BlockSpec((nr, 1), lambda i: (0, 0))],
        out_specs=[pl.BlockSpec((tm, d), lambda i: (i, 0)),
                   pl.BlockSpec((nr, tm), lambda i: (0, i))],
        out_shape=[jax.ShapeDtypeStruct((t, d), F32), jax.ShapeDtypeStruct((nr, t), F32)],
        compiler_params=_params(("parallel",)),
        name="mm_res_ln",
    )(x, w, res, g, b, wr, br)


def _rwkv_proj_kernel(x_ref, xp_ref, mu_ref, wrkv_ref, w1_ref, w2_ref, w0_ref, a1_ref, a2_ref, a0_ref,
                      g1_ref, g2_ref, r_ref, k_ref, v_ref, lw_ref, a_ref, g_ref):
    i = pl.program_id(0)
    x = x_ref[...]
    prev = jnp.where(i > 0, xp_ref[SUBLANES - 1:SUBLANES, :], 0.0)
    rows = lax.broadcasted_iota(jnp.int32, x.shape, 0)
    shifted = jnp.where(rows == 0, prev, pltpu.roll(x, 1, 0))
    xx = shifted - x

    def mix(p):
        return x + xx * mu_ref[p:p + 1, :]

    r_ref[...] = _dot(mix(0), wrkv_ref[0])
    k_ref[...] = _dot(mix(1), wrkv_ref[1])
    v_ref[...] = _dot(mix(2), wrkv_ref[2])
    w_pre = w0_ref[...] + _dot(jnp.tanh(_dot(mix(3), w1_ref[...])), w2_ref[...])
    lw_ref[...] = -jnp.exp(-_softplus(-w_pre) - 0.5)
    a_ref[...] = _sigmoid(a0_ref[...] + _dot(_dot(mix(4), a1_ref[...]), a2_ref[...]))
    g_ref[...] = _dot(_sigmoid(_dot(mix(5), g1_ref[...])), g2_ref[...])


def _rwkv_proj(x, mu, w_rkv, w1, w2, w0, a1, a2, a0, g1, g2, tm=256):
    t, d = x.shape
    full = lambda a: pl.BlockSpec(a.shape, lambda i: (0,) * a.ndim)
    row = pl.BlockSpec((tm, d), lambda i: (i, 0))
    args = (mu, w_rkv, w1, w2, w0, a1, a2, a0, g1, g2)
    return pl.pallas_call(
        _rwkv_proj_kernel,
        grid=(t // tm,),
        in_specs=[row, _prev_rows_spec(tm, d)] + [full(a) for a in args],
        out_specs=[row] * 6,
        out_shape=[jax.ShapeDtypeStruct((t, d), F32)] * 6,
        compiler_params=_params(("parallel",)),
        name="rwkv_proj",
    )(x, x, *args)


def _rwkv_rec_kernel(r_ref, k_ref, v_ref, lw_ref, a_ref, g_ref, kk_ref, ka_ref, rk_ref, lg_ref, lb_ref,
                     o_ref, s_ref):
    c = pl.program_id(0)

    @pl.when(c == 0)
    def _():
        s_ref[...] = jnp.zeros_like(s_ref)

    n = RW_CHUNK
    m = 2 * n
    pairs = range(r_ref.shape[1] // LANES)
    cs_all = _dot_f32(_tri(n), lw_ref[...])
    lane = lax.broadcasted_iota(jnp.int32, (n, LANES), 1)
    first = lane < RW_HEAD
    r2 = lax.broadcasted_iota(jnp.int32, (m, LANES), 0)
    c2 = lax.broadcasted_iota(jnp.int32, (m, LANES), 1)
    own = (r2 < n) == (c2 < RW_HEAD)
    eye = (c2 == r2).astype(F32)
    r4 = lax.broadcasted_iota(jnp.int32, (2 * m, 2 * m), 0)
    c4 = lax.broadcasted_iota(jnp.int32, (2 * m, 2 * m), 1)
    rr = jnp.where(r4 < m, r4, r4 - m)
    cc = jnp.where(c4 < m, c4, c4 - m)
    keep = cc < rr + jnp.where(r4 < m, 0, 1)

    def stack(x):
        return jnp.where(own, jnp.concatenate([x, x], axis=0), 0.0)

    def head_sum(x):
        sa = jnp.sum(jnp.where(first, x, 0.0), axis=1, keepdims=True)
        sb = jnp.sum(jnp.where(first, 0.0, x), axis=1, keepdims=True)
        return jnp.where(first, sa, sb)

    ar, bk, bkh, vs, g_chunk, bonus = [], [], [], [], [], []
    for p in pairs:
        sl = slice(LANES * p, LANES * (p + 1))
        r = r_ref[:, sl]
        k = k_ref[:, sl]
        v = v_ref[:, sl]
        lw = lw_ref[:, sl]
        a = a_ref[:, sl]
        cs = cs_all[:, sl]
        cs_end = cs[n - 1:n, :]
        kk = k * kk_ref[:, sl]
        kk = kk / jnp.maximum(jnp.sqrt(head_sum(kk * kk)), 1e-12)
        kh = k * (1.0 + (a - 1.0) * ka_ref[:, sl])
        kka = kk * a
        g_inv = jnp.exp(-cs)
        g_end = jnp.exp(cs_end - cs)
        ar.append(jnp.concatenate([stack(-kk * jnp.exp(cs - lw)), stack(r * jnp.exp(cs))], axis=0).astype(BF16))
        bk.append(jnp.concatenate([stack(kka * g_inv), stack(kh * g_inv)], axis=0).astype(BF16))
        bkh.append(jnp.concatenate([stack(kka * g_end), stack(kh * g_end)], axis=0).astype(BF16))
        vs.append(stack(v))
        g_chunk.append(jnp.exp(cs_end))
        bonus.append(head_sum(r * kh * rk_ref[:, sl]) * v)

    gm = [jnp.where(keep, _dot_nt(ar[p], bk[p]), 0.0) for p in pairs]
    nm = [gm[p][:m, :m] for p in pairs]
    akv = [_dot(gm[p][:m, m:], vs[p]) for p in pairs]
    arbk = [gm[p][m:, :].astype(BF16) for p in pairs]

    tm = [eye + nm[p] for p in pairs]
    pw = [_dot(nm[p], nm[p]) for p in pairs]
    for _ in range(4):
        both = [_dot(jnp.concatenate([tm[p], pw[p]], axis=0), pw[p]) for p in pairs]
        tm = [tm[p] + both[p][:m] for p in pairs]
        pw = [both[p][m:] for p in pairs]
    tm = [tm[p] + _dot(tm[p], pw[p]) for p in pairs]

    ars = [_dot_nt(ar[p], s_ref[p]) for p in pairs]
    u = [_dot(tm[p], ars[p][:m] + akv[p]) for p in pairs]
    uv = [jnp.concatenate([u[p], vs[p]], axis=0).astype(BF16) for p in pairs]
    y = [ars[p][m:] + _dot(arbk[p], uv[p]) for p in pairs]
    for p in pairs:
        s_ref[p] = s_ref[p] * g_chunk[p] + _dot_tn(uv[p], bkh[p])

    for p in pairs:
        sl = slice(LANES * p, LANES * (p + 1))
        mu = jnp.sum(y[p], axis=1, keepdims=True) * (1.0 / RW_HEAD)
        yc = jnp.where(own, y[p] - mu, 0.0)
        var = jnp.sum(yc * yc, axis=1, keepdims=True) * (1.0 / RW_HEAD)
        yn = yc * lax.rsqrt(var + RW_GN_EPS)
        yn = yn[:n] + yn[n:]
        o_ref[:, sl] = (yn * lg_ref[:, sl] + lb_ref[:, sl] + bonus[p]) * g_ref[:, sl]


def _rwkv_rec(r, k, v, lw, a, g, k_k, k_a, r_k, lnx_g, lnx_b):
    t, d = r.shape
    row = pl.BlockSpec((RW_CHUNK, d), lambda c: (c, 0))
    vec = pl.BlockSpec((1, d), lambda c: (0, 0))
    return pl.pallas_call(
        _rwkv_rec_kernel,
        grid=(t // RW_CHUNK,),
        in_specs=[row] * 6 + [vec] * 5,
        out_specs=row,
        out_shape=jax.ShapeDtypeStruct((t, d), F32),
        scratch_shapes=[pltpu.VMEM((d // LANES, LANES, LANES), F32)],
        compiler_params=_params(("arbitrary",)),
        name="rwkv_rec",
    )(r, k, v, lw, a, g, k_k, k_a, r_k, lnx_g, lnx_b)


def _rwkv7_mix(h, ln_g, ln_b, mu, w_rkv, w0, w1, w2, a0, a1, a2, g1, g2, k_k, k_a, r_k, lnx_g, lnx_b, w_o,
               wr, br):
    t, d = h.shape
    lora = w1.shape[1]
    lp = -(-lora // LANES) * LANES
    gl = g1.shape[1]
    gp = -(-gl // LANES) * LANES
    r, k, v, lw, a, g = _rwkv_proj(
        h, _pad_rows(mu, SUBLANES), w_rkv.astype(BF16),
        _pad_cols(w1, lp).astype(BF16), _pad_rows(w2, lp).astype(BF16), _row(w0),
        _pad_cols(a1, lp).astype(BF16), _pad_rows(a2, lp).astype(BF16), _row(a0),
        _pad_cols(g1, gp).astype(BF16), _pad_rows(g2, gp).astype(BF16))
    y = _rwkv_rec(r, k, v, lw, a, g, _row(k_k), _row(k_a), _row(r_k), _row(lnx_g), _row(lnx_b))
    return _mm_res_ln(y, w_o.astype(BF16), h, _row(ln_g), _row(ln_b), wr, br)


def _mb_conv_kernel(x_ref, xp_ref, w_ref, b_ref, xs_ref, bm_ref, cm_ref, buf_ref):
    i = pl.program_id(0)
    tm = x_ref.shape[0]
    x = x_ref[...]
    buf_ref[0:SUBLANES, :] = jnp.where(i > 0, xp_ref[...], 0.0)
    buf_ref[SUBLANES:, :] = x
    acc = b_ref[...] + x * w_ref[MB_CONV - 1:MB_CONV, :]
    for s in range(1, MB_CONV):
        acc = acc + buf_ref[pl.ds(SUBLANES - s, tm), :] * w_ref[MB_CONV - 1 - s:MB_CONV - s, :]
    y = _silu(acc)
    di = xs_ref.shape[1]
    gn = bm_ref.shape[1]
    xs_ref[...] = y[:, :di]
    bm_ref[...] = y[:, di:di + gn]
    cm_ref[...] = y[:, di + gn:]


def _mb_conv(xbc, conv_w, conv_b, di, gn, tm=256):
    t, cd = xbc.shape
    return pl.pallas_call(
        _mb_conv_kernel,
        grid=(t // tm,),
        in_specs=[pl.BlockSpec((tm, cd), lambda i: (i, 0)),
                  _prev_rows_spec(tm, cd),
                  pl.BlockSpec((SUBLANES, cd), lambda i: (0, 0)),
                  pl.BlockSpec((1, cd), lambda i: (0, 0))],
        out_specs=[pl.BlockSpec((tm, di), lambda i: (i, 0)),
                   pl.BlockSpec((tm, gn), lambda i: (i, 0)),
                   pl.BlockSpec((tm, gn), lambda i: (i, 0))],
        out_shape=[jax.ShapeDtypeStruct((t, di), F32),
                   jax.ShapeDtypeStruct((t, gn), F32),
                   jax.ShapeDtypeStruct((t, gn), F32)],
        scratch_shapes=[pltpu.VMEM((tm + SUBLANES, cd), F32)],
        compiler_params=_params(("parallel",)),
        name="mb_conv",
    )(xbc, xbc, _pad_rows(conv_w, SUBLANES), _row(conv_b))


def _mb_ssd_kernel(xs_ref, bm_ref, cm_ref, dt_ref, z_ref, dtb_ref, alog_ref, dsk_ref, ng_ref, o_ref, s_ref,
                   *, chunks):
    c = pl.program_id(0)

    @pl.when(c == 0)
    def _():
        s_ref[...] = jnp.zeros_like(s_ref)

    n = MB_CHUNK
    p_dim = MB_HEADDIM
    groups = s_ref.shape[0]
    hpg = s_ref.shape[1] // p_dim
    gw = hpg * p_dim
    grange = range(groups)
    heads = range(groups * hpg)
    r2 = lax.broadcasted_iota(jnp.int32, (n, n), 0)
    c2 = lax.broadcasted_iota(jnp.int32, (n, n), 1)
    incl = c2 <= r2
    tri = _tri(n)
    a_neg = -jnp.exp(alog_ref[...])

    parts = []
    for j in range(chunks):
        rs = slice(n * j, n * (j + 1))
        dt = _softplus(dt_ref[rs, :] + dtb_ref[...])
        cs = _dot_f32(tri, dt * a_neg)
        cs_t = cs.T
        cs_end = cs[n - 1:n, :]
        to_end = jnp.exp(cs_end - cs)
        from_start = jnp.exp(cs)
        chunk_decay = jnp.exp(cs_end)
        bgs = [bm_ref[rs, MB_STATE * g:MB_STATE * (g + 1)].astype(BF16) for g in grange]
        cgs = [cm_ref[rs, MB_STATE * g:MB_STATE * (g + 1)].astype(BF16) for g in grange]
        scores = [_dot_nt(cgs[g], bgs[g]) for g in grange]
        x_hs = [xs_ref[rs, p_dim * hd:p_dim * (hd + 1)] for hd in heads]
        xdt = [x_hs[hd] * dt[:, hd:hd + 1] for hd in heads]
        y_diag = []
        for hd in heads:
            seg = jnp.minimum(cs[:, hd:hd + 1] - cs_t[hd:hd + 1, :], 0.0)
            y_diag.append(_dot(scores[hd // hpg] * jnp.where(incl, jnp.exp(seg), 0.0), xdt[hd])
                          + x_hs[hd] * dsk_ref[:, p_dim * hd:p_dim * (hd + 1)])
        x_end, decay = [], []
        for g in grange:
            hs = range(g * hpg, (g + 1) * hpg)
            x_end.append(jnp.concatenate([xdt[hd] * to_end[:, hd:hd + 1] for hd in hs], axis=1).astype(BF16))
            decay.append(jnp.concatenate(
                [jnp.broadcast_to(chunk_decay[:, hd:hd + 1], (p_dim, 1)) for hd in hs], axis=0))
        parts.append((bgs, cgs, y_diag, x_end, decay, from_start))

    for j, (bgs, cgs, y_diag, x_end, decay, from_start) in enumerate(parts):
        rs = slice(n * j, n * (j + 1))
        y_off = [_dot_nt(cgs[g], s_ref[g]) for g in grange]
        for g in grange:
            s_ref[g] = s_ref[g] * decay[g] + _dot_tn(x_end[g], bgs[g])
        for g in grange:
            y_g = jnp.concatenate(
                [y_diag[hd] + y_off[g][:, p_dim * (hd - g * hpg):p_dim * (hd - g * hpg + 1)] * from_start[:, hd:hd + 1]
                 for hd in range(g * hpg, (g + 1) * hpg)], axis=1)
            y_g = y_g * _silu(z_ref[rs, gw * g:gw * (g + 1)])
            ms = jnp.mean(y_g * y_g, axis=-1, keepdims=True)
            o_ref[rs, gw * g:gw * (g + 1)] = y_g * lax.rsqrt(ms + MB_NORM_EPS) * ng_ref[:, gw * g:gw * (g + 1)]


def _mb_ssd(xs, bm, cm, dt, z, dt_bias, a_log, d_skip, norm_g, chunks=1):
    t, di = xs.shape
    gn = bm.shape[1]
    groups = gn // MB_STATE
    tm = MB_CHUNK * chunks
    blk = lambda w: pl.BlockSpec((tm, w), lambda c: (c, 0))
    vec = lambda w: pl.BlockSpec((1, w), lambda c: (0, 0))
    return pl.pallas_call(
        functools.partial(_mb_ssd_kernel, chunks=chunks),
        grid=(t // tm,),
        in_specs=[blk(di), blk(gn), blk(gn), blk(LANES), blk(di), vec(LANES), vec(LANES), vec(di), vec(di)],
        out_specs=blk(di),
        out_shape=jax.ShapeDtypeStruct((t, di), F32),
        scratch_shapes=[pltpu.VMEM((groups, di // groups, MB_STATE), F32)],
        compiler_params=_params(("arbitrary",)),
        name="mb_ssd",
    )(xs, bm, cm, dt, z, dt_bias, a_log, d_skip, norm_g)


def _mamba2_mix(h, ln_g, ln_b, w_in, conv_w, conv_b, dt_bias, a_log, d_skip, norm_g, w_out, wr, br):
    t, d = h.shape
    heads = dt_bias.shape[0]
    di = heads * MB_HEADDIM
    gn = MB_GROUPS * MB_STATE
    cd = di + 2 * gn
    w_bf = w_in.astype(BF16)
    z = _mm(h, w_bf[:, :di], name="mb_in_z")
    xbc = _mm(h, w_bf[:, di:di + cd], name="mb_in_xbc")
    dt = _mm(h, _pad_cols(w_bf[:, di + cd:], LANES), name="mb_in_dt")
    xs, bm, cm = _mb_conv(xbc, conv_w, conv_b, di, gn)
    pad_h = lambda v_: _pad_cols(_row(v_), LANES)
    y = _mb_ssd(xs, bm, cm, dt, z, pad_h(dt_bias), pad_h(a_log),
                _row(jnp.repeat(d_skip, MB_HEADDIM)), _row(norm_g))
    return _mm_res_ln(y, w_out.astype(BF16), h, _row(ln_g), _row(ln_b), wr, br)


def _gla_kernel(qk_ref, vg_ref, lr_ref, w2_ref, b2_ref, ng_ref, o_ref, s_ref, *, chunks):
    c = pl.program_id(0)

    @pl.when(c == 0)
    def _():
        s_ref[...] = jnp.zeros_like(s_ref)

    n = GL_CHUNK
    heads, dv, dk = s_ref.shape
    kd = heads * dk
    vd = heads * dv
    tri = _tri(n)
    r2 = lax.broadcasted_iota(jnp.int32, (n, n), 0)
    c2 = lax.broadcasted_iota(jnp.int32, (n, n), 1)
    incl = c2 <= r2
    hrange = range(heads)
    pre = _dot(lr_ref[...], w2_ref[...]) + b2_ref[...]
    log_alpha = (jnp.minimum(pre, 0.0) - jnp.log(1.0 + jnp.exp(-jnp.abs(pre)))) * (1.0 / GL_GATE_NORM)
    q_in, k_end, v_bf, decay, intra = {}, {}, {}, {}, {}
    for j in range(chunks):
        rs = slice(n * j, n * (j + 1))
        b_all = _dot_f32(tri, log_alpha[rs, :])
        for hd in hrange:
            q = qk_ref[rs, dk * hd:dk * (hd + 1)] * (dk ** -0.5)
            k = qk_ref[rs, kd + dk * hd:kd + dk * (hd + 1)]
            b = b_all[:, dk * hd:dk * (hd + 1)]
            b_end = b[n - 1:n, :]
            q_in[j, hd] = (q * jnp.exp(b)).astype(BF16)
            k_end[j, hd] = (k * jnp.exp(b_end - b)).astype(BF16)
            v_bf[j, hd] = vg_ref[rs, dv * hd:dv * (hd + 1)].astype(BF16)
            decay[j, hd] = jnp.exp(b_end)
            attn = jnp.where(incl, _dot_nt(q_in[j, hd], k * jnp.exp(-b)), 0.0)
            intra[j, hd] = _dot(attn, v_bf[j, hd])
    for j in range(chunks):
        rs = slice(n * j, n * (j + 1))
        inter = [_dot_nt(q_in[j, hd], s_ref[hd]) for hd in hrange]
        for hd in hrange:
            s_ref[hd] = s_ref[hd] * decay[j, hd] + _dot_tn(v_bf[j, hd], k_end[j, hd])
        for hd in hrange:
            o = intra[j, hd] + inter[hd]
            o = o * lax.rsqrt(jnp.mean(o * o, axis=-1, keepdims=True) + GL_NORM_EPS) * ng_ref[...]
            o_ref[rs, dv * hd:dv * (hd + 1)] = o * _silu(vg_ref[rs, vd + dv * hd:vd + dv * (hd + 1)])


def _gla(qk, vg, lr, w2, b2, norm_g, chunks=4):
    t = qk.shape[0]
    kd = qk.shape[1] // 2
    vd = vg.shape[1] // 2
    tm = GL_CHUNK * chunks
    blk = lambda w: pl.BlockSpec((tm, w), lambda c: (c, 0))
    full = lambda a: pl.BlockSpec(a.shape, lambda c: (0, 0))
    return pl.pallas_call(
        functools.partial(_gla_kernel, chunks=chunks),
        grid=(t // tm,),
        in_specs=[blk(2 * kd), blk(2 * vd), blk(LANES), full(w2), full(b2), full(norm_g)],
        out_specs=blk(vd),
        out_shape=jax.ShapeDtypeStruct((t, vd), F32),
        scratch_shapes=[pltpu.VMEM((GL_HEADS, vd // GL_HEADS, kd // GL_HEADS), F32)],
        compiler_params=_params(("arbitrary",)),
        name="gla",
    )(qk, vg, lr, w2, b2, norm_g)


def _gla_mix(h, ln_g, ln_b, w_in, gk_w2, gk_b, norm_g, w_out, wr, br):
    kd = gk_w2.shape[1]
    vd = w_out.shape[0]
    w_bf = w_in.astype(BF16)
    qk = _mm(h, w_bf[:, :2 * kd], name="gl_in_qk")
    vg = _mm(h, w_bf[:, 2 * kd:2 * kd + 2 * vd], name="gl_in_vg")
    lr = _mm(h, _pad_cols(w_bf[:, 2 * kd + 2 * vd:], LANES), name="gl_in_lr")
    y = _gla(qk, vg, lr, _pad_rows(gk_w2, LANES).astype(BF16), _row(gk_b), _row(norm_g))
    return _mm_res_ln(y, w_out.astype(BF16), h, _row(ln_g), _row(ln_b), wr, br)


def _row_copy(src_ref, src_row, dst_ref, dst_row, sem):
    return pltpu.make_async_copy(src_ref.at[pl.ds(src_row, 1)], dst_ref.at[pl.ds(dst_row, 1)], sem)


def _moe_ffn_kernel(be_ref, nb_ref, tok0_ref, tokn_ref, slotp_ref, h_ref, win_ref, wdn_ref, y_ref,
                    xbuf, ybuf, xb_ref, hid_ref, act_ref, gsem, ssem):
    s = pl.program_id(0)
    n_used = nb_ref[0]
    cur = s % 2
    nxt = 1 - cur
    blk = xbuf.shape[1]
    ff = wdn_ref.shape[2]
    n_in = 2 * ff // MOE_PIECE
    n_out = wdn_ref.shape[3] // MOE_PIECE
    rows_g = blk // n_in
    rows_s = blk // n_out

    @pl.when(s == 0)
    def _():
        ybuf[...] = jnp.zeros_like(ybuf)

        def first(r, carry):
            _row_copy(h_ref, tok0_ref[0, 0, r], xbuf.at[0], r, gsem.at[0]).start()
            return carry

        lax.fori_loop(0, blk, first, 0, unroll=8)

    active = s <= n_used

    @pl.when(active)
    def _():
        for r in range(blk):
            _row_copy(h_ref, 0, xbuf.at[cur], r, gsem.at[cur]).wait()
        xb_ref[...] = xbuf[cur].astype(BF16)

    for j in range(n_in):
        @pl.when(active)
        def _(j=j):
            cols = slice(MOE_PIECE * j, MOE_PIECE * (j + 1))
            for r in range(rows_g * j, rows_g * (j + 1)):
                _row_copy(h_ref, tokn_ref[0, 0, r], xbuf.at[nxt], r, gsem.at[nxt]).start()
            hid_ref[:, cols] = jnp.dot(xb_ref[...], win_ref[0, 0, :, cols].astype(BF16),
                                       preferred_element_type=F32)

    @pl.when(active)
    def _():
        act_ref[...] = (_silu(hid_ref[:, :ff]) * hid_ref[:, ff:]).astype(BF16)

    for j in range(n_out):
        @pl.when(active)
        def _(j=j):
            cols = slice(MOE_PIECE * j, MOE_PIECE * (j + 1))
            for r in range(rows_s * j, rows_s * (j + 1)):
                _row_copy(ybuf.at[nxt], r, y_ref, slotp_ref[0, 0, r], ssem).start()
            ybuf[cur, :, cols] = jnp.dot(act_ref[...], wdn_ref[0, 0, :, cols].astype(BF16),
                                         preferred_element_type=F32)

    @pl.when(active)
    def _():
        for r in range(blk):
            _row_copy(ybuf.at[nxt], 0, y_ref, 0, ssem).wait()

    @pl.when(s == n_used)
    def _():
        for r in range(blk):
            _row_copy(h_ref, 0, xbuf.at[nxt], r, gsem.at[nxt]).wait()


def _moe_ffn(block_expert, n_used, tok, slot_prev, h, w_in, w_down, layer, n_out_rows):
    t, d = h.shape
    nb = tok.shape[0]
    ff2 = w_in.shape[3]
    last = nb - 1
    smem_blk = lambda fn: pl.BlockSpec((1, 1, MOE_BLOCK), fn, memory_space=pltpu.SMEM)
    grid_spec = pltpu.PrefetchScalarGridSpec(
        num_scalar_prefetch=2,
        grid=(nb + 1,),
        in_specs=[smem_blk(lambda s, be, nu: (0, 0, 0)),
                  smem_blk(lambda s, be, nu: (jnp.minimum(s + 1, last), 0, 0)),
                  smem_blk(lambda s, be, nu: (s, 0, 0)),
                  pl.BlockSpec(memory_space=pl.ANY),
                  pl.BlockSpec((1, 1, d, ff2), lambda s, be, nu: (layer, be[jnp.minimum(s, last)], 0, 0)),
                  pl.BlockSpec((1, 1, ff2 // 2, d), lambda s, be, nu: (layer, be[jnp.minimum(s, last)], 0, 0))],
        out_specs=pl.BlockSpec(memory_space=pl.ANY),
        scratch_shapes=[pltpu.VMEM((2, MOE_BLOCK, d), F32), pltpu.VMEM((2, MOE_BLOCK, d), F32),
                        pltpu.VMEM((MOE_BLOCK, d), BF16), pltpu.VMEM((MOE_BLOCK, ff2), F32),
                        pltpu.VMEM((MOE_BLOCK, ff2 // 2), BF16),
                        pltpu.SemaphoreType.DMA((2,)), pltpu.SemaphoreType.DMA(())],
    )
    return pl.pallas_call(
        _moe_ffn_kernel,
        grid_spec=grid_spec,
        out_shape=jax.ShapeDtypeStruct((n_out_rows, d), F32),
        compiler_params=_params(("arbitrary",)),
        name="moe_ffn",
    )(block_expert, n_used, tok, tok, slot_prev, h, w_in, w_down)


def _first_argmax(x, rows):
    m = jnp.max(x, axis=0, keepdims=True)
    idx = jnp.min(jnp.where(x == m, rows, x.shape[0]), axis=0, keepdims=True)
    return m, idx


def _softmax_rows(x):
    e = jnp.exp(x - jnp.max(x, axis=0, keepdims=True))
    return e / jnp.sum(e, axis=0, keepdims=True)


def _route_kernel(lg_ref, ids_ref, gt_ref, cnt_ref, carry_ref):
    i = pl.program_id(0)

    @pl.when(i == 0)
    def _():
        carry_ref[...] = jnp.zeros_like(carry_ref)

    tm = lg_ref.shape[1]
    epg = MOE_PER_GROUP
    n_exp = MOE_GROUPS * epg
    rows8 = lax.broadcasted_iota(jnp.int32, (SUBLANES, tm), 0)
    group_probs = _softmax_rows(lg_ref[0:SUBLANES, :])
    p_group, g_idx = _first_argmax(group_probs, rows8)
    sel = jnp.zeros((epg, tm), F32)
    for gi in range(MOE_GROUPS):
        sel = sel + jnp.where(g_idx == gi, lg_ref[SUBLANES + epg * gi:SUBLANES + epg * (gi + 1), :], 0.0)
    p_exp = _softmax_rows(sel)
    p1, i1 = _first_argmax(p_exp, rows8)
    p2, i2 = _first_argmax(jnp.where(rows8 == i1, -1.0, p_exp), rows8)
    denom = p1 + p2
    e0 = g_idx * epg + i1
    e1 = g_idx * epg + i2

    rows_e = lax.broadcasted_iota(jnp.int32, (n_exp, tm), 0)
    oh0 = (rows_e == e0).astype(F32)
    oh1 = (rows_e == e1).astype(F32)
    oh = oh0 + oh1
    tr = lax.broadcasted_iota(jnp.int32, (tm, tm), 0)
    tc = lax.broadcasted_iota(jnp.int32, (tm, tm), 1)
    before = _dot(oh, (tr < tc).astype(F32))
    base = carry_ref[:, 0:1] + before
    rank0 = jnp.sum(oh0 * base, axis=0, keepdims=True)
    rank1 = jnp.sum(oh1 * base, axis=0, keepdims=True)
    carry_ref[...] = carry_ref[...] + jnp.sum(oh, axis=1, keepdims=True)
    cnt_ref[...] = carry_ref[...]

    zi = jnp.zeros((SUBLANES - 4, tm), jnp.int32)
    ids_ref[...] = jnp.concatenate([e0, e1, rank0.astype(jnp.int32), rank1.astype(jnp.int32), zi], axis=0)
    zf = jnp.zeros((SUBLANES - 2, tm), F32)
    gt_ref[...] = jnp.concatenate([p_group * p1 / denom, p_group * p2 / denom, zf], axis=0)


def _route(logits_t, tm=512):
    nr, t = logits_t.shape
    n_exp = MOE_GROUPS * MOE_PER_GROUP
    return pl.pallas_call(
        _route_kernel,
        grid=(t // tm,),
        in_specs=[pl.BlockSpec((nr, tm), lambda i: (0, i))],
        out_specs=[pl.BlockSpec((SUBLANES, tm), lambda i: (0, i)),
                   pl.BlockSpec((SUBLANES, tm), lambda i: (0, i)),
                   pl.BlockSpec((n_exp, LANES), lambda i: (0, 0))],
        out_shape=[jax.ShapeDtypeStruct((SUBLANES, t), jnp.int32),
                   jax.ShapeDtypeStruct((SUBLANES, t), F32),
                   jax.ShapeDtypeStruct((n_exp, LANES), F32)],
        scratch_shapes=[pltpu.VMEM((n_exp, LANES), F32)],
        compiler_params=_params(("arbitrary",)),
        name="moe_route",
    )(logits_t)


def _combine_ln_kernel(y0_ref, y1_ref, gt_ref, res_ref, g_ref, b_ref, o_ref):
    tm = res_ref.shape[0]
    gates = jnp.concatenate([gt_ref[...], jnp.zeros((LANES - SUBLANES, tm), F32)], axis=0).T
    y = y0_ref[...] * gates[:, 0:1] + y1_ref[...] * gates[:, 1:2]
    o_ref[...] = _layer_norm(ALPHA * res_ref[...] + y, g_ref[...], b_ref[...])


def _combine_ln(y_slots, gates, res, g, b, tm=512):
    t, d = res.shape
    nt = t // tm
    row = pl.BlockSpec((tm, d), lambda i: (i, 0))
    vec = pl.BlockSpec((1, d), lambda i: (0, 0))
    return pl.pallas_call(
        _combine_ln_kernel,
        grid=(nt,),
        in_specs=[row, pl.BlockSpec((tm, d), lambda i: (nt + i, 0)),
                  pl.BlockSpec((SUBLANES, tm), lambda i: (0, i)),
                  row, vec, vec],
        out_specs=row,
        out_shape=jax.ShapeDtypeStruct((t, d), F32),
        compiler_params=_params(("parallel",)),
        name="moe_combine_ln",
    )(y_slots, y_slots, gates, res, g, b)


def _hier_moe(h, logits_t, ln_g, ln_b, w_in, w_down, layer):
    t, d = h.shape
    n_exp = w_in.shape[1]
    ids, gates, counts = _route(logits_t)

    counts = counts[:, 0].astype(jnp.int32)
    padded = (counts + MOE_BLOCK - 1) // MOE_BLOCK * MOE_BLOCK
    pad_end = jnp.cumsum(padded)
    pad_start = pad_end - padded
    n_assign = t * MOE_TOPK
    n_blocks = (n_assign + n_exp * (MOE_BLOCK - 1) + MOE_BLOCK - 1) // MOE_BLOCK
    n_rows = n_blocks * MOE_BLOCK
    block_first_row = jnp.arange(n_blocks, dtype=jnp.int32) * MOE_BLOCK
    block_expert = jnp.minimum(jnp.sum(pad_end[None, :] <= block_first_row[:, None], axis=1),
                               n_exp - 1).astype(jnp.int32)
    n_used = (pad_end[-1:] // MOE_BLOCK).astype(jnp.int32)
    experts = jnp.arange(n_exp, dtype=jnp.int32)[:, None, None]
    seg_start = jnp.sum(jnp.where(ids[None, :MOE_TOPK] == experts, pad_start[:, None, None], 0), axis=0)
    dest = (seg_start + ids[MOE_TOPK:2 * MOE_TOPK]).reshape(n_assign)

    code = jnp.zeros((n_rows,), jnp.int32).at[dest].set(
        jnp.arange(1, n_assign + 1, dtype=jnp.int32), unique_indices=True)
    spare = n_assign + jnp.arange(n_rows, dtype=jnp.int32) % MOE_BLOCK
    tok = jnp.where(code > 0, (code - 1) % t, 0).reshape(n_blocks, 1, MOE_BLOCK)
    slot = jnp.where(code > 0, code - 1, spare)
    slot_prev = jnp.concatenate([spare[:MOE_BLOCK], slot]).reshape(n_blocks + 1, 1, MOE_BLOCK)

    y_slots = _moe_ffn(block_expert, n_used, tok, slot_prev, h, w_in, w_down, layer, n_assign + MOE_BLOCK)
    return _combine_ln(y_slots, gates, h, _row(ln_g), _row(ln_b))


def kernel(x, ln_g, ln_b, rw_mu, rw_w_rkv, rw_w0, rw_w1, rw_w2, rw_a0, rw_a1, rw_a2, rw_g1, rw_g2, rw_k_k, rw_k_a, rw_r_k, rw_lnx_g, rw_lnx_b, rw_w_o, mb_w_in, mb_conv_w, mb_conv_b, mb_dt_bias, mb_a_log, mb_d, mb_norm_g, mb_w_out, gl_w_in, gl_gk_w2, gl_gk_b, gl_norm_g, gl_w_out, moe_w_group, moe_b_group, moe_w_route, moe_b_route, moe_w_in, moe_w_down):
    bsz, seq, d = x.shape
    assert bsz == 1
    h = x.reshape(seq, d)
    depth = ln_g.shape[0]
    for i in range(depth):
        kind, j = i % 3, i // 3
        spare = SUBLANES - MOE_GROUPS
        wr = jnp.concatenate([moe_w_group[i].T, jnp.zeros((spare, d), F32), moe_w_route[i].T], axis=0)
        br = jnp.concatenate([moe_b_group[i], jnp.full((spare,), -1e30, F32), moe_b_route[i]]).reshape(-1, 1)
        if kind == 0:
            h, logits = _rwkv7_mix(h, ln_g[i, 0], ln_b[i, 0], rw_mu[j], rw_w_rkv[j], rw_w0[j], rw_w1[j],
                                   rw_w2[j], rw_a0[j], rw_a1[j], rw_a2[j], rw_g1[j], rw_g2[j], rw_k_k[j],
                                   rw_k_a[j], rw_r_k[j], rw_lnx_g[j], rw_lnx_b[j], rw_w_o[j], wr, br)
        elif kind == 1:
            h, logits = _mamba2_mix(h, ln_g[i, 0], ln_b[i, 0], mb_w_in[j], mb_conv_w[j], mb_conv_b[j],
                                    mb_dt_bias[j], mb_a_log[j], mb_d[j], mb_norm_g[j], mb_w_out[j], wr, br)
        else:
            h, logits = _gla_mix(h, ln_g[i, 0], ln_b[i, 0], gl_w_in[j], gl_gk_w2[j], gl_gk_b[j],
                                 gl_norm_g[j], gl_w_out[j], wr, br)
        h = _hier_moe(h, logits, ln_g[i, 1], ln_b[i, 1], moe_w_in, moe_w_down, i)
    return h.reshape(bsz, seq, d)
```

```python
import functools

import jax
import jax.numpy as jnp
from jax import lax
from jax.experimental import pallas as pl
from jax.experimental.pallas import tpu as pltpu

F32 = jnp.float32
BF16 = jnp.bfloat16

LANES = 128
SUBLANES = 8
VMEM_LIMIT = 56 * 1024 * 1024

DEPTH = 4
ALPHA = (2.0 * DEPTH) ** 0.25
LN_EPS = 1e-5

RW_HEAD = 64
RW_CHUNK = 64
RW_GN_EPS = 64e-5

MB_HEADDIM = 64
MB_GROUPS = 4
MB_STATE = 128
MB_CHUNK = 128
MB_CONV = 4
MB_NORM_EPS = 1e-5

GL_HEADS = 4
GL_CHUNK = 64
GL_GATE_NORM = 16.0
GL_NORM_EPS = 1e-5

MOE_GROUPS = 4
MOE_PER_GROUP = 8
MOE_TOPK = 2
MOE_BLOCK = 256
MOE_ISSUE = 64


def _dot(a, b):
    return jnp.dot(a.astype(BF16), b.astype(BF16), preferred_element_type=F32)


def _dot_nt(a, b):
    return lax.dot_general(a.astype(BF16), b.astype(BF16), (((1,), (1,)), ((), ())),
                           preferred_element_type=F32)


def _dot_tn(a, b):
    return lax.dot_general(a.astype(BF16), b.astype(BF16), (((0,), (0,)), ((), ())),
                           preferred_element_type=F32)


def _dot_f32(a, b):
    return jnp.dot(a, b, preferred_element_type=F32, precision=lax.Precision.HIGHEST)


def _sigmoid(x):
    return 1.0 / (1.0 + jnp.exp(-x))


def _softplus(x):
    return jnp.maximum(x, 0.0) + jnp.log(1.0 + jnp.exp(-jnp.abs(x)))


def _silu(x):
    return x * _sigmoid(x)


def _tri(n, dtype=F32):
    r = lax.broadcasted_iota(jnp.int32, (n, n), 0)
    c = lax.broadcasted_iota(jnp.int32, (n, n), 1)
    return (c <= r).astype(dtype)


def _params(sem):
    return pltpu.CompilerParams(dimension_semantics=sem, vmem_limit_bytes=VMEM_LIMIT)


def _pad_cols(w, n):
    return jnp.pad(w, ((0, 0), (0, n - w.shape[1])))


def _pad_rows(w, n):
    return jnp.pad(w, ((0, n - w.shape[0]), (0, 0)))


def _row(v):
    return v.reshape(1, -1).astype(F32)


def _prev_rows_spec(tm, width):
    per = tm // SUBLANES
    return pl.BlockSpec((SUBLANES, width), lambda i: (jnp.maximum(i * per - 1, 0), 0))


def _mm_kernel(x_ref, w_ref, o_ref):
    o_ref[...] = _dot(x_ref[...], w_ref[...])


def _mm(x, w, tm=512, tn=None, name="mm"):
    t, k = x.shape
    n = w.shape[1]
    tn = tn or min(n, 1024)
    return pl.pallas_call(
        _mm_kernel,
        grid=(t // tm, n // tn),
        in_specs=[pl.BlockSpec((tm, k), lambda i, j: (i, 0)),
                  pl.BlockSpec((k, tn), lambda i, j: (0, j))],
        out_specs=pl.BlockSpec((tm, tn), lambda i, j: (i, j)),
        out_shape=jax.ShapeDtypeStruct((t, n), F32),
        compiler_params=_params(("parallel", "parallel")),
        name=name,
    )(x, w)


def _layer_norm(z, g, b):
    mu = jnp.mean(z, axis=-1, keepdims=True)
    zc = z - mu
    var = jnp.mean(zc * zc, axis=-1, keepdims=True)
    return zc * lax.rsqrt(var + LN_EPS) * g + b


def _store_row_tiles(ref, lead, x):
    rows = x.shape[0]
    chunks = x.shape[1] // LANES
    for c in range(chunks):
        ref[lead + (pl.ds(c, rows, stride=chunks), slice(None))] = x[:, LANES * c:LANES * (c + 1)]


def _load_row_tiles(ref, lead, rows, chunks):
    return jnp.concatenate([ref[lead + (pl.ds(c, rows, stride=chunks), slice(None))] for c in range(chunks)],
                           axis=1)


def _mm_res_ln_kernel(x_ref, w_ref, res_ref, g_ref, b_ref, wr_ref, br_ref, o_ref, ot_ref, lg_ref):
    y = _dot(x_ref[...], w_ref[...])
    h = _layer_norm(ALPHA * res_ref[...] + y, g_ref[...], b_ref[...])
    o_ref[...] = h
    _store_row_tiles(ot_ref, (), h)
    lg_ref[...] = lax.dot_general(wr_ref[...], h, (((1,), (1,)), ((), ())), preferred_element_type=F32,
                                  precision=lax.Precision.HIGHEST) + br_ref[...]


def _mm_res_ln(x, w, res, g, b, wr, br, tm=512):
    t, k = x.shape
    d = w.shape[1]
    nr = wr.shape[0]
    ch = d // LANES
    h, h_tiles, logits_t = pl.pallas_call(
        _mm_res_ln_kernel,
        grid=(t // tm,),
        in_specs=[pl.BlockSpec((tm, k), lambda i: (i, 0)),
                  pl.BlockSpec((k, d), lambda i: (0, 0)),
                  pl.BlockSpec((tm, d), lambda i: (i, 0)),
                  pl.BlockSpec((1, d), lambda i: (0, 0)),
                  pl.BlockSpec((1, d), lambda i: (0, 0)),
                  pl.BlockSpec((nr, d), lambda i: (0, 0)),
                  pl.BlockSpec((nr, 1), lambda i: (0, 0))],
        out_specs=[pl.BlockSpec((tm, d), lambda i: (i, 0)),
                   pl.BlockSpec((tm * ch, LANES), lambda i: (i, 0)),
                   pl.BlockSpec((nr, tm), lambda i: (0, i))],
        out_shape=[jax.ShapeDtypeStruct((t, d), F32), jax.ShapeDtypeStruct((t * ch, LANES), F32),
                   jax.ShapeDtypeStruct((nr, t), F32)],
        compiler_params=_params(("parallel",)),
        name="mm_res_ln",
    )(x, w, res, g, b, wr, br)
    return h, h_tiles.reshape(t, ch, LANES), logits_t


def _rwkv_proj_kernel(x_ref, xp_ref, mu_ref, wrkv_ref, w1_ref, w2_ref, w0_ref, a1_ref, a2_ref, a0_ref,
                      g1_ref, g2_ref, r_ref, k_ref, v_ref, lw_ref, a_ref, g_ref):
    i = pl.program_id(0)
    x = x_ref[...]
    prev = jnp.where(i > 0, xp_ref[SUBLANES - 1:SUBLANES, :], 0.0)
    rows = lax.broadcasted_iota(jnp.int32, x.shape, 0)
    shifted = jnp.where(rows == 0, prev, pltpu.roll(x, 1, 0))
    xx = shifted - x

    def mix(p):
        return x + xx * mu_ref[p:p + 1, :]

    r_ref[...] = _dot(mix(0), wrkv_ref[0])
    k_ref[...] = _dot(mix(1), wrkv_ref[1])
    v_ref[...] = _dot(mix(2), wrkv_ref[2])
    w_pre = w0_ref[...] + _dot(jnp.tanh(_dot(mix(3), w1_ref[...])), w2_ref[...])
    lw_ref[...] = -jnp.exp(-_softplus(-w_pre) - 0.5)
    a_ref[...] = _sigmoid(a0_ref[...] + _dot(_dot(mix(4), a1_ref[...]), a2_ref[...]))
    g_ref[...] = _dot(_sigmoid(_dot(mix(5), g1_ref[...])), g2_ref[...])


def _rwkv_proj(x, mu, w_rkv, w1, w2, w0, a1, a2, a0, g1, g2, tm=256):
    t, d = x.shape
    full = lambda a: pl.BlockSpec(a.shape, lambda i: (0,) * a.ndim)
    row = pl.BlockSpec((tm, d), lambda i: (i, 0))
    args = (mu, w_rkv, w1, w2, w0, a1, a2, a0, g1, g2)
    return pl.pallas_call(
        _rwkv_proj_kernel,
        grid=(t // tm,),
        in_specs=[row, _prev_rows_spec(tm, d)] + [full(a) for a in args],
        out_specs=[row] * 6,
        out_shape=[jax.ShapeDtypeStruct((t, d), F32)] * 6,
        compiler_params=_params(("parallel",)),
        name="rwkv_proj",
    )(x, x, *args)


def _rwkv_rec_kernel(r_ref, k_ref, v_ref, lw_ref, a_ref, g_ref, kk_ref, ka_ref, rk_ref, lg_ref, lb_ref,
                     o_ref, s_ref):
    c = pl.program_id(0)

    @pl.when(c == 0)
    def _():
        s_ref[...] = jnp.zeros_like(s_ref)

    n = RW_CHUNK
    m = 2 * n
    pairs = range(r_ref.shape[1] // LANES)
    cs_all = _dot_f32(_tri(n), lw_ref[...])
    lane = lax.broadcasted_iota(jnp.int32, (n, LANES), 1)
    first = lane < RW_HEAD
    r2 = lax.broadcasted_iota(jnp.int32, (m, LANES), 0)
    c2 = lax.broadcasted_iota(jnp.int32, (m, LANES), 1)
    own = (r2 < n) == (c2 < RW_HEAD)
    eye = (c2 == r2).astype(F32)
    r4 = lax.broadcasted_iota(jnp.int32, (2 * m, 2 * m), 0)
    c4 = lax.broadcasted_iota(jnp.int32, (2 * m, 2 * m), 1)
    rr = jnp.where(r4 < m, r4, r4 - m)
    cc = jnp.where(c4 < m, c4, c4 - m)
    keep = cc < rr + jnp.where(r4 < m, 0, 1)

    def stack(x):
        return jnp.where(own, jnp.concatenate([x, x], axis=0), 0.0)

    def head_sum(x):
        sa = jnp.sum(jnp.where(first, x, 0.0), axis=1, keepdims=True)
        sb = jnp.sum(jnp.where(first, 0.0, x), axis=1, keepdims=True)
        return jnp.where(first, sa, sb)

    ar, bk, bkh, vs, g_chunk, bonus = [], [], [], [], [], []
    for p in pairs:
        sl = slice(LANES * p, LANES * (p + 1))
        r = r_ref[:, sl]
        k = k_ref[:, sl]
        v = v_ref[:, sl]
        lw = lw_ref[:, sl]
        a = a_ref[:, sl]
        cs = cs_all[:, sl]
        cs_end = cs[n - 1:n, :]
        kk = k * kk_ref[:, sl]
        kk = kk / jnp.maximum(jnp.sqrt(head_sum(kk * kk)), 1e-12)
        kh = k * (1.0 + (a - 1.0) * ka_ref[:, sl])
        kka = kk * a
        g_inv = jnp.exp(-cs)
        g_end = jnp.exp(cs_end - cs)
        ar.append(jnp.concatenate([stack(-kk * jnp.exp(cs - lw)), stack(r * jnp.exp(cs))], axis=0).astype(BF16))
        bk.append(jnp.concatenate([stack(kka * g_inv), stack(kh * g_inv)], axis=0).astype(BF16))
        bkh.append(jnp.concatenate([stack(kka * g_end), stack(kh * g_end)], axis=0).astype(BF16))
        vs.append(stack(v))
        g_chunk.append(jnp.exp(cs_end))
        bonus.append(head_sum(r * kh * rk_ref[:, sl]) * v)

    gm = [jnp.where(keep, _dot_nt(ar[p], bk[p]), 0.0) for p in pairs]
    nm = [gm[p][:m, :m] for p in pairs]
    akv = [_dot(gm[p][:m, m:], vs[p]) for p in pairs]
    arbk = [gm[p][m:, :].astype(BF16) for p in pairs]

    tm = [eye + nm[p] for p in pairs]
    pw = [_dot(nm[p], nm[p]) for p in pairs]
    for _ in range(4):
        both = [_dot(jnp.concatenate([tm[p], pw[p]], axis=0), pw[p]) for p in pairs]
        tm = [tm[p] + both[p][:m] for p in pairs]
        pw = [both[p][m:] for p in pairs]
    tm = [tm[p] + _dot(tm[p], pw[p]) for p in pairs]

    ars = [_dot_nt(ar[p], s_ref[p]) for p in pairs]
    u = [_dot(tm[p], ars[p][:m] + akv[p]) for p in pairs]
    uv = [jnp.concatenate([u[p], vs[p]], axis=0).astype(BF16) for p in pairs]
    y = [ars[p][m:] + _dot(arbk[p], uv[p]) for p in pairs]
    for p in pairs:
        s_ref[p] = s_ref[p] * g_chunk[p] + _dot_tn(uv[p], bkh[p])

    for p in pairs:
        sl = slice(LANES * p, LANES * (p + 1))
        mu = jnp.sum(y[p], axis=1, keepdims=True) * (1.0 / RW_HEAD)
        yc = jnp.where(own, y[p] - mu, 0.0)
        var = jnp.sum(yc * yc, axis=1, keepdims=True) * (1.0 / RW_HEAD)
        yn = yc * lax.rsqrt(var + RW_GN_EPS)
        yn = yn[:n] + yn[n:]
        o_ref[:, sl] = (yn * lg_ref[:, sl] + lb_ref[:, sl] + bonus[p]) * g_ref[:, sl]


def _rwkv_rec(r, k, v, lw, a, g, k_k, k_a, r_k, lnx_g, lnx_b):
    t, d = r.shape
    row = pl.BlockSpec((RW_CHUNK, d), lambda c: (c, 0))
    vec = pl.BlockSpec((1, d), lambda c: (0, 0))
    return pl.pallas_call(
        _rwkv_rec_kernel,
        grid=(t // RW_CHUNK,),
        in_specs=[row] * 6 + [vec] * 5,
        out_specs=row,
        out_shape=jax.ShapeDtypeStruct((t, d), F32),
        scratch_shapes=[pltpu.VMEM((d // LANES, LANES, LANES), F32)],
        compiler_params=_params(("arbitrary",)),
        name="rwkv_rec",
    )(r, k, v, lw, a, g, k_k, k_a, r_k, lnx_g, lnx_b)


def _rwkv7_mix(h, ln_g, ln_b, mu, w_rkv, w0, w1, w2, a0, a1, a2, g1, g2, k_k, k_a, r_k, lnx_g, lnx_b, w_o,
               wr, br):
    t, d = h.shape
    lora = w1.shape[1]
    lp = -(-lora // LANES) * LANES
    gl = g1.shape[1]
    gp = -(-gl // LANES) * LANES
    r, k, v, lw, a, g = _rwkv_proj(
        h, _pad_rows(mu, SUBLANES), w_rkv.astype(BF16),
        _pad_cols(w1, lp).astype(BF16), _pad_rows(w2, lp).astype(BF16), _row(w0),
        _pad_cols(a1, lp).astype(BF16), _pad_rows(a2, lp).astype(BF16), _row(a0),
        _pad_cols(g1, gp).astype(BF16), _pad_rows(g2, gp).astype(BF16))
    y = _rwkv_rec(r, k, v, lw, a, g, _row(k_k), _row(k_a), _row(r_k), _row(lnx_g), _row(lnx_b))
    return _mm_res_ln(y, w_o.astype(BF16), h, _row(ln_g), _row(ln_b), wr, br)


def _mb_conv_kernel(x_ref, xp_ref, w_ref, b_ref, xs_ref, bm_ref, cm_ref, buf_ref):
    i = pl.program_id(0)
    tm = x_ref.shape[0]
    x = x_ref[...]
    buf_ref[0:SUBLANES, :] = jnp.where(i > 0, xp_ref[...], 0.0)
    buf_ref[SUBLANES:, :] = x
    acc = b_ref[...] + x * w_ref[MB_CONV - 1:MB_CONV, :]
    for s in range(1, MB_CONV):
        acc = acc + buf_ref[pl.ds(SUBLANES - s, tm), :] * w_ref[MB_CONV - 1 - s:MB_CONV - s, :]
    y = _silu(acc)
    di = xs_ref.shape[1]
    gn = bm_ref.shape[1]
    xs_ref[...] = y[:, :di]
    bm_ref[...] = y[:, di:di + gn]
    cm_ref[...] = y[:, di + gn:]


def _mb_conv(xbc, conv_w, conv_b, di, gn, tm=256):
    t, cd = xbc.shape
    return pl.pallas_call(
        _mb_conv_kernel,
        grid=(t // tm,),
        in_specs=[pl.BlockSpec((tm, cd), lambda i: (i, 0)),
                  _prev_rows_spec(tm, cd),
                  pl.BlockSpec((SUBLANES, cd), lambda i: (0, 0)),
                  pl.BlockSpec((1, cd), lambda i: (0, 0))],
        out_specs=[pl.BlockSpec((tm, di), lambda i: (i, 0)),
                   pl.BlockSpec((tm, gn), lambda i: (i, 0)),
                   pl.BlockSpec((tm, gn), lambda i: (i, 0))],
        out_shape=[jax.ShapeDtypeStruct((t, di), F32),
                   jax.ShapeDtypeStruct((t, gn), F32),
                   jax.ShapeDtypeStruct((t, gn), F32)],
        scratch_shapes=[pltpu.VMEM((tm + SUBLANES, cd), F32)],
        compiler_params=_params(("parallel",)),
        name="mb_conv",
    )(xbc, xbc, _pad_rows(conv_w, SUBLANES), _row(conv_b))


def _mb_ssd_kernel(xs_ref, bm_ref, cm_ref, dt_ref, z_ref, dtb_ref, alog_ref, dsk_ref, ng_ref, o_ref, s_ref,
                   *, chunks):
    c = pl.program_id(0)

    @pl.when(c == 0)
    def _():
        s_ref[...] = jnp.zeros_like(s_ref)

    n = MB_CHUNK
    p_dim = MB_HEADDIM
    groups = s_ref.shape[0]
    hpg = s_ref.shape[1] // p_dim
    gw = hpg * p_dim
    grange = range(groups)
    heads = range(groups * hpg)
    r2 = lax.broadcasted_iota(jnp.int32, (n, n), 0)
    c2 = lax.broadcasted_iota(jnp.int32, (n, n), 1)
    incl = c2 <= r2
    tri = _tri(n)
    a_neg = -jnp.exp(alog_ref[...])

    parts = []
    for j in range(chunks):
        rs = slice(n * j, n * (j + 1))
        dt = _softplus(dt_ref[rs, :] + dtb_ref[...])
        cs = _dot_f32(tri, dt * a_neg)
        cs_t = cs.T
        cs_end = cs[n - 1:n, :]
        to_end = jnp.exp(cs_end - cs)
        from_start = jnp.exp(cs)
        chunk_decay = jnp.exp(cs_end)
        bgs = [bm_ref[rs, MB_STATE * g:MB_STATE * (g + 1)].astype(BF16) for g in grange]
        cgs = [cm_ref[rs, MB_STATE * g:MB_STATE * (g + 1)].astype(BF16) for g in grange]
        scores = [_dot_nt(cgs[g], bgs[g]) for g in grange]
        x_hs = [xs_ref[rs, p_dim * hd:p_dim * (hd + 1)] for hd in heads]
        xdt = [x_hs[hd] * dt[:, hd:hd + 1] for hd in heads]
        y_diag = []
        for hd in heads:
            seg = jnp.minimum(cs[:, hd:hd + 1] - cs_t[hd:hd + 1, :], 0.0)
            y_diag.append(_dot(scores[hd // hpg] * jnp.where(incl, jnp.exp(seg), 0.0), xdt[hd])
                          + x_hs[hd] * dsk_ref[:, p_dim * hd:p_dim * (hd + 1)])
        x_end, decay = [], []
        for g in grange:
            hs = range(g * hpg, (g + 1) * hpg)
            x_end.append(jnp.concatenate([xdt[hd] * to_end[:, hd:hd + 1] for hd in hs], axis=1).astype(BF16))
            decay.append(jnp.concatenate(
                [jnp.broadcast_to(chunk_decay[:, hd:hd + 1], (p_dim, 1)) for hd in hs], axis=0))
        parts.append((bgs, cgs, y_diag, x_end, decay, from_start))

    for j, (bgs, cgs, y_diag, x_end, decay, from_start) in enumerate(parts):
        rs = slice(n * j, n * (j + 1))
        y_off = [_dot_nt(cgs[g], s_ref[g]) for g in grange]
        for g in grange:
            s_ref[g] = s_ref[g] * decay[g] + _dot_tn(x_end[g], bgs[g])
        for g in grange:
            y_g = jnp.concatenate(
                [y_diag[hd] + y_off[g][:, p_dim * (hd - g * hpg):p_dim * (hd - g * hpg + 1)] * from_start[:, hd:hd + 1]
                 for hd in range(g * hpg, (g + 1) * hpg)], axis=1)
            y_g = y_g * _silu(z_ref[rs, gw * g:gw * (g + 1)])
            ms = jnp.mean(y_g * y_g, axis=-1, keepdims=True)
            o_ref[rs, gw * g:gw * (g + 1)] = y_g * lax.rsqrt(ms + MB_NORM_EPS) * ng_ref[:, gw * g:gw * (g + 1)]


def _mb_ssd(xs, bm, cm, dt, z, dt_bias, a_log, d_skip, norm_g, chunks=1):
    t, di = xs.shape
    gn = bm.shape[1]
    groups = gn // MB_STATE
    tm = MB_CHUNK * chunks
    blk = lambda w: pl.BlockSpec((tm, w), lambda c: (c, 0))
    vec = lambda w: pl.BlockSpec((1, w), lambda c: (0, 0))
    return pl.pallas_call(
        functools.partial(_mb_ssd_kernel, chunks=chunks),
        grid=(t // tm,),
        in_specs=[blk(di), blk(gn), blk(gn), blk(LANES), blk(di), vec(LANES), vec(LANES), vec(di), vec(di)],
        out_specs=blk(di),
        out_shape=jax.ShapeDtypeStruct((t, di), F32),
        scratch_shapes=[pltpu.VMEM((groups, di // groups, MB_STATE), F32)],
        compiler_params=_params(("arbitrary",)),
        name="mb_ssd",
    )(xs, bm, cm, dt, z, dt_bias, a_log, d_skip, norm_g)


def _mamba2_mix(h, ln_g, ln_b, w_in, conv_w, conv_b, dt_bias, a_log, d_skip, norm_g, w_out, wr, br):
    t, d = h.shape
    heads = dt_bias.shape[0]
    di = heads * MB_HEADDIM
    gn = MB_GROUPS * MB_STATE
    cd = di + 2 * gn
    w_bf = w_in.astype(BF16)
    z = _mm(h, w_bf[:, :di], name="mb_in_z")
    xbc = _mm(h, w_bf[:, di:di + cd], name="mb_in_xbc")
    dt = _mm(h, _pad_cols(w_bf[:, di + cd:], LANES), name="mb_in_dt")
    xs, bm, cm = _mb_conv(xbc, conv_w, conv_b, di, gn)
    pad_h = lambda v_: _pad_cols(_row(v_), LANES)
    y = _mb_ssd(xs, bm, cm, dt, z, pad_h(dt_bias), pad_h(a_log),
                _row(jnp.repeat(d_skip, MB_HEADDIM)), _row(norm_g))
    return _mm_res_ln(y, w_out.astype(BF16), h, _row(ln_g), _row(ln_b), wr, br)


def _gla_kernel(qk_ref, vg_ref, lr_ref, w2_ref, b2_ref, ng_ref, o_ref, s_ref, *, chunks):
    c = pl.program_id(0)

    @pl.when(c == 0)
    def _():
        s_ref[...] = jnp.zeros_like(s_ref)

    n = GL_CHUNK
    heads, dv, dk = s_ref.shape
    kd = heads * dk
    vd = heads * dv
    tri = _tri(n)
    r2 = lax.broadcasted_iota(jnp.int32, (n, n), 0)
    c2 = lax.broadcasted_iota(jnp.int32, (n, n), 1)
    incl = c2 <= r2
    hrange = range(heads)
    pre = _dot(lr_ref[...], w2_ref[...]) + b2_ref[...]
    log_alpha = (jnp.minimum(pre, 0.0) - jnp.log(1.0 + jnp.exp(-jnp.abs(pre)))) * (1.0 / GL_GATE_NORM)
    q_in, k_end, v_bf, decay, intra = {}, {}, {}, {}, {}
    for j in range(chunks):
        rs = slice(n * j, n * (j + 1))
        b_all = _dot_f32(tri, log_alpha[rs, :])
        for hd in hrange:
            q = qk_ref[rs, dk * hd:dk * (hd + 1)] * (dk ** -0.5)
            k = qk_ref[rs, kd + dk * hd:kd + dk * (hd + 1)]
            b = b_all[:, dk * hd:dk * (hd + 1)]
            b_end = b[n - 1:n, :]
            q_in[j, hd] = (q * jnp.exp(b)).astype(BF16)
            k_end[j, hd] = (k * jnp.exp(b_end - b)).astype(BF16)
            v_bf[j, hd] = vg_ref[rs, dv * hd:dv * (hd + 1)].astype(BF16)
            decay[j, hd] = jnp.exp(b_end)
            attn = jnp.where(incl, _dot_nt(q_in[j, hd], k * jnp.exp(-b)), 0.0)
            intra[j, hd] = _dot(attn, v_bf[j, hd])
    for j in range(chunks):
        rs = slice(n * j, n * (j + 1))
        inter = [_dot_nt(q_in[j, hd], s_ref[hd]) for hd in hrange]
        for hd in hrange:
            s_ref[hd] = s_ref[hd] * decay[j, hd] + _dot_tn(v_bf[j, hd], k_end[j, hd])
        for hd in hrange:
            o = intra[j, hd] + inter[hd]
            o = o * lax.rsqrt(jnp.mean(o * o, axis=-1, keepdims=True) + GL_NORM_EPS) * ng_ref[...]
            o_ref[rs, dv * hd:dv * (hd + 1)] = o * _silu(vg_ref[rs, vd + dv * hd:vd + dv * (hd + 1)])


def _gla(qk, vg, lr, w2, b2, norm_g, chunks=4):
    t = qk.shape[0]
    kd = qk.shape[1] // 2
    vd = vg.shape[1] // 2
    tm = GL_CHUNK * chunks
    blk = lambda w: pl.BlockSpec((tm, w), lambda c: (c, 0))
    full = lambda a: pl.BlockSpec(a.shape, lambda c: (0, 0))
    return pl.pallas_call(
        functools.partial(_gla_kernel, chunks=chunks),
        grid=(t // tm,),
        in_specs=[blk(2 * kd), blk(2 * vd), blk(LANES), full(w2), full(b2), full(norm_g)],
        out_specs=blk(vd),
        out_shape=jax.ShapeDtypeStruct((t, vd), F32),
        scratch_shapes=[pltpu.VMEM((GL_HEADS, vd // GL_HEADS, kd // GL_HEADS), F32)],
        compiler_params=_params(("arbitrary",)),
        name="gla",
    )(qk, vg, lr, w2, b2, norm_g)


def _gla_mix(h, ln_g, ln_b, w_in, gk_w2, gk_b, norm_g, w_out, wr, br):
    kd = gk_w2.shape[1]
    vd = w_out.shape[0]
    w_bf = w_in.astype(BF16)
    qk = _mm(h, w_bf[:, :2 * kd], name="gl_in_qk")
    vg = _mm(h, w_bf[:, 2 * kd:2 * kd + 2 * vd], name="gl_in_vg")
    lr = _mm(h, _pad_cols(w_bf[:, 2 * kd + 2 * vd:], LANES), name="gl_in_lr")
    y = _gla(qk, vg, lr, _pad_rows(gk_w2, LANES).astype(BF16), _row(gk_b), _row(norm_g))
    return _mm_res_ln(y, w_out.astype(BF16), h, _row(ln_g), _row(ln_b), wr, br)


def _moe_ffn_kernel(be_ref, nb_ref, tok0_ref, tokn_ref, slotp_ref, h_ref, win_ref, wdn_ref, y_ref,
                    xbuf, ybuf, gsem, ssem):
    s = pl.program_id(0)
    n_used = nb_ref[0]
    cur = s % 2
    nxt = 1 - cur
    ch = h_ref.shape[1]
    blk = xbuf.shape[1] // ch
    ff = wdn_ref.shape[2]

    def tile(buf, slot, r):
        return buf.at[slot, pl.ds(pl.multiple_of(r * ch, ch), ch)]

    def gather(tok, slot, r):
        return pltpu.make_async_copy(h_ref.at[tok], tile(xbuf, slot, r), gsem.at[slot])

    def scatter(slot, r, row):
        return pltpu.make_async_copy(tile(ybuf, slot, r), y_ref.at[row], ssem)

    @pl.when(s == 0)
    def _():
        ybuf[...] = jnp.zeros_like(ybuf)

        def first(r, carry):
            gather(tok0_ref[0, 0, r], 0, r).start()
            return carry

        lax.fori_loop(0, blk, first, 0, unroll=8)

    @pl.when(s <= n_used)
    def _():
        def issue(i, carry):
            for u in range(MOE_ISSUE):
                r = i * MOE_ISSUE + u
                scatter(nxt, r, slotp_ref[0, 0, r]).start()
                gather(tokn_ref[0, 0, r], nxt, r).start()
            return carry

        lax.fori_loop(0, blk // MOE_ISSUE, issue, 0)

        for r in range(blk):
            gather(0, cur, r).wait()
        x = _load_row_tiles(xbuf, (cur,), blk, ch)
        hcat = jnp.dot(x.astype(BF16), win_ref[0, 0].astype(BF16), preferred_element_type=F32)
        act = (_silu(hcat[:, :ff]) * hcat[:, ff:]).astype(BF16)
        _store_row_tiles(ybuf, (cur,), jnp.dot(act, wdn_ref[0, 0].astype(BF16), preferred_element_type=F32))

        for r in range(blk):
            scatter(nxt, r, 0).wait()

    @pl.when(s == n_used)
    def _():
        for r in range(blk):
            gather(0, nxt, r).wait()


def _moe_ffn(block_expert, n_used, tok, slot_prev, h_tiles, w_in, w_down, layer, n_out_rows):
    t, ch, _ = h_tiles.shape
    d = ch * LANES
    nb = tok.shape[0]
    ff2 = w_in.shape[3]
    last = nb - 1
    smem_blk = lambda fn: pl.BlockSpec((1, 1, MOE_BLOCK), fn, memory_space=pltpu.SMEM)
    grid_spec = pltpu.PrefetchScalarGridSpec(
        num_scalar_prefetch=2,
        grid=(nb + 1,),
        in_specs=[smem_blk(lambda s, be, nu: (0, 0, 0)),
                  smem_blk(lambda s, be, nu: (jnp.minimum(s + 1, last), 0, 0)),
                  smem_blk(lambda s, be, nu: (s, 0, 0)),
                  pl.BlockSpec(memory_space=pl.ANY),
                  pl.BlockSpec((1, 1, d, ff2), lambda s, be, nu: (layer, be[jnp.minimum(s, last)], 0, 0)),
                  pl.BlockSpec((1, 1, ff2 // 2, d), lambda s, be, nu: (layer, be[jnp.minimum(s, last)], 0, 0))],
        out_specs=pl.BlockSpec(memory_space=pl.ANY),
        scratch_shapes=[pltpu.VMEM((2, MOE_BLOCK * ch, LANES), F32), pltpu.VMEM((2, MOE_BLOCK * ch, LANES), F32),
                        pltpu.SemaphoreType.DMA((2,)), pltpu.SemaphoreType.DMA(())],
    )
    return pl.pallas_call(
        _moe_ffn_kernel,
        grid_spec=grid_spec,
        out_shape=jax.ShapeDtypeStruct((n_out_rows, ch, LANES), F32),
        compiler_params=_params(("arbitrary",)),
        name="moe_ffn",
    )(block_expert, n_used, tok, tok, slot_prev, h_tiles, w_in, w_down)


def _first_argmax(x, rows):
    m = jnp.max(x, axis=0, keepdims=True)
    idx = jnp.min(jnp.where(x == m, rows, x.shape[0]), axis=0, keepdims=True)
    return m, idx


def _softmax_rows(x):
    e = jnp.exp(x - jnp.max(x, axis=0, keepdims=True))
    return e / jnp.sum(e, axis=0, keepdims=True)


def _route_kernel(lg_ref, ids_ref, gt_ref, cnt_ref, carry_ref):
    i = pl.program_id(0)

    @pl.when(i == 0)
    def _():
        carry_ref[...] = jnp.zeros_like(carry_ref)

    tm = lg_ref.shape[1]
    epg = MOE_PER_GROUP
    n_exp = MOE_GROUPS * epg
    rows8 = lax.broadcasted_iota(jnp.int32, (SUBLANES, tm), 0)
    group_probs = _softmax_rows(lg_ref[0:SUBLANES, :])
    p_group, g_idx = _first_argmax(group_probs, rows8)
    sel = jnp.zeros((epg, tm), F32)
    for gi in range(MOE_GROUPS):
        sel = sel + jnp.where(g_idx == gi, lg_ref[SUBLANES + epg * gi:SUBLANES + epg * (gi + 1), :], 0.0)
    p_exp = _softmax_rows(sel)
    p1, i1 = _first_argmax(p_exp, rows8)
    p2, i2 = _first_argmax(jnp.where(rows8 == i1, -1.0, p_exp), rows8)
    denom = p1 + p2
    e0 = g_idx * epg + i1
    e1 = g_idx * epg + i2

    rows_e = lax.broadcasted_iota(jnp.int32, (n_exp, tm), 0)
    oh0 = (rows_e == e0).astype(F32)
    oh1 = (rows_e == e1).astype(F32)
    oh = oh0 + oh1
    tr = lax.broadcasted_iota(jnp.int32, (tm, tm), 0)
    tc = lax.broadcasted_iota(jnp.int32, (tm, tm), 1)
    before = _dot(oh, (tr < tc).astype(F32))
    base = carry_ref[:, 0:1] + before
    rank0 = jnp.sum(oh0 * base, axis=0, keepdims=True)
    rank1 = jnp.sum(oh1 * base, axis=0, keepdims=True)
    carry_ref[...] = carry_ref[...] + jnp.sum(oh, axis=1, keepdims=True)
    cnt_ref[...] = carry_ref[...]

    zi = jnp.zeros((SUBLANES - 4, tm), jnp.int32)
    ids_ref[...] = jnp.concatenate([e0, e1, rank0.astype(jnp.int32), rank1.astype(jnp.int32), zi], axis=0)
    zf = jnp.zeros((SUBLANES - 2, tm), F32)
    gt_ref[...] = jnp.concatenate([p_group * p1 / denom, p_group * p2 / denom, zf], axis=0)


def _route(logits_t, tm=512):
    nr, t = logits_t.shape
    n_exp = MOE_GROUPS * MOE_PER_GROUP
    return pl.pallas_call(
        _route_kernel,
        grid=(t // tm,),
        in_specs=[pl.BlockSpec((nr, tm), lambda i: (0, i))],
        out_specs=[pl.BlockSpec((SUBLANES, tm), lambda i: (0, i)),
                   pl.BlockSpec((SUBLANES, tm), lambda i: (0, i)),
                   pl.BlockSpec((n_exp, LANES), lambda i: (0, 0))],
        out_shape=[jax.ShapeDtypeStruct((SUBLANES, t), jnp.int32),
                   jax.ShapeDtypeStruct((SUBLANES, t), F32),
                   jax.ShapeDtypeStruct((n_exp, LANES), F32)],
        scratch_shapes=[pltpu.VMEM((n_exp, LANES), F32)],
        compiler_params=_params(("arbitrary",)),
        name="moe_route",
    )(logits_t)


def _combine_ln_kernel(y0_ref, y1_ref, gt_ref, res_ref, g_ref, b_ref, o_ref):
    tm, d = res_ref.shape
    ch = d // LANES
    gates = jnp.concatenate([gt_ref[...], jnp.zeros((LANES - SUBLANES, tm), F32)], axis=0).T
    y = (_load_row_tiles(y0_ref, (), tm, ch) * gates[:, 0:1]
         + _load_row_tiles(y1_ref, (), tm, ch) * gates[:, 1:2])
    o_ref[...] = _layer_norm(ALPHA * res_ref[...] + y, g_ref[...], b_ref[...])


def _combine_ln(y_slots, gates, res, g, b, tm=512):
    t, d = res.shape
    nt = t // tm
    ch = d // LANES
    y2d = y_slots.reshape(-1, LANES)
    row = pl.BlockSpec((tm, d), lambda i: (i, 0))
    vec = pl.BlockSpec((1, d), lambda i: (0, 0))
    return pl.pallas_call(
        _combine_ln_kernel,
        grid=(nt,),
        in_specs=[pl.BlockSpec((tm * ch, LANES), lambda i: (i, 0)),
                  pl.BlockSpec((tm * ch, LANES), lambda i: (nt + i, 0)),
                  pl.BlockSpec((SUBLANES, tm), lambda i: (0, i)),
                  row, vec, vec],
        out_specs=row,
        out_shape=jax.ShapeDtypeStruct((t, d), F32),
        compiler_params=_params(("parallel",)),
        name="moe_combine_ln",
    )(y2d, y2d, gates, res, g, b)


def _hier_moe(h, h_tiles, logits_t, ln_g, ln_b, w_in, w_down, layer):
    t, d = h.shape
    n_exp = w_in.shape[1]
    ids, gates, counts = _route(logits_t)

    counts = counts[:, 0].astype(jnp.int32)
    padded = (counts + MOE_BLOCK - 1) // MOE_BLOCK * MOE_BLOCK
    pad_end = jnp.cumsum(padded)
    pad_start = pad_end - padded
    n_assign = t * MOE_TOPK
    n_blocks = (n_assign + n_exp * (MOE_BLOCK - 1) + MOE_BLOCK - 1) // MOE_BLOCK
    n_rows = n_blocks * MOE_BLOCK
    block_first_row = jnp.arange(n_blocks, dtype=jnp.int32) * MOE_BLOCK
    block_expert = jnp.minimum(jnp.sum(pad_end[None, :] <= block_first_row[:, None], axis=1),
                               n_exp - 1).astype(jnp.int32)
    n_used = (pad_end[-1:] // MOE_BLOCK).astype(jnp.int32)
    experts = jnp.arange(n_exp, dtype=jnp.int32)[:, None, None]
    seg_start = jnp.sum(jnp.where(ids[None, :MOE_TOPK] == experts, pad_start[:, None, None], 0), axis=0)
    dest = (seg_start + ids[MOE_TOPK:2 * MOE_TOPK]).reshape(n_assign)

    code = jnp.zeros((n_rows,), jnp.int32).at[dest].set(
        jnp.arange(1, n_assign + 1, dtype=jnp.int32), unique_indices=True)
    spare = n_assign + jnp.arange(n_rows, dtype=jnp.int32) % MOE_BLOCK
    tok = jnp.where(code > 0, (code - 1) % t, 0).reshape(n_blocks, 1, MOE_BLOCK)
    slot = jnp.where(code > 0, code - 1, spare)
    slot_prev = jnp.concatenate([spare[:MOE_BLOCK], slot]).reshape(n_blocks + 1, 1, MOE_BLOCK)

    y_slots = _moe_ffn(block_expert, n_used, tok, slot_prev, h_tiles, w_in, w_down, layer,
                       n_assign + MOE_BLOCK)
    return _combine_ln(y_slots, gates, h, _row(ln_g), _row(ln_b))


def kernel(x, ln_g, ln_b, rw_mu, rw_w_rkv, rw_w0, rw_w1, rw_w2, rw_a0, rw_a1, rw_a2, rw_g1, rw_g2, rw_k_k, rw_k_a, rw_r_k, rw_lnx_g, rw_lnx_b, rw_w_o, mb_w_in, mb_conv_w, mb_conv_b, mb_dt_bias, mb_a_log, mb_d, mb_norm_g, mb_w_out, gl_w_in, gl_gk_w2, gl_gk_b, gl_norm_g, gl_w_out, moe_w_group, moe_b_group, moe_w_route, moe_b_route, moe_w_in, moe_w_down):
    bsz, seq, d = x.shape
    assert bsz == 1
    h = x.reshape(seq, d)
    depth = ln_g.shape[0]
    for i in range(depth):
        kind, j = i % 3, i // 3
        spare = SUBLANES - MOE_GROUPS
        wr = jnp.concatenate([moe_w_group[i].T, jnp.zeros((spare, d), F32), moe_w_route[i].T], axis=0)
        br = jnp.concatenate([moe_b_group[i], jnp.full((spare,), -1e30, F32), moe_b_route[i]]).reshape(-1, 1)
        if kind == 0:
            h, h_tiles, logits =_rwkv7_mix(h, ln_g[i, 0], ln_b[i, 0], rw_mu[j], rw_w_rkv[j], rw_w0[j], rw_w1[j],
                                   rw_w2[j], rw_a0[j], rw_a1[j], rw_a2[j], rw_g1[j], rw_g2[j], rw_k_k[j],
                                   rw_k_a[j], rw_r_k[j], rw_lnx_g[j], rw_lnx_b[j], rw_w_o[j], wr, br)
        elif kind == 1:
            h, h_tiles, logits =_mamba2_mix(h, ln_g[i, 0], ln_b[i, 0], mb_w_in[j], mb_conv_w[j], mb_conv_b[j],
                                    mb_dt_bias[j], mb_a_log[j], mb_d[j], mb_norm_g[j], mb_w_out[j], wr, br)
        else:
            h, h_tiles, logits =_gla_mix(h, ln_g[i, 0], ln_b[i, 0], gl_w_in[j], gl_gk_w2[j], gl_gk_b[j],
                                 gl_norm_g[j], gl_w_out[j], wr, br)
        h = _hier_moe(h, h_tiles, logits, ln_g[i, 1], ln_b[i, 1], moe_w_in, moe_w_down, i)
    return h.reshape(bsz, seq, d)
```

```python
import functools

import jax
import jax.numpy as jnp
from jax import lax
from jax.experimental import pallas as pl
from jax.experimental.pallas import tpu as pltpu

F32 = jnp.float32
BF16 = jnp.bfloat16

LANES = 128
SUBLANES = 8
VMEM_LIMIT = 56 * 1024 * 1024

DEPTH = 4
ALPHA = (2.0 * DEPTH) ** 0.25
LN_EPS = 1e-5

RW_HEAD = 64
RW_CHUNK = 64
RW_GN_EPS = 64e-5

MB_HEADDIM = 64
MB_GROUPS = 4
MB_STATE = 128
MB_CHUNK = 128
MB_CONV = 4
MB_NORM_EPS = 1e-5

GL_HEADS = 4
GL_CHUNK = 64
GL_GATE_NORM = 16.0
GL_NORM_EPS = 1e-5

MOE_GROUPS = 4
MOE_PER_GROUP = 8
MOE_TOPK = 2
MOE_BLOCK = 256
MOE_ISSUE = 64


def _dot(a, b):
    return jnp.dot(a.astype(BF16), b.astype(BF16), preferred_element_type=F32)


def _dot_nt(a, b):
    return lax.dot_general(a.astype(BF16), b.astype(BF16), (((1,), (1,)), ((), ())),
                           preferred_element_type=F32)


def _dot_tn(a, b):
    return lax.dot_general(a.astype(BF16), b.astype(BF16), (((0,), (0,)), ((), ())),
                           preferred_element_type=F32)


def _dot_f32(a, b):
    return jnp.dot(a, b, preferred_element_type=F32, precision=lax.Precision.HIGHEST)


def _sigmoid(x):
    return 1.0 / (1.0 + jnp.exp(-x))


def _softplus(x):
    return jnp.maximum(x, 0.0) + jnp.log(1.0 + jnp.exp(-jnp.abs(x)))


def _silu(x):
    return x * _sigmoid(x)


def _tri(n, dtype=F32):
    r = lax.broadcasted_iota(jnp.int32, (n, n), 0)
    c = lax.broadcasted_iota(jnp.int32, (n, n), 1)
    return (c <= r).astype(dtype)


def _params(sem):
    return pltpu.CompilerParams(dimension_semantics=sem, vmem_limit_bytes=VMEM_LIMIT)


def _pad_cols(w, n):
    return jnp.pad(w, ((0, 0), (0, n - w.shape[1])))


def _pad_rows(w, n):
    return jnp.pad(w, ((0, n - w.shape[0]), (0, 0)))


def _row(v):
    return v.reshape(1, -1).astype(F32)


def _prev_rows_spec(tm, width):
    per = tm // SUBLANES
    return pl.BlockSpec((SUBLANES, width), lambda i: (jnp.maximum(i * per - 1, 0), 0))


def _mm_kernel(x_ref, w_ref, o_ref):
    o_ref[...] = _dot(x_ref[...], w_ref[...])


def _mm(x, w, tm=1024, tn=None, name="mm"):
    t, k = x.shape
    n = w.shape[1]
    tn = tn or min(n, 1024)
    return pl.pallas_call(
        _mm_kernel,
        grid=(t // tm, n // tn),
        in_specs=[pl.BlockSpec((tm, k), lambda i, j: (i, 0)),
                  pl.BlockSpec((k, tn), lambda i, j: (0, j))],
        out_specs=pl.BlockSpec((tm, tn), lambda i, j: (i, j)),
        out_shape=jax.ShapeDtypeStruct((t, n), F32),
        compiler_params=_params(("parallel", "parallel")),
        name=name,
    )(x, w)


def _layer_norm(z, g, b):
    mu = jnp.mean(z, axis=-1, keepdims=True)
    zc = z - mu
    var = jnp.mean(zc * zc, axis=-1, keepdims=True)
    return zc * lax.rsqrt(var + LN_EPS) * g + b


def _store_row_tiles(ref, lead, x):
    rows = x.shape[0]
    chunks = x.shape[1] // LANES
    for c in range(chunks):
        ref[lead + (pl.ds(c, rows, stride=chunks), slice(None))] = x[:, LANES * c:LANES * (c + 1)]


def _load_row_tiles(ref, lead, rows, chunks):
    return jnp.concatenate([ref[lead + (pl.ds(c, rows, stride=chunks), slice(None))] for c in range(chunks)],
                           axis=1)


def _mm_res_ln_kernel(x_ref, w_ref, res_ref, g_ref, b_ref, wr_ref, br_ref, o_ref, ot_ref, lg_ref):
    y = _dot(x_ref[...], w_ref[...])
    h = _layer_norm(ALPHA * res_ref[...] + y, g_ref[...], b_ref[...])
    o_ref[...] = h
    _store_row_tiles(ot_ref, (), h)
    lg_ref[...] = lax.dot_general(wr_ref[...], h, (((1,), (1,)), ((), ())), preferred_element_type=F32,
                                  precision=lax.Precision.HIGHEST) + br_ref[...]


def _mm_res_ln(x, w, res, g, b, wr, br, tm=512):
    t, k = x.shape
    d = w.shape[1]
    nr = wr.shape[0]
    ch = d // LANES
    h, h_tiles, logits_t = pl.pallas_call(
        _mm_res_ln_kernel,
        grid=(t // tm,),
        in_specs=[pl.BlockSpec((tm, k), lambda i: (i, 0)),
                  pl.BlockSpec((k, d), lambda i: (0, 0)),
                  pl.BlockSpec((tm, d), lambda i: (i, 0)),
                  pl.BlockSpec((1, d), lambda i: (0, 0)),
                  pl.BlockSpec((1, d), lambda i: (0, 0)),
                  pl.BlockSpec((nr, d), lambda i: (0, 0)),
                  pl.BlockSpec((nr, 1), lambda i: (0, 0))],
        out_specs=[pl.BlockSpec((tm, d), lambda i: (i, 0)),
                   pl.BlockSpec((tm * ch, LANES), lambda i: (i, 0)),
                   pl.BlockSpec((nr, tm), lambda i: (0, i))],
        out_shape=[jax.ShapeDtypeStruct((t, d), F32), jax.ShapeDtypeStruct((t * ch, LANES), F32),
                   jax.ShapeDtypeStruct((nr, t), F32)],
        compiler_params=_params(("parallel",)),
        name="mm_res_ln",
    )(x, w, res, g, b, wr, br)
    return h, h_tiles.reshape(t, ch, LANES), logits_t


def _rwkv_proj_kernel(x_ref, xp_ref, mu_ref, wrkv_ref, w1_ref, w2_ref, w0_ref, a1_ref, a2_ref, a0_ref,
                      g1_ref, g2_ref, r_ref, k_ref, v_ref, lw_ref, a_ref, g_ref):
    i = pl.program_id(0)
    x = x_ref[...]
    prev = jnp.where(i > 0, xp_ref[SUBLANES - 1:SUBLANES, :], 0.0)
    rows = lax.broadcasted_iota(jnp.int32, x.shape, 0)
    shifted = jnp.where(rows == 0, prev, pltpu.roll(x, 1, 0))
    xx = shifted - x

    def mix(p):
        return x + xx * mu_ref[p:p + 1, :]

    r_ref[...] = _dot(mix(0), wrkv_ref[0])
    k_ref[...] = _dot(mix(1), wrkv_ref[1])
    v_ref[...] = _dot(mix(2), wrkv_ref[2])
    w_pre = w0_ref[...] + _dot(jnp.tanh(_dot(mix(3), w1_ref[...])), w2_ref[...])
    lw_ref[...] = -jnp.exp(-_softplus(-w_pre) - 0.5)
    a_ref[...] = _sigmoid(a0_ref[...] + _dot(_dot(mix(4), a1_ref[...]), a2_ref[...]))
    g_ref[...] = _dot(_sigmoid(_dot(mix(5), g1_ref[...])), g2_ref[...])


def _rwkv_proj(x, mu, w_rkv, w1, w2, w0, a1, a2, a0, g1, g2, tm=512):
    t, d = x.shape
    full = lambda a: pl.BlockSpec(a.shape, lambda i: (0,) * a.ndim)
    row = pl.BlockSpec((tm, d), lambda i: (i, 0))
    args = (mu, w_rkv, w1, w2, w0, a1, a2, a0, g1, g2)
    return pl.pallas_call(
        _rwkv_proj_kernel,
        grid=(t // tm,),
        in_specs=[row, _prev_rows_spec(tm, d)] + [full(a) for a in args],
        out_specs=[row] * 6,
        out_shape=[jax.ShapeDtypeStruct((t, d), F32)] * 6,
        compiler_params=_params(("parallel",)),
        name="rwkv_proj",
    )(x, x, *args)


def _rwkv_rec_kernel(r_ref, k_ref, v_ref, lw_ref, a_ref, g_ref, kk_ref, ka_ref, rk_ref, lg_ref, lb_ref,
                     o_ref, s_ref):
    c = pl.program_id(0)

    @pl.when(c == 0)
    def _():
        s_ref[...] = jnp.zeros_like(s_ref)

    n = RW_CHUNK
    m = 2 * n
    pairs = range(r_ref.shape[1] // LANES)
    cs_all = _dot_f32(_tri(n), lw_ref[...])
    lane = lax.broadcasted_iota(jnp.int32, (n, LANES), 1)
    first = lane < RW_HEAD
    r2 = lax.broadcasted_iota(jnp.int32, (m, LANES), 0)
    c2 = lax.broadcasted_iota(jnp.int32, (m, LANES), 1)
    own = (r2 < n) == (c2 < RW_HEAD)
    eye = (c2 == r2).astype(F32)
    r4 = lax.broadcasted_iota(jnp.int32, (2 * m, 2 * m), 0)
    c4 = lax.broadcasted_iota(jnp.int32, (2 * m, 2 * m), 1)
    rr = jnp.where(r4 < m, r4, r4 - m)
    cc = jnp.where(c4 < m, c4, c4 - m)
    keep = cc < rr + jnp.where(r4 < m, 0, 1)

    def stack(x):
        return jnp.where(own, jnp.concatenate([x, x], axis=0), 0.0)

    def head_sum(x):
        sa = jnp.sum(jnp.where(first, x, 0.0), axis=1, keepdims=True)
        sb = jnp.sum(jnp.where(first, 0.0, x), axis=1, keepdims=True)
        return jnp.where(first, sa, sb)

    ar, bk, bkh, vs, g_chunk, bonus = [], [], [], [], [], []
    for p in pairs:
        sl = slice(LANES * p, LANES * (p + 1))
        r = r_ref[:, sl]
        k = k_ref[:, sl]
        v = v_ref[:, sl]
        lw = lw_ref[:, sl]
        a = a_ref[:, sl]
        cs = cs_all[:, sl]
        cs_end = cs[n - 1:n, :]
        kk = k * kk_ref[:, sl]
        kk = kk / jnp.maximum(jnp.sqrt(head_sum(kk * kk)), 1e-12)
        kh = k * (1.0 + (a - 1.0) * ka_ref[:, sl])
        kka = kk * a
        g_inv = jnp.exp(-cs)
        g_end = jnp.exp(cs_end - cs)
        ar.append(jnp.concatenate([stack(-kk * jnp.exp(cs - lw)), stack(r * jnp.exp(cs))], axis=0).astype(BF16))
        bk.append(jnp.concatenate([stack(kka * g_inv), stack(kh * g_inv)], axis=0).astype(BF16))
        bkh.append(jnp.concatenate([stack(kka * g_end), stack(kh * g_end)], axis=0).astype(BF16))
        vs.append(stack(v))
        g_chunk.append(jnp.exp(cs_end))
        bonus.append(head_sum(r * kh * rk_ref[:, sl]) * v)

    gm = [jnp.where(keep, _dot_nt(ar[p], bk[p]), 0.0) for p in pairs]
    nm = [gm[p][:m, :m] for p in pairs]
    akv = [_dot(gm[p][:m, m:], vs[p]) for p in pairs]
    arbk = [gm[p][m:, :].astype(BF16) for p in pairs]

    tm = [eye + nm[p] for p in pairs]
    pw = [_dot(nm[p], nm[p]) for p in pairs]
    for _ in range(4):
        both = [_dot(jnp.concatenate([tm[p], pw[p]], axis=0), pw[p]) for p in pairs]
        tm = [tm[p] + both[p][:m] for p in pairs]
        pw = [both[p][m:] for p in pairs]
    tm = [tm[p] + _dot(tm[p], pw[p]) for p in pairs]

    ars = [_dot_nt(ar[p], s_ref[p]) for p in pairs]
    u = [_dot(tm[p], ars[p][:m] + akv[p]) for p in pairs]
    uv = [jnp.concatenate([u[p], vs[p]], axis=0).astype(BF16) for p in pairs]
    y = [ars[p][m:] + _dot(arbk[p], uv[p]) for p in pairs]
    for p in pairs:
        s_ref[p] = s_ref[p] * g_chunk[p] + _dot_tn(uv[p], bkh[p])

    for p in pairs:
        sl = slice(LANES * p, LANES * (p + 1))
        mu = jnp.sum(y[p], axis=1, keepdims=True) * (1.0 / RW_HEAD)
        yc = jnp.where(own, y[p] - mu, 0.0)
        var = jnp.sum(yc * yc, axis=1, keepdims=True) * (1.0 / RW_HEAD)
        yn = yc * lax.rsqrt(var + RW_GN_EPS)
        yn = yn[:n] + yn[n:]
        o_ref[:, sl] = (yn * lg_ref[:, sl] + lb_ref[:, sl] + bonus[p]) * g_ref[:, sl]


def _rwkv_rec(r, k, v, lw, a, g, k_k, k_a, r_k, lnx_g, lnx_b):
    t, d = r.shape
    row = pl.BlockSpec((RW_CHUNK, d), lambda c: (c, 0))
    vec = pl.BlockSpec((1, d), lambda c: (0, 0))
    return pl.pallas_call(
        _rwkv_rec_kernel,
        grid=(t // RW_CHUNK,),
        in_specs=[row] * 6 + [vec] * 5,
        out_specs=row,
        out_shape=jax.ShapeDtypeStruct((t, d), F32),
        scratch_shapes=[pltpu.VMEM((d // LANES, LANES, LANES), F32)],
        compiler_params=_params(("arbitrary",)),
        name="rwkv_rec",
    )(r, k, v, lw, a, g, k_k, k_a, r_k, lnx_g, lnx_b)


def _rwkv7_mix(h, ln_g, ln_b, mu, w_rkv, w0, w1, w2, a0, a1, a2, g1, g2, k_k, k_a, r_k, lnx_g, lnx_b, w_o,
               wr, br):
    t, d = h.shape
    lora = w1.shape[1]
    lp = -(-lora // LANES) * LANES
    gl = g1.shape[1]
    gp = -(-gl // LANES) * LANES
    r, k, v, lw, a, g = _rwkv_proj(
        h, _pad_rows(mu, SUBLANES), w_rkv.astype(BF16),
        _pad_cols(w1, lp).astype(BF16), _pad_rows(w2, lp).astype(BF16), _row(w0),
        _pad_cols(a1, lp).astype(BF16), _pad_rows(a2, lp).astype(BF16), _row(a0),
        _pad_cols(g1, gp).astype(BF16), _pad_rows(g2, gp).astype(BF16))
    y = _rwkv_rec(r, k, v, lw, a, g, _row(k_k), _row(k_a), _row(r_k), _row(lnx_g), _row(lnx_b))
    return _mm_res_ln(y, w_o.astype(BF16), h, _row(ln_g), _row(ln_b), wr, br)


def _mb_conv_kernel(x_ref, xp_ref, w_ref, b_ref, xs_ref, bm_ref, cm_ref, buf_ref):
    i = pl.program_id(0)
    tm = x_ref.shape[0]
    x = x_ref[...]
    buf_ref[0:SUBLANES, :] = jnp.where(i > 0, xp_ref[...], 0.0)
    buf_ref[SUBLANES:, :] = x
    acc = b_ref[...] + x * w_ref[MB_CONV - 1:MB_CONV, :]
    for s in range(1, MB_CONV):
        acc = acc + buf_ref[pl.ds(SUBLANES - s, tm), :] * w_ref[MB_CONV - 1 - s:MB_CONV - s, :]
    y = _silu(acc)
    di = xs_ref.shape[1]
    gn = bm_ref.shape[1]
    xs_ref[...] = y[:, :di]
    bm_ref[...] = y[:, di:di + gn]
    cm_ref[...] = y[:, di + gn:]


def _mb_conv(xbc, conv_w, conv_b, di, gn, tm=256):
    t, cd = xbc.shape
    return pl.pallas_call(
        _mb_conv_kernel,
        grid=(t // tm,),
        in_specs=[pl.BlockSpec((tm, cd), lambda i: (i, 0)),
                  _prev_rows_spec(tm, cd),
                  pl.BlockSpec((SUBLANES, cd), lambda i: (0, 0)),
                  pl.BlockSpec((1, cd), lambda i: (0, 0))],
        out_specs=[pl.BlockSpec((tm, di), lambda i: (i, 0)),
                   pl.BlockSpec((tm, gn), lambda i: (i, 0)),
                   pl.BlockSpec((tm, gn), lambda i: (i, 0))],
        out_shape=[jax.ShapeDtypeStruct((t, di), F32),
                   jax.ShapeDtypeStruct((t, gn), F32),
                   jax.ShapeDtypeStruct((t, gn), F32)],
        scratch_shapes=[pltpu.VMEM((tm + SUBLANES, cd), F32)],
        compiler_params=_params(("parallel",)),
        name="mb_conv",
    )(xbc, xbc, _pad_rows(conv_w, SUBLANES), _row(conv_b))


def _mb_ssd_kernel(xs_ref, bm_ref, cm_ref, dt_ref, z_ref, dtb_ref, alog_ref, dsk_ref, ng_ref, o_ref, s_ref,
                   *, chunks):
    c = pl.program_id(0)

    @pl.when(c == 0)
    def _():
        s_ref[...] = jnp.zeros_like(s_ref)

    n = MB_CHUNK
    p_dim = MB_HEADDIM
    groups = s_ref.shape[0]
    hpg = s_ref.shape[1] // p_dim
    gw = hpg * p_dim
    grange = range(groups)
    heads = range(groups * hpg)
    r2 = lax.broadcasted_iota(jnp.int32, (n, n), 0)
    c2 = lax.broadcasted_iota(jnp.int32, (n, n), 1)
    incl = c2 <= r2
    tri = _tri(n)
    a_neg = -jnp.exp(alog_ref[...])

    parts = []
    for j in range(chunks):
        rs = slice(n * j, n * (j + 1))
        dt = _softplus(dt_ref[rs, :] + dtb_ref[...])
        cs = _dot_f32(tri, dt * a_neg)
        cs_t = cs.T
        cs_end = cs[n - 1:n, :]
        to_end = jnp.exp(cs_end - cs)
        from_start = jnp.exp(cs)
        chunk_decay = jnp.exp(cs_end)
        bgs = [bm_ref[rs, MB_STATE * g:MB_STATE * (g + 1)].astype(BF16) for g in grange]
        cgs = [cm_ref[rs, MB_STATE * g:MB_STATE * (g + 1)].astype(BF16) for g in grange]
        scores = [_dot_nt(cgs[g], bgs[g]) for g in grange]
        x_hs = [xs_ref[rs, p_dim * hd:p_dim * (hd + 1)] for hd in heads]
        xdt = [x_hs[hd] * dt[:, hd:hd + 1] for hd in heads]
        y_diag = []
        for hd in heads:
            seg = jnp.minimum(cs[:, hd:hd + 1] - cs_t[hd:hd + 1, :], 0.0)
            y_diag.append(_dot(scores[hd // hpg] * jnp.where(incl, jnp.exp(seg), 0.0), xdt[hd])
                          + x_hs[hd] * dsk_ref[:, p_dim * hd:p_dim * (hd + 1)])
        x_end, decay = [], []
        for g in grange:
            hs = range(g * hpg, (g + 1) * hpg)
            x_end.append(jnp.concatenate([xdt[hd] * to_end[:, hd:hd + 1] for hd in hs], axis=1).astype(BF16))
            decay.append(jnp.concatenate(
                [jnp.broadcast_to(chunk_decay[:, hd:hd + 1], (p_dim, 1)) for hd in hs], axis=0))
        parts.append((bgs, cgs, y_diag, x_end, decay, from_start))

    for j, (bgs, cgs, y_diag, x_end, decay, from_start) in enumerate(parts):
        rs = slice(n * j, n * (j + 1))
        y_off = [_dot_nt(cgs[g], s_ref[g]) for g in grange]
        for g in grange:
            s_ref[g] = s_ref[g] * decay[g] + _dot_tn(x_end[g], bgs[g])
        for g in grange:
            y_g = jnp.concatenate(
                [y_diag[hd] + y_off[g][:, p_dim * (hd - g * hpg):p_dim * (hd - g * hpg + 1)] * from_start[:, hd:hd + 1]
                 for hd in range(g * hpg, (g + 1) * hpg)], axis=1)
            y_g = y_g * _silu(z_ref[rs, gw * g:gw * (g + 1)])
            ms = jnp.mean(y_g * y_g, axis=-1, keepdims=True)
            o_ref[rs, gw * g:gw * (g + 1)] = y_g * lax.rsqrt(ms + MB_NORM_EPS) * ng_ref[:, gw * g:gw * (g + 1)]


def _mb_ssd(xs, bm, cm, dt, z, dt_bias, a_log, d_skip, norm_g, chunks=1):
    t, di = xs.shape
    gn = bm.shape[1]
    groups = gn // MB_STATE
    tm = MB_CHUNK * chunks
    blk = lambda w: pl.BlockSpec((tm, w), lambda c: (c, 0))
    vec = lambda w: pl.BlockSpec((1, w), lambda c: (0, 0))
    return pl.pallas_call(
        functools.partial(_mb_ssd_kernel, chunks=chunks),
        grid=(t // tm,),
        in_specs=[blk(di), blk(gn), blk(gn), blk(LANES), blk(di), vec(LANES), vec(LANES), vec(di), vec(di)],
        out_specs=blk(di),
        out_shape=jax.ShapeDtypeStruct((t, di), F32),
        scratch_shapes=[pltpu.VMEM((groups, di // groups, MB_STATE), F32)],
        compiler_params=_params(("arbitrary",)),
        name="mb_ssd",
    )(xs, bm, cm, dt, z, dt_bias, a_log, d_skip, norm_g)


def _mamba2_mix(h, ln_g, ln_b, w_in, conv_w, conv_b, dt_bias, a_log, d_skip, norm_g, w_out, wr, br):
    t, d = h.shape
    heads = dt_bias.shape[0]
    di = heads * MB_HEADDIM
    gn = MB_GROUPS * MB_STATE
    cd = di + 2 * gn
    w_bf = w_in.astype(BF16)
    z = _mm(h, w_bf[:, :di], name="mb_in_z")
    xbc = _mm(h, w_bf[:, di:di + cd], name="mb_in_xbc")
    dt = _mm(h, _pad_cols(w_bf[:, di + cd:], LANES), name="mb_in_dt")
    xs, bm, cm = _mb_conv(xbc, conv_w, conv_b, di, gn)
    pad_h = lambda v_: _pad_cols(_row(v_), LANES)
    y = _mb_ssd(xs, bm, cm, dt, z, pad_h(dt_bias), pad_h(a_log),
                _row(jnp.repeat(d_skip, MB_HEADDIM)), _row(norm_g))
    return _mm_res_ln(y, w_out.astype(BF16), h, _row(ln_g), _row(ln_b), wr, br)


def _gla_kernel(qk_ref, vg_ref, lr_ref, w2_ref, b2_ref, ng_ref, o_ref, s_ref, *, chunks):
    c = pl.program_id(0)

    @pl.when(c == 0)
    def _():
        s_ref[...] = jnp.zeros_like(s_ref)

    n = GL_CHUNK
    heads, dv, dk = s_ref.shape
    kd = heads * dk
    vd = heads * dv
    tri = _tri(n)
    r2 = lax.broadcasted_iota(jnp.int32, (n, n), 0)
    c2 = lax.broadcasted_iota(jnp.int32, (n, n), 1)
    incl = c2 <= r2
    hrange = range(heads)
    pre = _dot(lr_ref[...], w2_ref[...]) + b2_ref[...]
    log_alpha = (jnp.minimum(pre, 0.0) - jnp.log(1.0 + jnp.exp(-jnp.abs(pre)))) * (1.0 / GL_GATE_NORM)
    q_in, k_end, v_bf, decay, intra = {}, {}, {}, {}, {}
    for j in range(chunks):
        rs = slice(n * j, n * (j + 1))
        b_all = _dot_f32(tri, log_alpha[rs, :])
        for hd in hrange:
            q = qk_ref[rs, dk * hd:dk * (hd + 1)] * (dk ** -0.5)
            k = qk_ref[rs, kd + dk * hd:kd + dk * (hd + 1)]
            b = b_all[:, dk * hd:dk * (hd + 1)]
            b_end = b[n - 1:n, :]
            q_in[j, hd] = (q * jnp.exp(b)).astype(BF16)
            k_end[j, hd] = (k * jnp.exp(b_end - b)).astype(BF16)
            v_bf[j, hd] = vg_ref[rs, dv * hd:dv * (hd + 1)].astype(BF16)
            decay[j, hd] = jnp.exp(b_end)
            attn = jnp.where(incl, _dot_nt(q_in[j, hd], k * jnp.exp(-b)), 0.0)
            intra[j, hd] = _dot(attn, v_bf[j, hd])
    for j in range(chunks):
        rs = slice(n * j, n * (j + 1))
        inter = [_dot_nt(q_in[j, hd], s_ref[hd]) for hd in hrange]
        for hd in hrange:
            s_ref[hd] = s_ref[hd] * decay[j, hd] + _dot_tn(v_bf[j, hd], k_end[j, hd])
        for hd in hrange:
            o = intra[j, hd] + inter[hd]
            o = o * lax.rsqrt(jnp.mean(o * o, axis=-1, keepdims=True) + GL_NORM_EPS) * ng_ref[...]
            o_ref[rs, dv * hd:dv * (hd + 1)] = o * _silu(vg_ref[rs, vd + dv * hd:vd + dv * (hd + 1)])


def _gla(qk, vg, lr, w2, b2, norm_g, chunks=4):
    t = qk.shape[0]
    kd = qk.shape[1] // 2
    vd = vg.shape[1] // 2
    tm = GL_CHUNK * chunks
    blk = lambda w: pl.BlockSpec((tm, w), lambda c: (c, 0))
    full = lambda a: pl.BlockSpec(a.shape, lambda c: (0, 0))
    return pl.pallas_call(
        functools.partial(_gla_kernel, chunks=chunks),
        grid=(t // tm,),
        in_specs=[blk(2 * kd), blk(2 * vd), blk(LANES), full(w2), full(b2), full(norm_g)],
        out_specs=blk(vd),
        out_shape=jax.ShapeDtypeStruct((t, vd), F32),
        scratch_shapes=[pltpu.VMEM((GL_HEADS, vd // GL_HEADS, kd // GL_HEADS), F32)],
        compiler_params=_params(("arbitrary",)),
        name="gla",
    )(qk, vg, lr, w2, b2, norm_g)


def _gla_mix(h, ln_g, ln_b, w_in, gk_w2, gk_b, norm_g, w_out, wr, br):
    kd = gk_w2.shape[1]
    vd = w_out.shape[0]
    w_bf = w_in.astype(BF16)
    qk = _mm(h, w_bf[:, :2 * kd], name="gl_in_qk")
    vg = _mm(h, w_bf[:, 2 * kd:2 * kd + 2 * vd], name="gl_in_vg")
    lr = _mm(h, _pad_cols(w_bf[:, 2 * kd + 2 * vd:], LANES), name="gl_in_lr")
    y = _gla(qk, vg, lr, _pad_rows(gk_w2, LANES).astype(BF16), _row(gk_b), _row(norm_g))
    return _mm_res_ln(y, w_out.astype(BF16), h, _row(ln_g), _row(ln_b), wr, br)


def _moe_ffn_kernel(be_ref, nb_ref, tok0_ref, tokn_ref, slotp_ref, h_ref, win_ref, wdn_ref, y_ref,
                    xbuf, ybuf, gsem, ssem):
    s = pl.program_id(0)
    n_used = nb_ref[0]
    cur = s % 2
    nxt = 1 - cur
    ch = h_ref.shape[1]
    blk = xbuf.shape[1] // ch
    ff = wdn_ref.shape[2]

    def tile(buf, slot, r):
        return buf.at[slot, pl.ds(pl.multiple_of(r * ch, ch), ch)]

    def gather(tok, slot, r):
        return pltpu.make_async_copy(h_ref.at[tok], tile(xbuf, slot, r), gsem.at[slot])

    def scatter(slot, r, row):
        return pltpu.make_async_copy(tile(ybuf, slot, r), y_ref.at[row], ssem)

    @pl.when(s == 0)
    def _():
        ybuf[...] = jnp.zeros_like(ybuf)

        def first(r, carry):
            gather(tok0_ref[0, 0, r], 0, r).start()
            return carry

        lax.fori_loop(0, blk, first, 0, unroll=8)

    @pl.when(s <= n_used)
    def _():
        def issue(i, carry):
            for u in range(MOE_ISSUE):
                r = i * MOE_ISSUE + u
                scatter(nxt, r, slotp_ref[0, 0, r]).start(priority=0)
                gather(tokn_ref[0, 0, r], nxt, r).start(priority=1)
            return carry

        lax.fori_loop(0, blk // MOE_ISSUE, issue, 0)

        for r in range(blk):
            gather(0, cur, r).wait()
        x = _load_row_tiles(xbuf, (cur,), blk, ch)
        hcat = jnp.dot(x.astype(BF16), win_ref[0, 0].astype(BF16), preferred_element_type=F32)
        act = (_silu(hcat[:, :ff]) * hcat[:, ff:]).astype(BF16)
        _store_row_tiles(ybuf, (cur,), jnp.dot(act, wdn_ref[0, 0].astype(BF16), preferred_element_type=F32))

        for r in range(blk):
            scatter(nxt, r, 0).wait()

    @pl.when(s == n_used)
    def _():
        for r in range(blk):
            gather(0, nxt, r).wait()


def _moe_ffn(block_expert, n_used, tok, slot_prev, h_tiles, w_in, w_down, layer, n_out_rows):
    t, ch, _ = h_tiles.shape
    d = ch * LANES
    nb = tok.shape[0]
    ff2 = w_in.shape[3]
    last = nb - 1
    smem_blk = lambda fn: pl.BlockSpec((1, 1, MOE_BLOCK), fn, memory_space=pltpu.SMEM)
    grid_spec = pltpu.PrefetchScalarGridSpec(
        num_scalar_prefetch=2,
        grid=(nb + 1,),
        in_specs=[smem_blk(lambda s, be, nu: (0, 0, 0)),
                  smem_blk(lambda s, be, nu: (jnp.minimum(s + 1, last), 0, 0)),
                  smem_blk(lambda s, be, nu: (s, 0, 0)),
                  pl.BlockSpec(memory_space=pl.ANY),
                  pl.BlockSpec((1, 1, d, ff2), lambda s, be, nu: (layer, be[jnp.minimum(s, last)], 0, 0)),
                  pl.BlockSpec((1, 1, ff2 // 2, d), lambda s, be, nu: (layer, be[jnp.minimum(s, last)], 0, 0))],
        out_specs=pl.BlockSpec(memory_space=pl.ANY),
        scratch_shapes=[pltpu.VMEM((2, MOE_BLOCK * ch, LANES), F32), pltpu.VMEM((2, MOE_BLOCK * ch, LANES), F32),
                        pltpu.SemaphoreType.DMA((2,)), pltpu.SemaphoreType.DMA(())],
    )
    return pl.pallas_call(
        _moe_ffn_kernel,
        grid_spec=grid_spec,
        out_shape=jax.ShapeDtypeStruct((n_out_rows, ch, LANES), F32),
        compiler_params=_params(("arbitrary",)),
        name="moe_ffn",
    )(block_expert, n_used, tok, tok, slot_prev, h_tiles, w_in, w_down)


def _first_argmax(x, rows):
    m = jnp.max(x, axis=0, keepdims=True)
    idx = jnp.min(jnp.where(x == m, rows, x.shape[0]), axis=0, keepdims=True)
    return m, idx


def _softmax_rows(x):
    e = jnp.exp(x - jnp.max(x, axis=0, keepdims=True))
    return e / jnp.sum(e, axis=0, keepdims=True)


def _route_kernel(lg_ref, ids_ref, gt_ref, cnt_ref, carry_ref):
    i = pl.program_id(0)

    @pl.when(i == 0)
    def _():
        carry_ref[...] = jnp.zeros_like(carry_ref)

    tm = lg_ref.shape[1]
    epg = MOE_PER_GROUP
    n_exp = MOE_GROUPS * epg
    rows8 = lax.broadcasted_iota(jnp.int32, (SUBLANES, tm), 0)
    group_probs = _softmax_rows(lg_ref[0:SUBLANES, :])
    p_group, g_idx = _first_argmax(group_probs, rows8)
    sel = jnp.zeros((epg, tm), F32)
    for gi in range(MOE_GROUPS):
        sel = sel + jnp.where(g_idx == gi, lg_ref[SUBLANES + epg * gi:SUBLANES + epg * (gi + 1), :], 0.0)
    p_exp = _softmax_rows(sel)
    p1, i1 = _first_argmax(p_exp, rows8)
    p2, i2 = _first_argmax(jnp.where(rows8 == i1, -1.0, p_exp), rows8)
    denom = p1 + p2
    e0 = g_idx * epg + i1
    e1 = g_idx * epg + i2

    rows_e = lax.broadcasted_iota(jnp.int32, (n_exp, tm), 0)
    oh0 = (rows_e == e0).astype(F32)
    oh1 = (rows_e == e1).astype(F32)
    oh = oh0 + oh1
    tr = lax.broadcasted_iota(jnp.int32, (tm, tm), 0)
    tc = lax.broadcasted_iota(jnp.int32, (tm, tm), 1)
    before = _dot(oh, (tr < tc).astype(F32))
    base = carry_ref[:, 0:1] + before
    rank0 = jnp.sum(oh0 * base, axis=0, keepdims=True)
    rank1 = jnp.sum(oh1 * base, axis=0, keepdims=True)
    carry_ref[...] = carry_ref[...] + jnp.sum(oh, axis=1, keepdims=True)
    cnt_ref[...] = carry_ref[...]

    zi = jnp.zeros((SUBLANES - 4, tm), jnp.int32)
    ids_ref[...] = jnp.concatenate([e0, e1, rank0.astype(jnp.int32), rank1.astype(jnp.int32), zi], axis=0)
    zf = jnp.zeros((SUBLANES - 2, tm), F32)
    gt_ref[...] = jnp.concatenate([p_group * p1 / denom, p_group * p2 / denom, zf], axis=0)


def _route(logits_t, tm=512):
    nr, t = logits_t.shape
    n_exp = MOE_GROUPS * MOE_PER_GROUP
    return pl.pallas_call(
        _route_kernel,
        grid=(t // tm,),
        in_specs=[pl.BlockSpec((nr, tm), lambda i: (0, i))],
        out_specs=[pl.BlockSpec((SUBLANES, tm), lambda i: (0, i)),
                   pl.BlockSpec((SUBLANES, tm), lambda i: (0, i)),
                   pl.BlockSpec((n_exp, LANES), lambda i: (0, 0))],
        out_shape=[jax.ShapeDtypeStruct((SUBLANES, t), jnp.int32),
                   jax.ShapeDtypeStruct((SUBLANES, t), F32),
                   jax.ShapeDtypeStruct((n_exp, LANES), F32)],
        scratch_shapes=[pltpu.VMEM((n_exp, LANES), F32)],
        compiler_params=_params(("arbitrary",)),
        name="moe_route",
    )(logits_t)


def _combine_ln_kernel(y0_ref, y1_ref, gt_ref, res_ref, g_ref, b_ref, o_ref):
    tm, d = res_ref.shape
    ch = d // LANES
    gates = jnp.concatenate([gt_ref[...], jnp.zeros((LANES - SUBLANES, tm), F32)], axis=0).T
    y = (_load_row_tiles(y0_ref, (), tm, ch) * gates[:, 0:1]
         + _load_row_tiles(y1_ref, (), tm, ch) * gates[:, 1:2])
    o_ref[...] = _layer_norm(ALPHA * res_ref[...] + y, g_ref[...], b_ref[...])


def _combine_ln(y_slots, gates, res, g, b, tm=512):
    t, d = res.shape
    nt = t // tm
    ch = d // LANES
    y2d = y_slots.reshape(-1, LANES)
    row = pl.BlockSpec((tm, d), lambda i: (i, 0))
    vec = pl.BlockSpec((1, d), lambda i: (0, 0))
    return pl.pallas_call(
        _combine_ln_kernel,
        grid=(nt,),
        in_specs=[pl.BlockSpec((tm * ch, LANES), lambda i: (i, 0)),
                  pl.BlockSpec((tm * ch, LANES), lambda i: (nt + i, 0)),
                  pl.BlockSpec((SUBLANES, tm), lambda i: (0, i)),
                  row, vec, vec],
        out_specs=row,
        out_shape=jax.ShapeDtypeStruct((t, d), F32),
        compiler_params=_params(("parallel",)),
        name="moe_combine_ln",
    )(y2d, y2d, gates, res, g, b)


def _hier_moe(h, h_tiles, logits_t, ln_g, ln_b, w_in, w_down, layer):
    t, d = h.shape
    n_exp = w_in.shape[1]
    ids, gates, counts = _route(logits_t)

    counts = counts[:, 0].astype(jnp.int32)
    padded = (counts + MOE_BLOCK - 1) // MOE_BLOCK * MOE_BLOCK
    pad_end = jnp.cumsum(padded)
    pad_start = pad_end - padded
    n_assign = t * MOE_TOPK
    n_blocks = (n_assign + n_exp * (MOE_BLOCK - 1) + MOE_BLOCK - 1) // MOE_BLOCK
    n_rows = n_blocks * MOE_BLOCK
    block_first_row = jnp.arange(n_blocks, dtype=jnp.int32) * MOE_BLOCK
    block_expert = jnp.minimum(jnp.sum(pad_end[None, :] <= block_first_row[:, None], axis=1),
                               n_exp - 1).astype(jnp.int32)
    n_used = (pad_end[-1:] // MOE_BLOCK).astype(jnp.int32)
    experts = jnp.arange(n_exp, dtype=jnp.int32)[:, None, None]
    seg_start = jnp.sum(jnp.where(ids[None, :MOE_TOPK] == experts, pad_start[:, None, None], 0), axis=0)
    dest = (seg_start + ids[MOE_TOPK:2 * MOE_TOPK]).reshape(n_assign)

    code = jnp.zeros((n_rows,), jnp.int32).at[dest].set(
        jnp.arange(1, n_assign + 1, dtype=jnp.int32), unique_indices=True)
    spare = n_assign + jnp.arange(n_rows, dtype=jnp.int32) % MOE_BLOCK
    tok = jnp.where(code > 0, (code - 1) % t, 0).reshape(n_blocks, 1, MOE_BLOCK)
    slot = jnp.where(code > 0, code - 1, spare)
    slot_prev = jnp.concatenate([spare[:MOE_BLOCK], slot]).reshape(n_blocks + 1, 1, MOE_BLOCK)

    y_slots = _moe_ffn(block_expert, n_used, tok, slot_prev, h_tiles, w_in, w_down, layer,
                       n_assign + MOE_BLOCK)
    return _combine_ln(y_slots, gates, h, _row(ln_g), _row(ln_b))


def kernel(x, ln_g, ln_b, rw_mu, rw_w_rkv, rw_w0, rw_w1, rw_w2, rw_a0, rw_a1, rw_a2, rw_g1, rw_g2, rw_k_k, rw_k_a, rw_r_k, rw_lnx_g, rw_lnx_b, rw_w_o, mb_w_in, mb_conv_w, mb_conv_b, mb_dt_bias, mb_a_log, mb_d, mb_norm_g, mb_w_out, gl_w_in, gl_gk_w2, gl_gk_b, gl_norm_g, gl_w_out, moe_w_group, moe_b_group, moe_w_route, moe_b_route, moe_w_in, moe_w_down):
    bsz, seq, d = x.shape
    assert bsz == 1
    h = x.reshape(seq, d)
    depth = ln_g.shape[0]
    for i in range(depth):
        kind, j = i % 3, i // 3
        spare = SUBLANES - MOE_GROUPS
        wr = jnp.concatenate([moe_w_group[i].T, jnp.zeros((spare, d), F32), moe_w_route[i].T], axis=0)
        br = jnp.concatenate([moe_b_group[i], jnp.full((spare,), -1e30, F32), moe_b_route[i]]).reshape(-1, 1)
        if kind == 0:
            h, h_tiles, logits =_rwkv7_mix(h, ln_g[i, 0], ln_b[i, 0], rw_mu[j], rw_w_rkv[j], rw_w0[j], rw_w1[j],
                                   rw_w2[j], rw_a0[j], rw_a1[j], rw_a2[j], rw_g1[j], rw_g2[j], rw_k_k[j],
                                   rw_k_a[j], rw_r_k[j], rw_lnx_g[j], rw_lnx_b[j], rw_w_o[j], wr, br)
        elif kind == 1:
            h, h_tiles, logits =_mamba2_mix(h, ln_g[i, 0], ln_b[i, 0], mb_w_in[j], mb_conv_w[j], mb_conv_b[j],
                                    mb_dt_bias[j], mb_a_log[j], mb_d[j], mb_norm_g[j], mb_w_out[j], wr, br)
        else:
            h, h_tiles, logits =_gla_mix(h, ln_g[i, 0], ln_b[i, 0], gl_w_in[j], gl_gk_w2[j], gl_gk_b[j],
                                 gl_norm_g[j], gl_w_out[j], wr, br)
        h = _hier_moe(h, h_tiles, logits, ln_g[i, 1], ln_b[i, 1], moe_w_in, moe_w_down, i)
    return h.reshape(bsz, seq, d)
```

```python
import functools

import jax
import jax.numpy as jnp
from jax import lax
from jax.experimental import pallas as pl
from jax.experimental.pallas import tpu as pltpu

F32 = jnp.float32
BF16 = jnp.bfloat16

LANES = 128
SUBLANES = 8
VMEM_LIMIT = 56 * 1024 * 1024

DEPTH = 4
ALPHA = (2.0 * DEPTH) ** 0.25
LN_EPS = 1e-5

RW_HEAD = 64
RW_CHUNK = 64
RW_GN_EPS = 64e-5

MB_HEADDIM = 64
MB_GROUPS = 4
MB_STATE = 128
MB_CHUNK = 128
MB_CONV = 4
MB_NORM_EPS = 1e-5

GL_HEADS = 4
GL_CHUNK = 64
GL_GATE_NORM = 16.0
GL_NORM_EPS = 1e-5

MOE_GROUPS = 4
MOE_PER_GROUP = 8
MOE_TOPK = 2
MOE_BLOCK = 256
MOE_ISSUE = 64


def _dot(a, b):
    return jnp.dot(a.astype(BF16), b.astype(BF16), preferred_element_type=F32)


def _dot_nt(a, b):
    return lax.dot_general(a.astype(BF16), b.astype(BF16), (((1,), (1,)), ((), ())),
                           preferred_element_type=F32)


def _dot_tn(a, b):
    return lax.dot_general(a.astype(BF16), b.astype(BF16), (((0,), (0,)), ((), ())),
                           preferred_element_type=F32)


def _dot_f32(a, b):
    return jnp.dot(a, b, preferred_element_type=F32, precision=lax.Precision.HIGHEST)


def _sigmoid(x):
    return 1.0 / (1.0 + jnp.exp(-x))


def _softplus(x):
    return jnp.maximum(x, 0.0) + jnp.log(1.0 + jnp.exp(-jnp.abs(x)))


def _silu(x):
    return x * _sigmoid(x)


def _tri(n, dtype=F32):
    r = lax.broadcasted_iota(jnp.int32, (n, n), 0)
    c = lax.broadcasted_iota(jnp.int32, (n, n), 1)
    return (c <= r).astype(dtype)


def _params(sem):
    return pltpu.CompilerParams(dimension_semantics=sem, vmem_limit_bytes=VMEM_LIMIT)


def _pad_cols(w, n):
    return jnp.pad(w, ((0, 0), (0, n - w.shape[1])))


def _pad_rows(w, n):
    return jnp.pad(w, ((0, n - w.shape[0]), (0, 0)))


def _row(v):
    return v.reshape(1, -1).astype(F32)


def _prev_rows_spec(tm, width):
    per = tm // SUBLANES
    return pl.BlockSpec((SUBLANES, width), lambda i: (jnp.maximum(i * per - 1, 0), 0))


def _mm_kernel(x_ref, w_ref, o_ref):
    o_ref[...] = _dot(x_ref[...], w_ref[...])


def _mm(x, w, tm=1024, tn=None, name="mm"):
    t, k = x.shape
    n = w.shape[1]
    tn = tn or min(n, 1024)
    return pl.pallas_call(
        _mm_kernel,
        grid=(t // tm, n // tn),
        in_specs=[pl.BlockSpec((tm, k), lambda i, j: (i, 0)),
                  pl.BlockSpec((k, tn), lambda i, j: (0, j))],
        out_specs=pl.BlockSpec((tm, tn), lambda i, j: (i, j)),
        out_shape=jax.ShapeDtypeStruct((t, n), F32),
        compiler_params=_params(("parallel", "parallel")),
        name=name,
    )(x, w)


def _layer_norm(z, g, b):
    mu = jnp.mean(z, axis=-1, keepdims=True)
    zc = z - mu
    var = jnp.mean(zc * zc, axis=-1, keepdims=True)
    return zc * lax.rsqrt(var + LN_EPS) * g + b


def _store_row_tiles(ref, lead, x):
    rows = x.shape[0]
    chunks = x.shape[1] // LANES
    for c in range(chunks):
        ref[lead + (pl.ds(c, rows, stride=chunks), slice(None))] = x[:, LANES * c:LANES * (c + 1)]


def _load_row_tiles(ref, lead, rows, chunks):
    return jnp.concatenate([ref[lead + (pl.ds(c, rows, stride=chunks), slice(None))] for c in range(chunks)],
                           axis=1)


def _mm_res_ln_kernel(x_ref, w_ref, res_ref, g_ref, b_ref, wr_ref, br_ref, o_ref, ot_ref, lg_ref):
    y = _dot(x_ref[...], w_ref[...])
    h = _layer_norm(ALPHA * res_ref[...] + y, g_ref[...], b_ref[...])
    o_ref[...] = h
    _store_row_tiles(ot_ref, (), h)
    lg_ref[...] = lax.dot_general(wr_ref[...], h, (((1,), (1,)), ((), ())), preferred_element_type=F32,
                                  precision=lax.Precision.HIGHEST) + br_ref[...]


def _mm_res_ln(x, w, res, g, b, wr, br, tm=512):
    t, k = x.shape
    d = w.shape[1]
    nr = wr.shape[0]
    ch = d // LANES
    h, h_tiles, logits_t = pl.pallas_call(
        _mm_res_ln_kernel,
        grid=(t // tm,),
        in_specs=[pl.BlockSpec((tm, k), lambda i: (i, 0)),
                  pl.BlockSpec((k, d), lambda i: (0, 0)),
                  pl.BlockSpec((tm, d), lambda i: (i, 0)),
                  pl.BlockSpec((1, d), lambda i: (0, 0)),
                  pl.BlockSpec((1, d), lambda i: (0, 0)),
                  pl.BlockSpec((nr, d), lambda i: (0, 0)),
                  pl.BlockSpec((nr, 1), lambda i: (0, 0))],
        out_specs=[pl.BlockSpec((tm, d), lambda i: (i, 0)),
                   pl.BlockSpec((tm * ch, LANES), lambda i: (i, 0)),
                   pl.BlockSpec((nr, tm), lambda i: (0, i))],
        out_shape=[jax.ShapeDtypeStruct((t, d), F32), jax.ShapeDtypeStruct((t * ch, LANES), F32),
                   jax.ShapeDtypeStruct((nr, t), F32)],
        compiler_params=_params(("parallel",)),
        name="mm_res_ln",
    )(x, w, res, g, b, wr, br)
    return h, h_tiles.reshape(t, ch, LANES), logits_t


def _rwkv_proj_kernel(x_ref, xp_ref, mu_ref, wrkv_ref, w1_ref, w2_ref, w0_ref, a1_ref, a2_ref, a0_ref,
                      g1_ref, g2_ref, r_ref, k_ref, v_ref, lw_ref, a_ref, g_ref):
    i = pl.program_id(0)
    x = x_ref[...]
    prev = jnp.where(i > 0, xp_ref[SUBLANES - 1:SUBLANES, :], 0.0)
    rows = lax.broadcasted_iota(jnp.int32, x.shape, 0)
    shifted = jnp.where(rows == 0, prev, pltpu.roll(x, 1, 0))
    xx = shifted - x

    def mix(p):
        return x + xx * mu_ref[p:p + 1, :]

    r_ref[...] = _dot(mix(0), wrkv_ref[0])
    k_ref[...] = _dot(mix(1), wrkv_ref[1])
    v_ref[...] = _dot(mix(2), wrkv_ref[2])
    w_pre = w0_ref[...] + _dot(jnp.tanh(_dot(mix(3), w1_ref[...])), w2_ref[...])
    lw_ref[...] = -jnp.exp(-_softplus(-w_pre) - 0.5)
    a_ref[...] = _sigmoid(a0_ref[...] + _dot(_dot(mix(4), a1_ref[...]), a2_ref[...]))
    g_ref[...] = _dot(_sigmoid(_dot(mix(5), g1_ref[...])), g2_ref[...])


def _rwkv_proj(x, mu, w_rkv, w1, w2, w0, a1, a2, a0, g1, g2, tm=512):
    t, d = x.shape
    full = lambda a: pl.BlockSpec(a.shape, lambda i: (0,) * a.ndim)
    row = pl.BlockSpec((tm, d), lambda i: (i, 0))
    args = (mu, w_rkv, w1, w2, w0, a1, a2, a0, g1, g2)
    return pl.pallas_call(
        _rwkv_proj_kernel,
        grid=(t // tm,),
        in_specs=[row, _prev_rows_spec(tm, d)] + [full(a) for a in args],
        out_specs=[row] * 6,
        out_shape=[jax.ShapeDtypeStruct((t, d), F32)] * 6,
        compiler_params=_params(("parallel",)),
        name="rwkv_proj",
    )(x, x, *args)


def _rwkv_rec_kernel(r_ref, k_ref, v_ref, lw_ref, a_ref, g_ref, kk_ref, ka_ref, rk_ref, lg_ref, lb_ref,
                     o_ref, s_ref):
    c = pl.program_id(0)

    @pl.when(c == 0)
    def _():
        s_ref[...] = jnp.zeros_like(s_ref)

    n = RW_CHUNK
    m = 2 * n
    pairs = range(r_ref.shape[1] // LANES)
    cs_all = _dot_f32(_tri(n), lw_ref[...])
    lane = lax.broadcasted_iota(jnp.int32, (n, LANES), 1)
    first = lane < RW_HEAD
    r2 = lax.broadcasted_iota(jnp.int32, (m, LANES), 0)
    c2 = lax.broadcasted_iota(jnp.int32, (m, LANES), 1)
    own = (r2 < n) == (c2 < RW_HEAD)
    eye = (c2 == r2).astype(F32)
    r4 = lax.broadcasted_iota(jnp.int32, (2 * m, 2 * m), 0)
    c4 = lax.broadcasted_iota(jnp.int32, (2 * m, 2 * m), 1)
    rr = jnp.where(r4 < m, r4, r4 - m)
    cc = jnp.where(c4 < m, c4, c4 - m)
    keep = cc < rr + jnp.where(r4 < m, 0, 1)

    def stack(x):
        return jnp.where(own, jnp.concatenate([x, x], axis=0), 0.0)

    def head_sum(x):
        sa = jnp.sum(jnp.where(first, x, 0.0), axis=1, keepdims=True)
        sb = jnp.sum(jnp.where(first, 0.0, x), axis=1, keepdims=True)
        return jnp.where(first, sa, sb)

    ar, bk, bkh, vs, g_chunk, bonus = [], [], [], [], [], []
    for p in pairs:
        sl = slice(LANES * p, LANES * (p + 1))
        r = r_ref[:, sl]
        k = k_ref[:, sl]
        v = v_ref[:, sl]
        lw = lw_ref[:, sl]
        a = a_ref[:, sl]
        cs = cs_all[:, sl]
        cs_end = cs[n - 1:n, :]
        kk = k * kk_ref[:, sl]
        kk = kk / jnp.maximum(jnp.sqrt(head_sum(kk * kk)), 1e-12)
        kh = k * (1.0 + (a - 1.0) * ka_ref[:, sl])
        kka = kk * a
        g_inv = jnp.exp(-cs)
        g_end = jnp.exp(cs_end - cs)
        ar.append(jnp.concatenate([stack(-kk * jnp.exp(cs - lw)), stack(r * jnp.exp(cs))], axis=0).astype(BF16))
        bk.append(jnp.concatenate([stack(kka * g_inv), stack(kh * g_inv)], axis=0).astype(BF16))
        bkh.append(jnp.concatenate([stack(kka * g_end), stack(kh * g_end)], axis=0).astype(BF16))
        vs.append(stack(v))
        g_chunk.append(jnp.exp(cs_end))
        bonus.append(head_sum(r * kh * rk_ref[:, sl]) * v)

    gm = [jnp.where(keep, _dot_nt(ar[p], bk[p]), 0.0) for p in pairs]
    nm = [gm[p][:m, :m] for p in pairs]
    akv = [_dot(gm[p][:m, m:], vs[p]) for p in pairs]
    arbk = [gm[p][m:, :].astype(BF16) for p in pairs]

    tm = [eye + nm[p] for p in pairs]
    pw = [_dot(nm[p], nm[p]) for p in pairs]
    for _ in range(4):
        both = [_dot(jnp.concatenate([tm[p], pw[p]], axis=0), pw[p]) for p in pairs]
        tm = [tm[p] + both[p][:m] for p in pairs]
        pw = [both[p][m:] for p in pairs]
    tm = [tm[p] + _dot(tm[p], pw[p]) for p in pairs]

    ars = [_dot_nt(ar[p], s_ref[p]) for p in pairs]
    u = [_dot(tm[p], ars[p][:m] + akv[p]) for p in pairs]
    uv = [jnp.concatenate([u[p], vs[p]], axis=0).astype(BF16) for p in pairs]
    y = [ars[p][m:] + _dot(arbk[p], uv[p]) for p in pairs]
    for p in pairs:
        s_ref[p] = s_ref[p] * g_chunk[p] + _dot_tn(uv[p], bkh[p])

    for p in pairs:
        sl = slice(LANES * p, LANES * (p + 1))
        mu = jnp.sum(y[p], axis=1, keepdims=True) * (1.0 / RW_HEAD)
        yc = jnp.where(own, y[p] - mu, 0.0)
        var = jnp.sum(yc * yc, axis=1, keepdims=True) * (1.0 / RW_HEAD)
        yn = yc * lax.rsqrt(var + RW_GN_EPS)
        yn = yn[:n] + yn[n:]
        o_ref[:, sl] = (yn * lg_ref[:, sl] + lb_ref[:, sl] + bonus[p]) * g_ref[:, sl]


def _rwkv_rec(r, k, v, lw, a, g, k_k, k_a, r_k, lnx_g, lnx_b):
    t, d = r.shape
    row = pl.BlockSpec((RW_CHUNK, d), lambda c: (c, 0))
    vec = pl.BlockSpec((1, d), lambda c: (0, 0))
    return pl.pallas_call(
        _rwkv_rec_kernel,
        grid=(t // RW_CHUNK,),
        in_specs=[row] * 6 + [vec] * 5,
        out_specs=row,
        out_shape=jax.ShapeDtypeStruct((t, d), F32),
        scratch_shapes=[pltpu.VMEM((d // LANES, LANES, LANES), F32)],
        compiler_params=_params(("arbitrary",)),
        name="rwkv_rec",
    )(r, k, v, lw, a, g, k_k, k_a, r_k, lnx_g, lnx_b)


def _rwkv7_mix(h, ln_g, ln_b, mu, w_rkv, w0, w1, w2, a0, a1, a2, g1, g2, k_k, k_a, r_k, lnx_g, lnx_b, w_o,
               wr, br):
    t, d = h.shape
    lora = w1.shape[1]
    lp = -(-lora // LANES) * LANES
    gl = g1.shape[1]
    gp = -(-gl // LANES) * LANES
    r, k, v, lw, a, g = _rwkv_proj(
        h, _pad_rows(mu, SUBLANES), w_rkv.astype(BF16),
        _pad_cols(w1, lp).astype(BF16), _pad_rows(w2, lp).astype(BF16), _row(w0),
        _pad_cols(a1, lp).astype(BF16), _pad_rows(a2, lp).astype(BF16), _row(a0),
        _pad_cols(g1, gp).astype(BF16), _pad_rows(g2, gp).astype(BF16))
    y = _rwkv_rec(r, k, v, lw, a, g, _row(k_k), _row(k_a), _row(r_k), _row(lnx_g), _row(lnx_b))
    return _mm_res_ln(y, w_o.astype(BF16), h, _row(ln_g), _row(ln_b), wr, br)


def _mm_conv_kernel(x_ref, xp_ref, w_ref, cw_ref, cb_ref, o_ref, buf_ref):
    i = pl.program_id(0)
    tm = x_ref.shape[0]
    w = w_ref[...]
    pre = _dot(x_ref[...], w)
    buf_ref[0:SUBLANES, :] = jnp.where(i > 0, _dot(xp_ref[...], w), 0.0)
    buf_ref[SUBLANES:, :] = pre
    acc = cb_ref[...] + pre * cw_ref[MB_CONV - 1:MB_CONV, :]
    for s in range(1, MB_CONV):
        acc = acc + buf_ref[pl.ds(SUBLANES - s, tm), :] * cw_ref[MB_CONV - 1 - s:MB_CONV - s, :]
    o_ref[...] = _silu(acc)


def _mm_conv(x, w, conv_w, conv_b, tm=1024, tn=1024):
    t, k = x.shape
    n = w.shape[1]
    return pl.pallas_call(
        _mm_conv_kernel,
        grid=(t // tm, n // tn),
        in_specs=[pl.BlockSpec((tm, k), lambda i, j: (i, 0)),
                  pl.BlockSpec((SUBLANES, k), lambda i, j: (jnp.maximum(i * (tm // SUBLANES) - 1, 0), 0)),
                  pl.BlockSpec((k, tn), lambda i, j: (0, j)),
                  pl.BlockSpec((SUBLANES, tn), lambda i, j: (0, j)),
                  pl.BlockSpec((1, tn), lambda i, j: (0, j))],
        out_specs=pl.BlockSpec((tm, tn), lambda i, j: (i, j)),
        out_shape=jax.ShapeDtypeStruct((t, n), F32),
        scratch_shapes=[pltpu.VMEM((tm + SUBLANES, tn), F32)],
        compiler_params=_params(("parallel", "parallel")),
        name="mb_in_conv",
    )(x, x, w, _pad_rows(conv_w, SUBLANES), _row(conv_b))


def _mb_ssd_kernel(xs_ref, bm_ref, cm_ref, dt_ref, z_ref, dtb_ref, alog_ref, dsk_ref, ng_ref, o_ref, s_ref,
                   *, chunks):
    c = pl.program_id(0)

    @pl.when(c == 0)
    def _():
        s_ref[...] = jnp.zeros_like(s_ref)

    n = MB_CHUNK
    p_dim = MB_HEADDIM
    groups = s_ref.shape[0]
    hpg = s_ref.shape[1] // p_dim
    gw = hpg * p_dim
    grange = range(groups)
    heads = range(groups * hpg)
    r2 = lax.broadcasted_iota(jnp.int32, (n, n), 0)
    c2 = lax.broadcasted_iota(jnp.int32, (n, n), 1)
    incl = c2 <= r2
    tri = _tri(n)
    a_neg = -jnp.exp(alog_ref[...])

    def per_head_rows(rows_t, g):
        return jnp.concatenate([jnp.broadcast_to(rows_t[hd:hd + 1, :], (p_dim, n))
                                for hd in range(g * hpg, (g + 1) * hpg)], axis=0)

    parts = []
    for j in range(chunks):
        rs = slice(n * j, n * (j + 1))
        dt = _softplus(dt_ref[rs, :] + dtb_ref[...])
        cs = _dot_f32(tri, dt * a_neg)
        dt_t = dt.T
        cs_t = cs.T
        cs_end_t = cs_t[:, n - 1:n]
        w_end_t = dt_t * jnp.exp(cs_end_t - cs_t)
        from_start_t = jnp.exp(cs_t)
        chunk_decay_t = jnp.exp(cs_end_t)
        x_t = xs_ref[rs, :].T
        bgs = [bm_ref[rs, MB_STATE * g:MB_STATE * (g + 1)].astype(BF16) for g in grange]
        cgs = [cm_ref[rs, MB_STATE * g:MB_STATE * (g + 1)].astype(BF16) for g in grange]
        scores = [_dot_nt(cgs[g], bgs[g]) for g in grange]
        y_diag = []
        for hd in heads:
            seg = jnp.minimum(cs[:, hd:hd + 1] - cs_t[hd:hd + 1, :], 0.0)
            m = scores[hd // hpg] * jnp.where(incl, jnp.exp(seg), 0.0) * dt_t[hd:hd + 1, :]
            y_diag.append(_dot_nt(x_t[p_dim * hd:p_dim * (hd + 1), :], m))
        x_end = [(x_t[gw * g:gw * (g + 1), :] * per_head_rows(w_end_t, g)).astype(BF16) for g in grange]
        decay = [jnp.concatenate([jnp.broadcast_to(chunk_decay_t[hd:hd + 1, :], (p_dim, 1))
                                  for hd in range(g * hpg, (g + 1) * hpg)], axis=0) for g in grange]
        y_fixed = [jnp.concatenate(y_diag[g * hpg:(g + 1) * hpg], axis=0)
                   + x_t[gw * g:gw * (g + 1), :] * dsk_ref[gw * g:gw * (g + 1), :] for g in grange]
        scale = [per_head_rows(from_start_t, g) for g in grange]
        parts.append((bgs, cgs, x_end, decay, y_fixed, scale))

    for j, (bgs, cgs, x_end, decay, y_fixed, scale) in enumerate(parts):
        rs = slice(n * j, n * (j + 1))
        y_off = [_dot_nt(s_ref[g], cgs[g]) for g in grange]
        for g in grange:
            s_ref[g] = s_ref[g] * decay[g] + _dot(x_end[g], bgs[g])
        for g in grange:
            y_g = (y_fixed[g] + y_off[g] * scale[g]).T * _silu(z_ref[rs, gw * g:gw * (g + 1)])
            ms = jnp.mean(y_g * y_g, axis=-1, keepdims=True)
            o_ref[rs, gw * g:gw * (g + 1)] = y_g * lax.rsqrt(ms + MB_NORM_EPS) * ng_ref[:, gw * g:gw * (g + 1)]


def _mb_ssd(xbc, dt, z, dt_bias, a_log, d_skip, norm_g, chunks=1):
    t, di = z.shape
    gn = (xbc.shape[1] - di) // 2
    groups = gn // MB_STATE
    tm = MB_CHUNK * chunks
    blk = lambda w, col=0: pl.BlockSpec((tm, w), lambda c: (c, col))
    vec = lambda w: pl.BlockSpec((1, w), lambda c: (0, 0))
    return pl.pallas_call(
        functools.partial(_mb_ssd_kernel, chunks=chunks),
        grid=(t // tm,),
        in_specs=[blk(di), blk(gn, di // gn), blk(gn, di // gn + 1), blk(LANES), blk(di), vec(LANES), vec(LANES),
                  pl.BlockSpec((di, LANES), lambda c: (0, 0)), vec(di)],
        out_specs=blk(di),
        out_shape=jax.ShapeDtypeStruct((t, di), F32),
        scratch_shapes=[pltpu.VMEM((groups, di // groups, MB_STATE), F32)],
        compiler_params=_params(("arbitrary",)),
        name="mb_ssd",
    )(xbc, xbc, xbc, dt, z, dt_bias, a_log, d_skip, norm_g)


def _mamba2_mix(h, ln_g, ln_b, w_in, conv_w, conv_b, dt_bias, a_log, d_skip, norm_g, w_out, wr, br):
    t, d = h.shape
    heads = dt_bias.shape[0]
    di = heads * MB_HEADDIM
    gn = MB_GROUPS * MB_STATE
    cd = di + 2 * gn
    w_bf = w_in.astype(BF16)
    z = _mm(h, w_bf[:, :di], name="mb_in_z")
    xbc = _mm_conv(h, w_bf[:, di:di + cd], conv_w, conv_b)
    dt = _mm(h, _pad_cols(w_bf[:, di + cd:], LANES), name="mb_in_dt")
    pad_h = lambda v_: _pad_cols(_row(v_), LANES)
    y = _mb_ssd(xbc, dt, z, pad_h(dt_bias), pad_h(a_log),
                jnp.broadcast_to(jnp.repeat(d_skip, MB_HEADDIM)[:, None], (di, LANES)).astype(F32), _row(norm_g))
    return _mm_res_ln(y, w_out.astype(BF16), h, _row(ln_g), _row(ln_b), wr, br)


def _gla_kernel(qk_ref, vg_ref, lr_ref, w2_ref, b2_ref, ng_ref, o_ref, s_ref, *, chunks):
    c = pl.program_id(0)

    @pl.when(c == 0)
    def _():
        s_ref[...] = jnp.zeros_like(s_ref)

    n = GL_CHUNK
    heads, dv, dk = s_ref.shape
    kd = heads * dk
    vd = heads * dv
    tri = _tri(n)
    r2 = lax.broadcasted_iota(jnp.int32, (n, n), 0)
    c2 = lax.broadcasted_iota(jnp.int32, (n, n), 1)
    incl = c2 <= r2
    hrange = range(heads)
    pre = _dot(lr_ref[...], w2_ref[...]) + b2_ref[...]
    log_alpha = (jnp.minimum(pre, 0.0) - jnp.log(1.0 + jnp.exp(-jnp.abs(pre)))) * (1.0 / GL_GATE_NORM)
    q_in, k_end, v_bf, decay, intra = {}, {}, {}, {}, {}
    for j in range(chunks):
        rs = slice(n * j, n * (j + 1))
        b_all = _dot_f32(tri, log_alpha[rs, :])
        for hd in hrange:
            q = qk_ref[rs, dk * hd:dk * (hd + 1)] * (dk ** -0.5)
            k = qk_ref[rs, kd + dk * hd:kd + dk * (hd + 1)]
            b = b_all[:, dk * hd:dk * (hd + 1)]
            b_end = b[n - 1:n, :]
            q_in[j, hd] = (q * jnp.exp(b)).astype(BF16)
            k_end[j, hd] = (k * jnp.exp(b_end - b)).astype(BF16)
            v_bf[j, hd] = vg_ref[rs, dv * hd:dv * (hd + 1)].astype(BF16)
            decay[j, hd] = jnp.exp(b_end)
            attn = jnp.where(incl, _dot_nt(q_in[j, hd], k * jnp.exp(-b)), 0.0)
            intra[j, hd] = _dot(attn, v_bf[j, hd])
    for j in range(chunks):
        rs = slice(n * j, n * (j + 1))
        inter = [_dot_nt(q_in[j, hd], s_ref[hd]) for hd in hrange]
        for hd in hrange:
            s_ref[hd] = s_ref[hd] * decay[j, hd] + _dot_tn(v_bf[j, hd], k_end[j, hd])
        for hd in hrange:
            o = intra[j, hd] + inter[hd]
            o = o * lax.rsqrt(jnp.mean(o * o, axis=-1, keepdims=True) + GL_NORM_EPS) * ng_ref[...]
            o_ref[rs, dv * hd:dv * (hd + 1)] = o * _silu(vg_ref[rs, vd + dv * hd:vd + dv * (hd + 1)])


def _gla(qk, vg, lr, w2, b2, norm_g, chunks=4):
    t = qk.shape[0]
    kd = qk.shape[1] // 2
    vd = vg.shape[1] // 2
    tm = GL_CHUNK * chunks
    blk = lambda w: pl.BlockSpec((tm, w), lambda c: (c, 0))
    full = lambda a: pl.BlockSpec(a.shape, lambda c: (0, 0))
    return pl.pallas_call(
        functools.partial(_gla_kernel, chunks=chunks),
        grid=(t // tm,),
        in_specs=[blk(2 * kd), blk(2 * vd), blk(LANES), full(w2), full(b2), full(norm_g)],
        out_specs=blk(vd),
        out_shape=jax.ShapeDtypeStruct((t, vd), F32),
        scratch_shapes=[pltpu.VMEM((GL_HEADS, vd // GL_HEADS, kd // GL_HEADS), F32)],
        compiler_params=_params(("arbitrary",)),
        name="gla",
    )(qk, vg, lr, w2, b2, norm_g)


def _gla_mix(h, ln_g, ln_b, w_in, gk_w2, gk_b, norm_g, w_out, wr, br):
    kd = gk_w2.shape[1]
    vd = w_out.shape[0]
    w_bf = w_in.astype(BF16)
    qk = _mm(h, w_bf[:, :2 * kd], name="gl_in_qk")
    vg = _mm(h, w_bf[:, 2 * kd:2 * kd + 2 * vd], name="gl_in_vg")
    lr = _mm(h, _pad_cols(w_bf[:, 2 * kd + 2 * vd:], LANES), name="gl_in_lr")
    y = _gla(qk, vg, lr, _pad_rows(gk_w2, LANES).astype(BF16), _row(gk_b), _row(norm_g))
    return _mm_res_ln(y, w_out.astype(BF16), h, _row(ln_g), _row(ln_b), wr, br)


def _moe_ffn_kernel(be_ref, nb_ref, tok0_ref, tokn_ref, slotp_ref, h_ref, win_ref, wdn_ref, y_ref,
                    xbuf, ybuf, gsem, ssem):
    s = pl.program_id(0)
    n_used = nb_ref[0]
    cur = s % 2
    nxt = 1 - cur
    ch = h_ref.shape[1]
    blk = xbuf.shape[1] // ch
    ff = wdn_ref.shape[2]

    def tile(buf, slot, r):
        return buf.at[slot, pl.ds(pl.multiple_of(r * ch, ch), ch)]

    def gather(tok, slot, r):
        return pltpu.make_async_copy(h_ref.at[tok], tile(xbuf, slot, r), gsem.at[slot])

    def scatter(slot, r, row):
        return pltpu.make_async_copy(tile(ybuf, slot, r), y_ref.at[row], ssem)

    @pl.when(s == 0)
    def _():
        ybuf[...] = jnp.zeros_like(ybuf)

        def first(r, carry):
            gather(tok0_ref[0, 0, r], 0, r).start()
            return carry

        lax.fori_loop(0, blk, first, 0, unroll=8)

    @pl.when(s <= n_used)
    def _():
        def issue(i, carry):
            for u in range(MOE_ISSUE):
                r = i * MOE_ISSUE + u
                scatter(nxt, r, slotp_ref[0, 0, r]).start(priority=0)
                gather(tokn_ref[0, 0, r], nxt, r).start(priority=1)
            return carry

        lax.fori_loop(0, blk // MOE_ISSUE, issue, 0)

        for r in range(blk):
            gather(0, cur, r).wait()
        x = _load_row_tiles(xbuf, (cur,), blk, ch)
        hcat = jnp.dot(x.astype(BF16), win_ref[0, 0].astype(BF16), preferred_element_type=F32)
        act = (_silu(hcat[:, :ff]) * hcat[:, ff:]).astype(BF16)
        _store_row_tiles(ybuf, (cur,), jnp.dot(act, wdn_ref[0, 0].astype(BF16), preferred_element_type=F32))

        for r in range(blk):
            scatter(nxt, r, 0).wait()

    @pl.when(s == n_used)
    def _():
        for r in range(blk):
            gather(0, nxt, r).wait()


def _moe_ffn(block_expert, n_used, tok, slot_prev, h_tiles, w_in, w_down, layer, n_out_rows):
    t, ch, _ = h_tiles.shape
    d = ch * LANES
    nb = tok.shape[0]
    ff2 = w_in.shape[3]
    last = nb - 1
    smem_blk = lambda fn: pl.BlockSpec((1, 1, MOE_BLOCK), fn, memory_space=pltpu.SMEM)
    grid_spec = pltpu.PrefetchScalarGridSpec(
        num_scalar_prefetch=2,
        grid=(nb + 1,),
        in_specs=[smem_blk(lambda s, be, nu: (0, 0, 0)),
                  smem_blk(lambda s, be, nu: (jnp.minimum(s + 1, last), 0, 0)),
                  smem_blk(lambda s, be, nu: (s, 0, 0)),
                  pl.BlockSpec(memory_space=pl.ANY),
                  pl.BlockSpec((1, 1, d, ff2), lambda s, be, nu: (layer, be[jnp.minimum(s, last)], 0, 0)),
                  pl.BlockSpec((1, 1, ff2 // 2, d), lambda s, be, nu: (layer, be[jnp.minimum(s, last)], 0, 0))],
        out_specs=pl.BlockSpec(memory_space=pl.ANY),
        scratch_shapes=[pltpu.VMEM((2, MOE_BLOCK * ch, LANES), F32), pltpu.VMEM((2, MOE_BLOCK * ch, LANES), F32),
                        pltpu.SemaphoreType.DMA((2,)), pltpu.SemaphoreType.DMA(())],
    )
    return pl.pallas_call(
        _moe_ffn_kernel,
        grid_spec=grid_spec,
        out_shape=jax.ShapeDtypeStruct((n_out_rows, ch, LANES), F32),
        compiler_params=_params(("arbitrary",)),
        name="moe_ffn",
    )(block_expert, n_used, tok, tok, slot_prev, h_tiles, w_in, w_down)


def _first_argmax(x, rows):
    m = jnp.max(x, axis=0, keepdims=True)
    idx = jnp.min(jnp.where(x == m, rows, x.shape[0]), axis=0, keepdims=True)
    return m, idx


def _softmax_rows(x):
    e = jnp.exp(x - jnp.max(x, axis=0, keepdims=True))
    return e / jnp.sum(e, axis=0, keepdims=True)


def _route_kernel(lg_ref, ids_ref, gt_ref, cnt_ref, carry_ref):
    i = pl.program_id(0)

    @pl.when(i == 0)
    def _():
        carry_ref[...] = jnp.zeros_like(carry_ref)

    tm = lg_ref.shape[1]
    epg = MOE_PER_GROUP
    n_exp = MOE_GROUPS * epg
    rows8 = lax.broadcasted_iota(jnp.int32, (SUBLANES, tm), 0)
    group_probs = _softmax_rows(lg_ref[0:SUBLANES, :])
    p_group, g_idx = _first_argmax(group_probs, rows8)
    sel = jnp.zeros((epg, tm), F32)
    for gi in range(MOE_GROUPS):
        sel = sel + jnp.where(g_idx == gi, lg_ref[SUBLANES + epg * gi:SUBLANES + epg * (gi + 1), :], 0.0)
    p_exp = _softmax_rows(sel)
    p1, i1 = _first_argmax(p_exp, rows8)
    p2, i2 = _first_argmax(jnp.where(rows8 == i1, -1.0, p_exp), rows8)
    denom = p1 + p2
    e0 = g_idx * epg + i1
    e1 = g_idx * epg + i2

    rows_e = lax.broadcasted_iota(jnp.int32, (n_exp, tm), 0)
    oh0 = (rows_e == e0).astype(F32)
    oh1 = (rows_e == e1).astype(F32)
    oh = oh0 + oh1
    tr = lax.broadcasted_iota(jnp.int32, (tm, tm), 0)
    tc = lax.broadcasted_iota(jnp.int32, (tm, tm), 1)
    before = _dot(oh, (tr < tc).astype(F32))
    base = carry_ref[:, 0:1] + before
    rank0 = jnp.sum(oh0 * base, axis=0, keepdims=True)
    rank1 = jnp.sum(oh1 * base, axis=0, keepdims=True)
    carry_ref[...] = carry_ref[...] + jnp.sum(oh, axis=1, keepdims=True)
    cnt_ref[...] = carry_ref[...]

    zi = jnp.zeros((SUBLANES - 4, tm), jnp.int32)
    ids_ref[...] = jnp.concatenate([e0, e1, rank0.astype(jnp.int32), rank1.astype(jnp.int32), zi], axis=0)
    zf = jnp.zeros((SUBLANES - 2, tm), F32)
    gt_ref[...] = jnp.concatenate([p_group * p1 / denom, p_group * p2 / denom, zf], axis=0)


def _route(logits_t, tm=512):
    nr, t = logits_t.shape
    n_exp = MOE_GROUPS * MOE_PER_GROUP
    return pl.pallas_call(
        _route_kernel,
        grid=(t // tm,),
        in_specs=[pl.BlockSpec((nr, tm), lambda i: (0, i))],
        out_specs=[pl.BlockSpec((SUBLANES, tm), lambda i: (0, i)),
                   pl.BlockSpec((SUBLANES, tm), lambda i: (0, i)),
                   pl.BlockSpec((n_exp, LANES), lambda i: (0, 0))],
        out_shape=[jax.ShapeDtypeStruct((SUBLANES, t), jnp.int32),
                   jax.ShapeDtypeStruct((SUBLANES, t), F32),
                   jax.ShapeDtypeStruct((n_exp, LANES), F32)],
        scratch_shapes=[pltpu.VMEM((n_exp, LANES), F32)],
        compiler_params=_params(("arbitrary",)),
        name="moe_route",
    )(logits_t)


def _combine_ln_kernel(y0_ref, y1_ref, gt_ref, res_ref, g_ref, b_ref, o_ref):
    tm, d = res_ref.shape
    ch = d // LANES
    gates = jnp.concatenate([gt_ref[...], jnp.zeros((LANES - SUBLANES, tm), F32)], axis=0).T
    y = (_load_row_tiles(y0_ref, (), tm, ch) * gates[:, 0:1]
         + _load_row_tiles(y1_ref, (), tm, ch) * gates[:, 1:2])
    o_ref[...] = _layer_norm(ALPHA * res_ref[...] + y, g_ref[...], b_ref[...])


def _combine_ln(y_slots, gates, res, g, b, tm=512):
    t, d = res.shape
    nt = t // tm
    ch = d // LANES
    y2d = y_slots.reshape(-1, LANES)
    row = pl.BlockSpec((tm, d), lambda i: (i, 0))
    vec = pl.BlockSpec((1, d), lambda i: (0, 0))
    return pl.pallas_call(
        _combine_ln_kernel,
        grid=(nt,),
        in_specs=[pl.BlockSpec((tm * ch, LANES), lambda i: (i, 0)),
                  pl.BlockSpec((tm * ch, LANES), lambda i: (nt + i, 0)),
                  pl.BlockSpec((SUBLANES, tm), lambda i: (0, i)),
                  row, vec, vec],
        out_specs=row,
        out_shape=jax.ShapeDtypeStruct((t, d), F32),
        compiler_params=_params(("parallel",)),
        name="moe_combine_ln",
    )(y2d, y2d, gates, res, g, b)


def _hier_moe(h, h_tiles, logits_t, ln_g, ln_b, w_in, w_down, layer):
    t, d = h.shape
    n_exp = w_in.shape[1]
    ids, gates, counts = _route(logits_t)

    counts = counts[:, 0].astype(jnp.int32)
    padded = (counts + MOE_BLOCK - 1) // MOE_BLOCK * MOE_BLOCK
    pad_end = jnp.cumsum(padded)
    pad_start = pad_end - padded
    n_assign = t * MOE_TOPK
    n_blocks = (n_assign + n_exp * (MOE_BLOCK - 1) + MOE_BLOCK - 1) // MOE_BLOCK
    n_rows = n_blocks * MOE_BLOCK
    block_first_row = jnp.arange(n_blocks, dtype=jnp.int32) * MOE_BLOCK
    block_expert = jnp.minimum(jnp.sum(pad_end[None, :] <= block_first_row[:, None], axis=1),
                               n_exp - 1).astype(jnp.int32)
    n_used = (pad_end[-1:] // MOE_BLOCK).astype(jnp.int32)
    experts = jnp.arange(n_exp, dtype=jnp.int32)[:, None, None]
    seg_start = jnp.sum(jnp.where(ids[None, :MOE_TOPK] == experts, pad_start[:, None, None], 0), axis=0)
    dest = (seg_start + ids[MOE_TOPK:2 * MOE_TOPK]).reshape(n_assign)

    code = jnp.zeros((n_rows,), jnp.int32).at[dest].set(
        jnp.arange(1, n_assign + 1, dtype=jnp.int32), unique_indices=True)
    spare = n_assign + jnp.arange(n_rows, dtype=jnp.int32) % MOE_BLOCK
    tok = jnp.where(code > 0, (code - 1) % t, 0).reshape(n_blocks, 1, MOE_BLOCK)
    slot = jnp.where(code > 0, code - 1, spare)
    slot_prev = jnp.concatenate([spare[:MOE_BLOCK], slot]).reshape(n_blocks + 1, 1, MOE_BLOCK)

    y_slots = _moe_ffn(block_expert, n_used, tok, slot_prev, h_tiles, w_in, w_down, layer,
                       n_assign + MOE_BLOCK)
    return _combine_ln(y_slots, gates, h, _row(ln_g), _row(ln_b))


def kernel(x, ln_g, ln_b, rw_mu, rw_w_rkv, rw_w0, rw_w1, rw_w2, rw_a0, rw_a1, rw_a2, rw_g1, rw_g2, rw_k_k, rw_k_a, rw_r_k, rw_lnx_g, rw_lnx_b, rw_w_o, mb_w_in, mb_conv_w, mb_conv_b, mb_dt_bias, mb_a_log, mb_d, mb_norm_g, mb_w_out, gl_w_in, gl_gk_w2, gl_gk_b, gl_norm_g, gl_w_out, moe_w_group, moe_b_group, moe_w_route, moe_b_route, moe_w_in, moe_w_down):
    bsz, seq, d = x.shape
    assert bsz == 1
    h = x.reshape(seq, d)
    depth = ln_g.shape[0]
    for i in range(depth):
        kind, j = i % 3, i // 3
        spare = SUBLANES - MOE_GROUPS
        wr = jnp.concatenate([moe_w_group[i].T, jnp.zeros((spare, d), F32), moe_w_route[i].T], axis=0)
        br = jnp.concatenate([moe_b_group[i], jnp.full((spare,), -1e30, F32), moe_b_route[i]]).reshape(-1, 1)
        if kind == 0:
            h, h_tiles, logits =_rwkv7_mix(h, ln_g[i, 0], ln_b[i, 0], rw_mu[j], rw_w_rkv[j], rw_w0[j], rw_w1[j],
                                   rw_w2[j], rw_a0[j], rw_a1[j], rw_a2[j], rw_g1[j], rw_g2[j], rw_k_k[j],
                                   rw_k_a[j], rw_r_k[j], rw_lnx_g[j], rw_lnx_b[j], rw_w_o[j], wr, br)
        elif kind == 1:
            h, h_tiles, logits =_mamba2_mix(h, ln_g[i, 0], ln_b[i, 0], mb_w_in[j], mb_conv_w[j], mb_conv_b[j],
                                    mb_dt_bias[j], mb_a_log[j], mb_d[j], mb_norm_g[j], mb_w_out[j], wr, br)
        else:
            h, h_tiles, logits =_gla_mix(h, ln_g[i, 0], ln_b[i, 0], gl_w_in[j], gl_gk_w2[j], gl_gk_b[j],
                                 gl_norm_g[j], gl_w_out[j], wr, br)
        h = _hier_moe(h, h_tiles, logits, ln_g[i, 1], ln_b[i, 1], moe_w_in, moe_w_down, i)
    return h.reshape(bsz, seq, d)
```

```python
import functools

import jax
import jax.numpy as jnp
from jax import lax
from jax.experimental import pallas as pl
from jax.experimental.pallas import tpu as pltpu

F32 = jnp.float32
BF16 = jnp.bfloat16

LANES = 128
SUBLANES = 8
VMEM_LIMIT = 56 * 1024 * 1024

DEPTH = 4
ALPHA = (2.0 * DEPTH) ** 0.25
LN_EPS = 1e-5

RW_HEAD = 64
RW_CHUNK = 64
RW_GN_EPS = 64e-5

MB_HEADDIM = 64
MB_GROUPS = 4
MB_STATE = 128
MB_CHUNK = 128
MB_CONV = 4
MB_NORM_EPS = 1e-5

GL_HEADS = 4
GL_CHUNK = 64
GL_GATE_NORM = 16.0
GL_NORM_EPS = 1e-5

MOE_GROUPS = 4
MOE_PER_GROUP = 8
MOE_TOPK = 2
MOE_BLOCK = 256
MOE_ISSUE = 64


def _dot(a, b):
    return jnp.dot(a.astype(BF16), b.astype(BF16), preferred_element_type=F32)


def _dot_nt(a, b):
    return lax.dot_general(a.astype(BF16), b.astype(BF16), (((1,), (1,)), ((), ())),
                           preferred_element_type=F32)


def _dot_tn(a, b):
    return lax.dot_general(a.astype(BF16), b.astype(BF16), (((0,), (0,)), ((), ())),
                           preferred_element_type=F32)


def _dot_f32(a, b):
    return jnp.dot(a, b, preferred_element_type=F32, precision=lax.Precision.HIGHEST)


def _sigmoid(x):
    return 1.0 / (1.0 + jnp.exp(-x))


def _softplus(x):
    return jnp.maximum(x, 0.0) + jnp.log(1.0 + jnp.exp(-jnp.abs(x)))


def _silu(x):
    return x * _sigmoid(x)


def _tri(n, dtype=F32):
    r = lax.broadcasted_iota(jnp.int32, (n, n), 0)
    c = lax.broadcasted_iota(jnp.int32, (n, n), 1)
    return (c <= r).astype(dtype)


def _params(sem):
    return pltpu.CompilerParams(dimension_semantics=sem, vmem_limit_bytes=VMEM_LIMIT)


def _pad_cols(w, n):
    return jnp.pad(w, ((0, 0), (0, n - w.shape[1])))


def _pad_rows(w, n):
    return jnp.pad(w, ((0, n - w.shape[0]), (0, 0)))


def _row(v):
    return v.reshape(1, -1).astype(F32)


def _prev_rows_spec(tm, width):
    per = tm // SUBLANES
    return pl.BlockSpec((SUBLANES, width), lambda i: (jnp.maximum(i * per - 1, 0), 0))


def _mm_kernel(x_ref, w_ref, o_ref):
    o_ref[...] = _dot(x_ref[...], w_ref[...])


def _mm(x, w, tm=1024, tn=None, name="mm"):
    t, k = x.shape
    n = w.shape[1]
    tn = tn or min(n, 1024)
    return pl.pallas_call(
        _mm_kernel,
        grid=(t // tm, n // tn),
        in_specs=[pl.BlockSpec((tm, k), lambda i, j: (i, 0)),
                  pl.BlockSpec((k, tn), lambda i, j: (0, j))],
        out_specs=pl.BlockSpec((tm, tn), lambda i, j: (i, j)),
        out_shape=jax.ShapeDtypeStruct((t, n), F32),
        compiler_params=_params(("parallel", "parallel")),
        name=name,
    )(x, w)


def _layer_norm(z, g, b):
    mu = jnp.mean(z, axis=-1, keepdims=True)
    zc = z - mu
    var = jnp.mean(zc * zc, axis=-1, keepdims=True)
    return zc * lax.rsqrt(var + LN_EPS) * g + b


def _store_row_tiles(ref, lead, x):
    rows = x.shape[0]
    chunks = x.shape[1] // LANES
    for c in range(chunks):
        ref[lead + (pl.ds(c, rows, stride=chunks), slice(None))] = x[:, LANES * c:LANES * (c + 1)]


def _load_row_tiles(ref, lead, rows, chunks):
    return jnp.concatenate([ref[lead + (pl.ds(c, rows, stride=chunks), slice(None))] for c in range(chunks)],
                           axis=1)


def _mm_res_ln_kernel(x_ref, w_ref, res_ref, g_ref, b_ref, wr_ref, br_ref, o_ref, ot_ref, lg_ref):
    y = _dot(x_ref[...], w_ref[...])
    h = _layer_norm(ALPHA * res_ref[...] + y, g_ref[...], b_ref[...])
    o_ref[...] = h
    _store_row_tiles(ot_ref, (), h)
    lg_ref[...] = lax.dot_general(wr_ref[...], h, (((1,), (1,)), ((), ())), preferred_element_type=F32,
                                  precision=lax.Precision.HIGHEST) + br_ref[...]


def _mm_res_ln(x, w, res, g, b, wr, br, tm=512):
    t, k = x.shape
    d = w.shape[1]
    nr = wr.shape[0]
    ch = d // LANES
    h, h_tiles, logits_t = pl.pallas_call(
        _mm_res_ln_kernel,
        grid=(t // tm,),
        in_specs=[pl.BlockSpec((tm, k), lambda i: (i, 0)),
                  pl.BlockSpec((k, d), lambda i: (0, 0)),
                  pl.BlockSpec((tm, d), lambda i: (i, 0)),
                  pl.BlockSpec((1, d), lambda i: (0, 0)),
                  pl.BlockSpec((1, d), lambda i: (0, 0)),
                  pl.BlockSpec((nr, d), lambda i: (0, 0)),
                  pl.BlockSpec((nr, 1), lambda i: (0, 0))],
        out_specs=[pl.BlockSpec((tm, d), lambda i: (i, 0)),
                   pl.BlockSpec((tm * ch, LANES), lambda i: (i, 0)),
                   pl.BlockSpec((nr, tm), lambda i: (0, i))],
        out_shape=[jax.ShapeDtypeStruct((t, d), F32), jax.ShapeDtypeStruct((t * ch, LANES), F32),
                   jax.ShapeDtypeStruct((nr, t), F32)],
        compiler_params=_params(("parallel",)),
        name="mm_res_ln",
    )(x, w, res, g, b, wr, br)
    return h, h_tiles.reshape(t, ch, LANES), logits_t


def _rwkv_proj_kernel(x_ref, xp_ref, mu_ref, wrkv_ref, w1_ref, w2_ref, w0_ref, a1_ref, a2_ref, a0_ref,
                      g1_ref, g2_ref, r_ref, k_ref, v_ref, lw_ref, a_ref, g_ref):
    i = pl.program_id(0)
    x = x_ref[...]
    prev = jnp.where(i > 0, xp_ref[SUBLANES - 1:SUBLANES, :], 0.0)
    rows = lax.broadcasted_iota(jnp.int32, x.shape, 0)
    shifted = jnp.where(rows == 0, prev, pltpu.roll(x, 1, 0))
    xx = shifted - x

    def mix(p):
        return x + xx * mu_ref[p:p + 1, :]

    r_ref[...] = _dot(mix(0), wrkv_ref[0])
    k_ref[...] = _dot(mix(1), wrkv_ref[1])
    v_ref[...] = _dot(mix(2), wrkv_ref[2])
    w_pre = w0_ref[...] + _dot(jnp.tanh(_dot(mix(3), w1_ref[...])), w2_ref[...])
    lw_ref[...] = -jnp.exp(-_softplus(-w_pre) - 0.5)
    a_ref[...] = _sigmoid(a0_ref[...] + _dot(_dot(mix(4), a1_ref[...]), a2_ref[...]))
    g_ref[...] = _dot(_sigmoid(_dot(mix(5), g1_ref[...])), g2_ref[...])


def _rwkv_proj(x, mu, w_rkv, w1, w2, w0, a1, a2, a0, g1, g2, tm=512):
    t, d = x.shape
    full = lambda a: pl.BlockSpec(a.shape, lambda i: (0,) * a.ndim)
    row = pl.BlockSpec((tm, d), lambda i: (i, 0))
    args = (mu, w_rkv, w1, w2, w0, a1, a2, a0, g1, g2)
    return pl.pallas_call(
        _rwkv_proj_kernel,
        grid=(t // tm,),
        in_specs=[row, _prev_rows_spec(tm, d)] + [full(a) for a in args],
        out_specs=[row] * 6,
        out_shape=[jax.ShapeDtypeStruct((t, d), F32)] * 6,
        compiler_params=_params(("parallel",)),
        name="rwkv_proj",
    )(x, x, *args)


def _rwkv_rec_kernel(r_ref, k_ref, v_ref, lw_ref, a_ref, g_ref, kk_ref, ka_ref, rk_ref, lg_ref, lb_ref,
                     o_ref, s_ref):
    c = pl.program_id(0)

    @pl.when(c == 0)
    def _():
        s_ref[...] = jnp.zeros_like(s_ref)

    n = RW_CHUNK
    m = 2 * n
    pairs = range(r_ref.shape[1] // LANES)
    cs_all = _dot_f32(_tri(n), lw_ref[...])
    lane = lax.broadcasted_iota(jnp.int32, (n, LANES), 1)
    first = lane < RW_HEAD
    r2 = lax.broadcasted_iota(jnp.int32, (m, LANES), 0)
    c2 = lax.broadcasted_iota(jnp.int32, (m, LANES), 1)
    own = (r2 < n) == (c2 < RW_HEAD)
    eye = (c2 == r2).astype(F32)
    r4 = lax.broadcasted_iota(jnp.int32, (2 * m, 2 * m), 0)
    c4 = lax.broadcasted_iota(jnp.int32, (2 * m, 2 * m), 1)
    rr = jnp.where(r4 < m, r4, r4 - m)
    cc = jnp.where(c4 < m, c4, c4 - m)
    keep = cc < rr + jnp.where(r4 < m, 0, 1)

    def stack(x):
        return jnp.where(own, jnp.concatenate([x, x], axis=0), 0.0)

    def head_sum(x):
        sa = jnp.sum(jnp.where(first, x, 0.0), axis=1, keepdims=True)
        sb = jnp.sum(jnp.where(first, 0.0, x), axis=1, keepdims=True)
        return jnp.where(first, sa, sb)

    ar, bk, bkh, vs, g_chunk, bonus = [], [], [], [], [], []
    for p in pairs:
        sl = slice(LANES * p, LANES * (p + 1))
        r = r_ref[:, sl]
        k = k_ref[:, sl]
        v = v_ref[:, sl]
        lw = lw_ref[:, sl]
        a = a_ref[:, sl]
        cs = cs_all[:, sl]
        cs_end = cs[n - 1:n, :]
        kk = k * kk_ref[:, sl]
        kk = kk / jnp.maximum(jnp.sqrt(head_sum(kk * kk)), 1e-12)
        kh = k * (1.0 + (a - 1.0) * ka_ref[:, sl])
        kka = kk * a
        g_inv = jnp.exp(-cs)
        g_end = jnp.exp(cs_end - cs)
        ar.append(jnp.concatenate([stack(-kk * jnp.exp(cs - lw)), stack(r * jnp.exp(cs))], axis=0).astype(BF16))
        bk.append(jnp.concatenate([stack(kka * g_inv), stack(kh * g_inv)], axis=0).astype(BF16))
        bkh.append(jnp.concatenate([stack(kka * g_end), stack(kh * g_end)], axis=0).astype(BF16))
        vs.append(stack(v))
        g_chunk.append(jnp.exp(cs_end))
        bonus.append(head_sum(r * kh * rk_ref[:, sl]) * v)

    gm = [jnp.where(keep, _dot_nt(ar[p], bk[p]), 0.0) for p in pairs]
    nm = [gm[p][:m, :m] for p in pairs]
    akv = [_dot(gm[p][:m, m:], vs[p]) for p in pairs]
    arbk = [gm[p][m:, :].astype(BF16) for p in pairs]

    tm = [eye + nm[p] for p in pairs]
    pw = [_dot(nm[p], nm[p]) for p in pairs]
    for _ in range(4):
        both = [_dot(jnp.concatenate([tm[p], pw[p]], axis=0), pw[p]) for p in pairs]
        tm = [tm[p] + both[p][:m] for p in pairs]
        pw = [both[p][m:] for p in pairs]
    tm = [tm[p] + _dot(tm[p], pw[p]) for p in pairs]

    ars = [_dot_nt(ar[p], s_ref[p]) for p in pairs]
    u = [_dot(tm[p], ars[p][:m] + akv[p]) for p in pairs]
    uv = [jnp.concatenate([u[p], vs[p]], axis=0).astype(BF16) for p in pairs]
    y = [ars[p][m:] + _dot(arbk[p], uv[p]) for p in pairs]
    for p in pairs:
        s_ref[p] = s_ref[p] * g_chunk[p] + _dot_tn(uv[p], bkh[p])

    for p in pairs:
        sl = slice(LANES * p, LANES * (p + 1))
        mu = jnp.sum(y[p], axis=1, keepdims=True) * (1.0 / RW_HEAD)
        yc = jnp.where(own, y[p] - mu, 0.0)
        var = jnp.sum(yc * yc, axis=1, keepdims=True) * (1.0 / RW_HEAD)
        yn = yc * lax.rsqrt(var + RW_GN_EPS)
        yn = yn[:n] + yn[n:]
        o_ref[:, sl] = (yn * lg_ref[:, sl] + lb_ref[:, sl] + bonus[p]) * g_ref[:, sl]


def _rwkv_rec(r, k, v, lw, a, g, k_k, k_a, r_k, lnx_g, lnx_b):
    t, d = r.shape
    row = pl.BlockSpec((RW_CHUNK, d), lambda c: (c, 0))
    vec = pl.BlockSpec((1, d), lambda c: (0, 0))
    return pl.pallas_call(
        _rwkv_rec_kernel,
        grid=(t // RW_CHUNK,),
        in_specs=[row] * 6 + [vec] * 5,
        out_specs=row,
        out_shape=jax.ShapeDtypeStruct((t, d), F32),
        scratch_shapes=[pltpu.VMEM((d // LANES, LANES, LANES), F32)],
        compiler_params=_params(("arbitrary",)),
        name="rwkv_rec",
    )(r, k, v, lw, a, g, k_k, k_a, r_k, lnx_g, lnx_b)


def _rwkv7_mix(h, ln_g, ln_b, mu, w_rkv, w0, w1, w2, a0, a1, a2, g1, g2, k_k, k_a, r_k, lnx_g, lnx_b, w_o,
               wr, br):
    t, d = h.shape
    lora = w1.shape[1]
    lp = -(-lora // LANES) * LANES
    gl = g1.shape[1]
    gp = -(-gl // LANES) * LANES
    r, k, v, lw, a, g = _rwkv_proj(
        h, _pad_rows(mu, SUBLANES), w_rkv.astype(BF16),
        _pad_cols(w1, lp).astype(BF16), _pad_rows(w2, lp).astype(BF16), _row(w0),
        _pad_cols(a1, lp).astype(BF16), _pad_rows(a2, lp).astype(BF16), _row(a0),
        _pad_cols(g1, gp).astype(BF16), _pad_rows(g2, gp).astype(BF16))
    y = _rwkv_rec(r, k, v, lw, a, g, _row(k_k), _row(k_a), _row(r_k), _row(lnx_g), _row(lnx_b))
    return _mm_res_ln(y, w_o.astype(BF16), h, _row(ln_g), _row(ln_b), wr, br)


def _mm_conv_kernel(x_ref, xp_ref, w_ref, cw_ref, cb_ref, o_ref, buf_ref):
    i = pl.program_id(0)
    tm = x_ref.shape[0]
    w = w_ref[...]
    pre = _dot(x_ref[...], w)
    buf_ref[0:SUBLANES, :] = jnp.where(i > 0, _dot(xp_ref[...], w), 0.0)
    buf_ref[SUBLANES:, :] = pre
    acc = cb_ref[...] + pre * cw_ref[MB_CONV - 1:MB_CONV, :]
    for s in range(1, MB_CONV):
        acc = acc + buf_ref[pl.ds(SUBLANES - s, tm), :] * cw_ref[MB_CONV - 1 - s:MB_CONV - s, :]
    o_ref[...] = _silu(acc)


def _mm_conv(x, w, conv_w, conv_b, tm=1024, tn=1024):
    t, k = x.shape
    n = w.shape[1]
    return pl.pallas_call(
        _mm_conv_kernel,
        grid=(t // tm, n // tn),
        in_specs=[pl.BlockSpec((tm, k), lambda i, j: (i, 0)),
                  pl.BlockSpec((SUBLANES, k), lambda i, j: (jnp.maximum(i * (tm // SUBLANES) - 1, 0), 0)),
                  pl.BlockSpec((k, tn), lambda i, j: (0, j)),
                  pl.BlockSpec((SUBLANES, tn), lambda i, j: (0, j)),
                  pl.BlockSpec((1, tn), lambda i, j: (0, j))],
        out_specs=pl.BlockSpec((tm, tn), lambda i, j: (i, j)),
        out_shape=jax.ShapeDtypeStruct((t, n), F32),
        scratch_shapes=[pltpu.VMEM((tm + SUBLANES, tn), F32)],
        compiler_params=_params(("parallel", "parallel")),
        name="mb_in_conv",
    )(x, x, w, _pad_rows(conv_w, SUBLANES), _row(conv_b))


def _mb_ssd_kernel(xs_ref, bm_ref, cm_ref, dt_ref, z_ref, dtb_ref, alog_ref, dsk_ref, ng_ref, o_ref, s_ref,
                   *, chunks):
    c = pl.program_id(0)

    @pl.when(c == 0)
    def _():
        s_ref[...] = jnp.zeros_like(s_ref)

    n = MB_CHUNK
    p_dim = MB_HEADDIM
    groups = s_ref.shape[0]
    hpg = s_ref.shape[1] // p_dim
    gw = hpg * p_dim
    grange = range(groups)
    heads = range(groups * hpg)
    r2 = lax.broadcasted_iota(jnp.int32, (n, n), 0)
    c2 = lax.broadcasted_iota(jnp.int32, (n, n), 1)
    incl = c2 <= r2
    tri = _tri(n)
    a_neg = -jnp.exp(alog_ref[...])

    def per_head_rows(rows_t, g):
        return jnp.concatenate([jnp.broadcast_to(rows_t[hd:hd + 1, :], (p_dim, n))
                                for hd in range(g * hpg, (g + 1) * hpg)], axis=0)

    parts = []
    for j in range(chunks):
        rs = slice(n * j, n * (j + 1))
        dt = _softplus(dt_ref[rs, :] + dtb_ref[...])
        cs = _dot_f32(tri, dt * a_neg)
        dt_t = dt.T
        cs_t = cs.T
        cs_end_t = cs_t[:, n - 1:n]
        w_end_t = dt_t * jnp.exp(cs_end_t - cs_t)
        from_start_t = jnp.exp(cs_t)
        chunk_decay_t = jnp.exp(cs_end_t)
        x_t = xs_ref[rs, :].T
        bgs = [bm_ref[rs, MB_STATE * g:MB_STATE * (g + 1)].astype(BF16) for g in grange]
        cgs = [cm_ref[rs, MB_STATE * g:MB_STATE * (g + 1)].astype(BF16) for g in grange]
        scores = [_dot_nt(cgs[g], bgs[g]) for g in grange]
        y_diag = []
        for hd in heads:
            seg = jnp.minimum(cs[:, hd:hd + 1] - cs_t[hd:hd + 1, :], 0.0)
            m = scores[hd // hpg] * jnp.where(incl, jnp.exp(seg), 0.0) * dt_t[hd:hd + 1, :]
            y_diag.append(_dot_nt(x_t[p_dim * hd:p_dim * (hd + 1), :], m))
        x_end = [(x_t[gw * g:gw * (g + 1), :] * per_head_rows(w_end_t, g)).astype(BF16) for g in grange]
        decay = [jnp.concatenate([jnp.broadcast_to(chunk_decay_t[hd:hd + 1, :], (p_dim, 1))
                                  for hd in range(g * hpg, (g + 1) * hpg)], axis=0) for g in grange]
        y_fixed = [jnp.concatenate(y_diag[g * hpg:(g + 1) * hpg], axis=0)
                   + x_t[gw * g:gw * (g + 1), :] * dsk_ref[gw * g:gw * (g + 1), :] for g in grange]
        scale = [per_head_rows(from_start_t, g) for g in grange]
        parts.append((bgs, cgs, x_end, decay, y_fixed, scale))

    for j, (bgs, cgs, x_end, decay, y_fixed, scale) in enumerate(parts):
        rs = slice(n * j, n * (j + 1))
        y_off = [_dot_nt(s_ref[g], cgs[g]) for g in grange]
        for g in grange:
            s_ref[g] = s_ref[g] * decay[g] + _dot(x_end[g], bgs[g])
        for g in grange:
            y_g = (y_fixed[g] + y_off[g] * scale[g]).T * _silu(z_ref[rs, gw * g:gw * (g + 1)])
            ms = jnp.mean(y_g * y_g, axis=-1, keepdims=True)
            o_ref[rs, gw * g:gw * (g + 1)] = y_g * lax.rsqrt(ms + MB_NORM_EPS) * ng_ref[:, gw * g:gw * (g + 1)]


def _mb_ssd(xbc, dt, z, dt_bias, a_log, d_skip, norm_g, chunks=1):
    t, di = z.shape
    gn = (xbc.shape[1] - di) // 2
    groups = gn // MB_STATE
    tm = MB_CHUNK * chunks
    blk = lambda w, col=0: pl.BlockSpec((tm, w), lambda c: (c, col))
    vec = lambda w: pl.BlockSpec((1, w), lambda c: (0, 0))
    return pl.pallas_call(
        functools.partial(_mb_ssd_kernel, chunks=chunks),
        grid=(t // tm,),
        in_specs=[blk(di), blk(gn, di // gn), blk(gn, di // gn + 1), blk(LANES), blk(di), vec(LANES), vec(LANES),
                  pl.BlockSpec((di, LANES), lambda c: (0, 0)), vec(di)],
        out_specs=blk(di),
        out_shape=jax.ShapeDtypeStruct((t, di), F32),
        scratch_shapes=[pltpu.VMEM((groups, di // groups, MB_STATE), F32)],
        compiler_params=_params(("arbitrary",)),
        name="mb_ssd",
    )(xbc, xbc, xbc, dt, z, dt_bias, a_log, d_skip, norm_g)


def _mamba2_mix(h, ln_g, ln_b, w_in, conv_w, conv_b, dt_bias, a_log, d_skip, norm_g, w_out, wr, br):
    t, d = h.shape
    heads = dt_bias.shape[0]
    di = heads * MB_HEADDIM
    gn = MB_GROUPS * MB_STATE
    cd = di + 2 * gn
    w_bf = w_in.astype(BF16)
    z = _mm(h, w_bf[:, :di], name="mb_in_z")
    xbc = _mm_conv(h, w_bf[:, di:di + cd], conv_w, conv_b)
    dt = _mm(h, _pad_cols(w_bf[:, di + cd:], LANES), name="mb_in_dt")
    pad_h = lambda v_: _pad_cols(_row(v_), LANES)
    y = _mb_ssd(xbc, dt, z, pad_h(dt_bias), pad_h(a_log),
                jnp.broadcast_to(jnp.repeat(d_skip, MB_HEADDIM)[:, None], (di, LANES)).astype(F32), _row(norm_g))
    return _mm_res_ln(y, w_out.astype(BF16), h, _row(ln_g), _row(ln_b), wr, br)


def _gla_kernel(qk_ref, vg_ref, lr_ref, w2_ref, b2_ref, ng_ref, o_ref, s_ref, *, chunks):
    c = pl.program_id(0)

    @pl.when(c == 0)
    def _():
        s_ref[...] = jnp.zeros_like(s_ref)

    n = GL_CHUNK
    heads, dv, dk = s_ref.shape
    kd = heads * dk
    vd = heads * dv
    tri = _tri(n)
    r2 = lax.broadcasted_iota(jnp.int32, (n, n), 0)
    c2 = lax.broadcasted_iota(jnp.int32, (n, n), 1)
    incl = c2 <= r2
    hrange = range(heads)
    pre = _dot(lr_ref[...], w2_ref[...]) + b2_ref[...]
    log_alpha = (jnp.minimum(pre, 0.0) - jnp.log(1.0 + jnp.exp(-jnp.abs(pre)))) * (1.0 / GL_GATE_NORM)
    q_in, k_end, v_bf, decay, intra = {}, {}, {}, {}, {}
    for j in range(chunks):
        rs = slice(n * j, n * (j + 1))
        b_all = _dot_f32(tri, log_alpha[rs, :])
        for hd in hrange:
            q = qk_ref[rs, dk * hd:dk * (hd + 1)] * (dk ** -0.5)
            k = qk_ref[rs, kd + dk * hd:kd + dk * (hd + 1)]
            b = b_all[:, dk * hd:dk * (hd + 1)]
            b_end = b[n - 1:n, :]
            q_in[j, hd] = (q * jnp.exp(b)).astype(BF16)
            k_end[j, hd] = (k * jnp.exp(b_end - b)).astype(BF16)
            v_bf[j, hd] = vg_ref[rs, dv * hd:dv * (hd + 1)].astype(BF16)
            decay[j, hd] = jnp.exp(b_end)
            attn = jnp.where(incl, _dot_nt(q_in[j, hd], k * jnp.exp(-b)), 0.0)
            intra[j, hd] = _dot(attn, v_bf[j, hd])
    for j in range(chunks):
        rs = slice(n * j, n * (j + 1))
        inter = [_dot_nt(q_in[j, hd], s_ref[hd]) for hd in hrange]
        for hd in hrange:
            s_ref[hd] = s_ref[hd] * decay[j, hd] + _dot_tn(v_bf[j, hd], k_end[j, hd])
        for hd in hrange:
            o = intra[j, hd] + inter[hd]
            o = o * lax.rsqrt(jnp.mean(o * o, axis=-1, keepdims=True) + GL_NORM_EPS) * ng_ref[...]
            o_ref[rs, dv * hd:dv * (hd + 1)] = o * _silu(vg_ref[rs, vd + dv * hd:vd + dv * (hd + 1)])


def _gla(qk, vg, lr, w2, b2, norm_g, chunks=4):
    t = qk.shape[0]
    kd = qk.shape[1] // 2
    vd = vg.shape[1] // 2
    tm = GL_CHUNK * chunks
    blk = lambda w: pl.BlockSpec((tm, w), lambda c: (c, 0))
    full = lambda a: pl.BlockSpec(a.shape, lambda c: (0, 0))
    return pl.pallas_call(
        functools.partial(_gla_kernel, chunks=chunks),
        grid=(t // tm,),
        in_specs=[blk(2 * kd), blk(2 * vd), blk(LANES), full(w2), full(b2), full(norm_g)],
        out_specs=blk(vd),
        out_shape=jax.ShapeDtypeStruct((t, vd), F32),
        scratch_shapes=[pltpu.VMEM((GL_HEADS, vd // GL_HEADS, kd // GL_HEADS), F32)],
        compiler_params=_params(("arbitrary",)),
        name="gla",
    )(qk, vg, lr, w2, b2, norm_g)


def _gla_mix(h, ln_g, ln_b, w_in, gk_w2, gk_b, norm_g, w_out, wr, br):
    kd = gk_w2.shape[1]
    vd = w_out.shape[0]
    w_bf = w_in.astype(BF16)
    qk = _mm(h, w_bf[:, :2 * kd], name="gl_in_qk")
    vg = _mm(h, w_bf[:, 2 * kd:2 * kd + 2 * vd], name="gl_in_vg")
    lr = _mm(h, _pad_cols(w_bf[:, 2 * kd + 2 * vd:], LANES), name="gl_in_lr")
    y = _gla(qk, vg, lr, _pad_rows(gk_w2, LANES).astype(BF16), _row(gk_b), _row(norm_g))
    return _mm_res_ln(y, w_out.astype(BF16), h, _row(ln_g), _row(ln_b), wr, br)


def _moe_dispatch_kernel(pe_ref, pd_ref, idx_ref, h_ref, x_ref, zero_ref, sem, zsem):
    i = pl.program_id(0)
    blk, ch, _ = zero_ref.shape
    tm = h_ref.shape[0] // ch
    n_exp = pe_ref.shape[0]

    def fill(e):
        return pltpu.make_async_copy(zero_ref, x_ref.at[pl.ds(pe_ref[e] - blk, blk)], zsem)

    def fill_block(b):
        return pltpu.make_async_copy(zero_ref, x_ref.at[pl.ds(b * blk, blk)], zsem)

    @pl.when(i == 0)
    def _():
        zero_ref[...] = jnp.zeros_like(zero_ref)
        first_unused = pe_ref[n_exp - 1] // blk
        n_blocks = x_ref.shape[0] // blk
        for e in range(n_exp):
            @pl.when(pd_ref[e] > 0)
            def _(e=e):
                fill(e).start()
        lax.fori_loop(first_unused, n_blocks, lambda b, c: (fill_block(b).start(), c)[1], 0)
        for e in range(n_exp):
            @pl.when(pd_ref[e] > 0)
            def _(e=e):
                fill(e).wait()
        lax.fori_loop(first_unused, n_blocks, lambda b, c: (fill_block(b).wait(), c)[1], 0)

    def row(t, dst):
        return pltpu.make_async_copy(h_ref.at[pl.ds(pl.multiple_of(t * ch, ch), ch)], x_ref.at[dst], sem)

    def issue(j, carry):
        for u in range(MOE_ISSUE):
            t = j * MOE_ISSUE + u
            row(t, idx_ref[0, 0, t]).start()
            row(t, idx_ref[0, 0, tm + t]).start()
        return carry

    lax.fori_loop(0, tm // MOE_ISSUE, issue, 0)
    for _ in range(MOE_TOPK * tm):
        row(0, 0).wait()


def _moe_dispatch(pad_end, padded, idx, h_tiles, n_rows, tm):
    t, ch, _ = h_tiles.shape
    grid_spec = pltpu.PrefetchScalarGridSpec(
        num_scalar_prefetch=2,
        grid=(t // tm,),
        in_specs=[pl.BlockSpec((1, 1, MOE_TOPK * tm), lambda i, pe, pd: (i, 0, 0), memory_space=pltpu.SMEM),
                  pl.BlockSpec((tm * ch, LANES), lambda i, pe, pd: (i, 0))],
        out_specs=pl.BlockSpec(memory_space=pl.ANY),
        scratch_shapes=[pltpu.VMEM((MOE_BLOCK, ch, LANES), F32),
                        pltpu.SemaphoreType.DMA(()), pltpu.SemaphoreType.DMA(())],
    )
    return pl.pallas_call(
        _moe_dispatch_kernel,
        grid_spec=grid_spec,
        out_shape=jax.ShapeDtypeStruct((n_rows, ch, LANES), F32),
        compiler_params=_params(("arbitrary",)),
        name="moe_dispatch",
    )(pad_end, padded, idx, h_tiles.reshape(t * ch, LANES))


def _moe_ffn_kernel(be_ref, nb_ref, slotp_ref, x_ref, win_ref, wdn_ref, y_ref, ybuf, ssem):
    s = pl.program_id(0)
    n_used = nb_ref[0]
    cur = s % 3
    prev = (s + 2) % 3
    ch = y_ref.shape[1]
    blk = x_ref.shape[0] // ch
    ff = wdn_ref.shape[2]
    spare0 = y_ref.shape[0] - 2 * blk

    def scatter(slot, r, dst, sem):
        return pltpu.make_async_copy(ybuf.at[slot, pl.ds(pl.multiple_of(r * ch, ch), ch)], y_ref.at[dst], sem)

    @pl.when(s == 0)
    def _():
        ybuf[...] = jnp.zeros_like(ybuf)

        def zero(r, carry):
            scatter(0, r, spare0 + r, ssem.at[1]).start()
            return carry

        lax.fori_loop(0, blk, zero, 0, unroll=8)
        for _ in range(blk):
            scatter(0, 0, 0, ssem.at[1]).wait()

    @pl.when(s <= n_used)
    def _():
        def issue(j, carry):
            for u in range(MOE_ISSUE):
                r = j * MOE_ISSUE + u
                scatter(prev, r, slotp_ref[0, 0, r], ssem.at[s % 2]).start()
            return carry

        lax.fori_loop(0, blk // MOE_ISSUE, issue, 0)

    @pl.when(s < n_used)
    def _():
        x = _load_row_tiles(x_ref, (), blk, ch)
        hcat = jnp.dot(x.astype(BF16), win_ref[0, 0].astype(BF16), preferred_element_type=F32)
        act = (_silu(hcat[:, :ff]) * hcat[:, ff:]).astype(BF16)
        _store_row_tiles(ybuf, (cur,), jnp.dot(act, wdn_ref[0, 0].astype(BF16), preferred_element_type=F32))

    @pl.when(jnp.logical_and(s >= 1, s <= n_used + 1))
    def _():
        for _ in range(blk):
            scatter(0, 0, 0, ssem.at[(s + 1) % 2]).wait()


def _moe_ffn(block_expert, n_used, slot_prev, x_tiles, w_in, w_down, layer, n_out_rows):
    n_rows, ch, _ = x_tiles.shape
    d = ch * LANES
    nb = n_rows // MOE_BLOCK
    ff2 = w_in.shape[3]
    used = lambda s, nu: jnp.minimum(s, nu[0] - 1)
    grid_spec = pltpu.PrefetchScalarGridSpec(
        num_scalar_prefetch=2,
        grid=(nb + 2,),
        in_specs=[pl.BlockSpec((1, 1, MOE_BLOCK), lambda s, be, nu: (jnp.minimum(s, nb), 0, 0),
                               memory_space=pltpu.SMEM),
                  pl.BlockSpec((MOE_BLOCK * ch, LANES), lambda s, be, nu: (used(s, nu), 0)),
                  pl.BlockSpec((1, 1, d, ff2), lambda s, be, nu: (layer, be[used(s, nu)], 0, 0)),
                  pl.BlockSpec((1, 1, ff2 // 2, d), lambda s, be, nu: (layer, be[used(s, nu)], 0, 0))],
        out_specs=pl.BlockSpec(memory_space=pl.ANY),
        scratch_shapes=[pltpu.VMEM((3, MOE_BLOCK * ch, LANES), F32), pltpu.SemaphoreType.DMA((2,))],
    )
    return pl.pallas_call(
        _moe_ffn_kernel,
        grid_spec=grid_spec,
        out_shape=jax.ShapeDtypeStruct((n_out_rows, ch, LANES), F32),
        compiler_params=_params(("arbitrary",)),
        name="moe_ffn",
    )(block_expert, n_used, slot_prev, x_tiles.reshape(n_rows * ch, LANES), w_in, w_down)


def _first_argmax(x, rows):
    m = jnp.max(x, axis=0, keepdims=True)
    idx = jnp.min(jnp.where(x == m, rows, x.shape[0]), axis=0, keepdims=True)
    return m, idx


def _softmax_rows(x):
    e = jnp.exp(x - jnp.max(x, axis=0, keepdims=True))
    return e / jnp.sum(e, axis=0, keepdims=True)


def _route_kernel(lg_ref, ids_ref, gt_ref, cnt_ref, carry_ref):
    i = pl.program_id(0)

    @pl.when(i == 0)
    def _():
        carry_ref[...] = jnp.zeros_like(carry_ref)

    tm = lg_ref.shape[1]
    epg = MOE_PER_GROUP
    n_exp = MOE_GROUPS * epg
    rows8 = lax.broadcasted_iota(jnp.int32, (SUBLANES, tm), 0)
    group_probs = _softmax_rows(lg_ref[0:SUBLANES, :])
    p_group, g_idx = _first_argmax(group_probs, rows8)
    sel = jnp.zeros((epg, tm), F32)
    for gi in range(MOE_GROUPS):
        sel = sel + jnp.where(g_idx == gi, lg_ref[SUBLANES + epg * gi:SUBLANES + epg * (gi + 1), :], 0.0)
    p_exp = _softmax_rows(sel)
    p1, i1 = _first_argmax(p_exp, rows8)
    p2, i2 = _first_argmax(jnp.where(rows8 == i1, -1.0, p_exp), rows8)
    denom = p1 + p2
    e0 = g_idx * epg + i1
    e1 = g_idx * epg + i2

    rows_e = lax.broadcasted_iota(jnp.int32, (n_exp, tm), 0)
    oh0 = (rows_e == e0).astype(F32)
    oh1 = (rows_e == e1).astype(F32)
    oh = oh0 + oh1
    tr = lax.broadcasted_iota(jnp.int32, (tm, tm), 0)
    tc = lax.broadcasted_iota(jnp.int32, (tm, tm), 1)
    before = _dot(oh, (tr < tc).astype(F32))
    base = carry_ref[:, 0:1] + before
    rank0 = jnp.sum(oh0 * base, axis=0, keepdims=True)
    rank1 = jnp.sum(oh1 * base, axis=0, keepdims=True)
    carry_ref[...] = carry_ref[...] + jnp.sum(oh, axis=1, keepdims=True)
    cnt_ref[...] = carry_ref[...]

    zi = jnp.zeros((SUBLANES - 4, tm), jnp.int32)
    ids_ref[...] = jnp.concatenate([e0, e1, rank0.astype(jnp.int32), rank1.astype(jnp.int32), zi], axis=0)
    zf = jnp.zeros((SUBLANES - 2, tm), F32)
    gt_ref[...] = jnp.concatenate([p_group * p1 / denom, p_group * p2 / denom, zf], axis=0)


def _route(logits_t, tm=512):
    nr, t = logits_t.shape
    n_exp = MOE_GROUPS * MOE_PER_GROUP
    return pl.pallas_call(
        _route_kernel,
        grid=(t // tm,),
        in_specs=[pl.BlockSpec((nr, tm), lambda i: (0, i))],
        out_specs=[pl.BlockSpec((SUBLANES, tm), lambda i: (0, i)),
                   pl.BlockSpec((SUBLANES, tm), lambda i: (0, i)),
                   pl.BlockSpec((n_exp, LANES), lambda i: (0, 0))],
        out_shape=[jax.ShapeDtypeStruct((SUBLANES, t), jnp.int32),
                   jax.ShapeDtypeStruct((SUBLANES, t), F32),
                   jax.ShapeDtypeStruct((n_exp, LANES), F32)],
        scratch_shapes=[pltpu.VMEM((n_exp, LANES), F32)],
        compiler_params=_params(("arbitrary",)),
        name="moe_route",
    )(logits_t)


def _combine_ln_kernel(y0_ref, y1_ref, gt_ref, res_ref, g_ref, b_ref, o_ref):
    tm, d = res_ref.shape
    ch = d // LANES
    gates = jnp.concatenate([gt_ref[...], jnp.zeros((LANES - SUBLANES, tm), F32)], axis=0).T
    y = (_load_row_tiles(y0_ref, (), tm, ch) * gates[:, 0:1]
         + _load_row_tiles(y1_ref, (), tm, ch) * gates[:, 1:2])
    o_ref[...] = _layer_norm(ALPHA * res_ref[...] + y, g_ref[...], b_ref[...])


def _combine_ln(y_slots, gates, res, g, b, tm=512):
    t, d = res.shape
    nt = t // tm
    ch = d // LANES
    y2d = y_slots.reshape(-1, LANES)
    row = pl.BlockSpec((tm, d), lambda i: (i, 0))
    vec = pl.BlockSpec((1, d), lambda i: (0, 0))
    return pl.pallas_call(
        _combine_ln_kernel,
        grid=(nt,),
        in_specs=[pl.BlockSpec((tm * ch, LANES), lambda i: (i, 0)),
                  pl.BlockSpec((tm * ch, LANES), lambda i: (nt + i, 0)),
                  pl.BlockSpec((SUBLANES, tm), lambda i: (0, i)),
                  row, vec, vec],
        out_specs=row,
        out_shape=jax.ShapeDtypeStruct((t, d), F32),
        compiler_params=_params(("parallel",)),
        name="moe_combine_ln",
    )(y2d, y2d, gates, res, g, b)


def _hier_moe(h, h_tiles, logits_t, ln_g, ln_b, w_in, w_down, layer, tm=512):
    t, d = h.shape
    n_exp = w_in.shape[1]
    ids, gates, counts = _route(logits_t)

    counts = counts[:, 0].astype(jnp.int32)
    padded = (counts + MOE_BLOCK - 1) // MOE_BLOCK * MOE_BLOCK
    pad_end = jnp.cumsum(padded)
    pad_start = pad_end - padded
    n_assign = t * MOE_TOPK
    n_blocks = (n_assign + n_exp * (MOE_BLOCK - 1) + MOE_BLOCK - 1) // MOE_BLOCK
    n_rows = n_blocks * MOE_BLOCK
    block_first_row = jnp.arange(n_blocks, dtype=jnp.int32) * MOE_BLOCK
    block_expert = jnp.minimum(jnp.sum(pad_end[None, :] <= block_first_row[:, None], axis=1),
                               n_exp - 1).astype(jnp.int32)
    n_used = (pad_end[-1:] // MOE_BLOCK).astype(jnp.int32)
    experts = jnp.arange(n_exp, dtype=jnp.int32)[:, None, None]
    seg_start = jnp.sum(jnp.where(ids[None, :MOE_TOPK] == experts, pad_start[:, None, None], 0), axis=0)
    dest = seg_start + ids[MOE_TOPK:2 * MOE_TOPK]
    idx = dest.reshape(MOE_TOPK, t // tm, tm).transpose(1, 0, 2).reshape(t // tm, 1, MOE_TOPK * tm)

    code = jnp.zeros((n_rows,), jnp.int32).at[dest.reshape(n_assign)].set(
        jnp.arange(1, n_assign + 1, dtype=jnp.int32), unique_indices=True)
    rows = jnp.arange(n_rows, dtype=jnp.int32)
    spare = n_assign + (rows // MOE_BLOCK % 2) * MOE_BLOCK + rows % MOE_BLOCK
    slot = jnp.where(code > 0, code - 1, spare)
    slot_prev = jnp.concatenate([spare[MOE_BLOCK:2 * MOE_BLOCK], slot]).reshape(n_blocks + 1, 1, MOE_BLOCK)

    x_tiles = _moe_dispatch(pad_end, padded, idx, h_tiles, n_rows, tm)
    y_slots = _moe_ffn(block_expert, n_used, slot_prev, x_tiles, w_in, w_down, layer, n_assign + 2 * MOE_BLOCK)
    return _combine_ln(y_slots, gates, h, _row(ln_g), _row(ln_b))


def kernel(x, ln_g, ln_b, rw_mu, rw_w_rkv, rw_w0, rw_w1, rw_w2, rw_a0, rw_a1, rw_a2, rw_g1, rw_g2, rw_k_k, rw_k_a, rw_r_k, rw_lnx_g, rw_lnx_b, rw_w_o, mb_w_in, mb_conv_w, mb_conv_b, mb_dt_bias, mb_a_log, mb_d, mb_norm_g, mb_w_out, gl_w_in, gl_gk_w2, gl_gk_b, gl_norm_g, gl_w_out, moe_w_group, moe_b_group, moe_w_route, moe_b_route, moe_w_in, moe_w_down):
    bsz, seq, d = x.shape
    assert bsz == 1
    h = x.reshape(seq, d)
    depth = ln_g.shape[0]
    for i in range(depth):
        kind, j = i % 3, i // 3
        spare = SUBLANES - MOE_GROUPS
        wr = jnp.concatenate([moe_w_group[i].T, jnp.zeros((spare, d), F32), moe_w_route[i].T], axis=0)
        br = jnp.concatenate([moe_b_group[i], jnp.full((spare,), -1e30, F32), moe_b_route[i]]).reshape(-1, 1)
        if kind == 0:
            h, h_tiles, logits =_rwkv7_mix(h, ln_g[i, 0], ln_b[i, 0], rw_mu[j], rw_w_rkv[j], rw_w0[j], rw_w1[j],
                                   rw_w2[j], rw_a0[j], rw_a1[j], rw_a2[j], rw_g1[j], rw_g2[j], rw_k_k[j],
                                   rw_k_a[j], rw_r_k[j], rw_lnx_g[j], rw_lnx_b[j], rw_w_o[j], wr, br)
        elif kind == 1:
            h, h_tiles, logits =_mamba2_mix(h, ln_g[i, 0], ln_b[i, 0], mb_w_in[j], mb_conv_w[j], mb_conv_b[j],
                                    mb_dt_bias[j], mb_a_log[j], mb_d[j], mb_norm_g[j], mb_w_out[j], wr, br)
        else:
            h, h_tiles, logits =_gla_mix(h, ln_g[i, 0], ln_b[i, 0], gl_w_in[j], gl_gk_w2[j], gl_gk_b[j],
                                 gl_norm_g[j], gl_w_out[j], wr, br)
        h = _hier_moe(h, h_tiles, logits, ln_g[i, 1], ln_b[i, 1], moe_w_in, moe_w_down, i)
    return h.reshape(bsz, seq, d)
```

```python
import functools

import jax
import jax.numpy as jnp
from jax import lax
from jax.experimental import pallas as pl
from jax.experimental.pallas import tpu as pltpu

F32 = jnp.float32
BF16 = jnp.bfloat16

LANES = 128
SUBLANES = 8
VMEM_LIMIT = 56 * 1024 * 1024

DEPTH = 4
ALPHA = (2.0 * DEPTH) ** 0.25
LN_EPS = 1e-5

RW_HEAD = 64
RW_CHUNK = 64
RW_GN_EPS = 64e-5

MB_HEADDIM = 64
MB_GROUPS = 4
MB_STATE = 128
MB_CHUNK = 128
MB_CONV = 4
MB_NORM_EPS = 1e-5

GL_HEADS = 4
GL_CHUNK = 64
GL_GATE_NORM = 16.0
GL_NORM_EPS = 1e-5

MOE_GROUPS = 4
MOE_PER_GROUP = 8
MOE_TOPK = 2
MOE_BLOCK = 256
MOE_ISSUE = 64


def _dot(a, b):
    return jnp.dot(a.astype(BF16), b.astype(BF16), preferred_element_type=F32)


def _dot_nt(a, b):
    return lax.dot_general(a.astype(BF16), b.astype(BF16), (((1,), (1,)), ((), ())),
                           preferred_element_type=F32)


def _dot_tn(a, b):
    return lax.dot_general(a.astype(BF16), b.astype(BF16), (((0,), (0,)), ((), ())),
                           preferred_element_type=F32)


def _dot_f32(a, b):
    return jnp.dot(a, b, preferred_element_type=F32, precision=lax.Precision.HIGHEST)


def _sigmoid(x):
    return 1.0 / (1.0 + jnp.exp(-x))


def _softplus(x):
    return jnp.maximum(x, 0.0) + jnp.log(1.0 + jnp.exp(-jnp.abs(x)))


def _silu(x):
    return x * _sigmoid(x)


def _tri(n, dtype=F32):
    r = lax.broadcasted_iota(jnp.int32, (n, n), 0)
    c = lax.broadcasted_iota(jnp.int32, (n, n), 1)
    return (c <= r).astype(dtype)


def _params(sem):
    return pltpu.CompilerParams(dimension_semantics=sem, vmem_limit_bytes=VMEM_LIMIT)


def _pad_cols(w, n):
    return jnp.pad(w, ((0, 0), (0, n - w.shape[1])))


def _pad_rows(w, n):
    return jnp.pad(w, ((0, n - w.shape[0]), (0, 0)))


def _row(v):
    return v.reshape(1, -1).astype(F32)


def _prev_rows_spec(tm, width):
    per = tm // SUBLANES
    return pl.BlockSpec((SUBLANES, width), lambda i: (jnp.maximum(i * per - 1, 0), 0))


def _mm_kernel(x_ref, w_ref, o_ref):
    o_ref[...] = _dot(x_ref[...], w_ref[...])


def _mm(x, w, tm=1024, tn=None, name="mm"):
    t, k = x.shape
    n = w.shape[1]
    tn = tn or min(n, 1024)
    return pl.pallas_call(
        _mm_kernel,
        grid=(t // tm, n // tn),
        in_specs=[pl.BlockSpec((tm, k), lambda i, j: (i, 0)),
                  pl.BlockSpec((k, tn), lambda i, j: (0, j))],
        out_specs=pl.BlockSpec((tm, tn), lambda i, j: (i, j)),
        out_shape=jax.ShapeDtypeStruct((t, n), F32),
        compiler_params=_params(("parallel", "parallel")),
        name=name,
    )(x, w)


def _layer_norm(z, g, b):
    mu = jnp.mean(z, axis=-1, keepdims=True)
    zc = z - mu
    var = jnp.mean(zc * zc, axis=-1, keepdims=True)
    return zc * lax.rsqrt(var + LN_EPS) * g + b


def _store_row_tiles(ref, lead, x):
    rows = x.shape[0]
    chunks = x.shape[1] // LANES
    for c in range(chunks):
        ref[lead + (pl.ds(c, rows, stride=chunks), slice(None))] = x[:, LANES * c:LANES * (c + 1)]


def _load_row_tiles(ref, lead, rows, chunks):
    return jnp.concatenate([ref[lead + (pl.ds(c, rows, stride=chunks), slice(None))] for c in range(chunks)],
                           axis=1)


def _mm_res_ln_kernel(x_ref, w_ref, res_ref, g_ref, b_ref, wr_ref, br_ref, ot_ref, lg_ref):
    y = _dot(x_ref[...], w_ref[...])
    h = _layer_norm(ALPHA * res_ref[...] + y, g_ref[...], b_ref[...])
    _store_row_tiles(ot_ref, (), h)
    lg_ref[...] = lax.dot_general(wr_ref[...], h, (((1,), (1,)), ((), ())), preferred_element_type=F32,
                                  precision=lax.Precision.HIGHEST) + br_ref[...]


def _mm_res_ln(x, w, res, g, b, wr, br, tm=512):
    t, k = x.shape
    d = w.shape[1]
    nr = wr.shape[0]
    ch = d // LANES
    h_tiles, logits_t = pl.pallas_call(
        _mm_res_ln_kernel,
        grid=(t // tm,),
        in_specs=[pl.BlockSpec((tm, k), lambda i: (i, 0)),
                  pl.BlockSpec((k, d), lambda i: (0, 0)),
                  pl.BlockSpec((tm, d), lambda i: (i, 0)),
                  pl.BlockSpec((1, d), lambda i: (0, 0)),
                  pl.BlockSpec((1, d), lambda i: (0, 0)),
                  pl.BlockSpec((nr, d), lambda i: (0, 0)),
                  pl.BlockSpec((nr, 1), lambda i: (0, 0))],
        out_specs=[pl.BlockSpec((tm * ch, LANES), lambda i: (i, 0)),
                   pl.BlockSpec((nr, tm), lambda i: (0, i))],
        out_shape=[jax.ShapeDtypeStruct((t * ch, LANES), F32), jax.ShapeDtypeStruct((nr, t), F32)],
        compiler_params=_params(("parallel",)),
        name="mm_res_ln",
    )(x, w, res, g, b, wr, br)
    return h_tiles.reshape(t, ch, LANES), logits_t


def _rwkv_proj_kernel(x_ref, xp_ref, mu_ref, wrkv_ref, w1_ref, w2_ref, w0_ref, a1_ref, a2_ref, a0_ref,
                      g1_ref, g2_ref, r_ref, k_ref, v_ref, lw_ref, a_ref, g_ref):
    i = pl.program_id(0)
    x = x_ref[...]
    prev = jnp.where(i > 0, xp_ref[SUBLANES - 1:SUBLANES, :], 0.0)
    rows = lax.broadcasted_iota(jnp.int32, x.shape, 0)
    shifted = jnp.where(rows == 0, prev, pltpu.roll(x, 1, 0))
    xx = shifted - x

    def mix(p):
        return x + xx * mu_ref[p:p + 1, :]

    r_ref[...] = _dot(mix(0), wrkv_ref[0])
    k_ref[...] = _dot(mix(1), wrkv_ref[1])
    v_ref[...] = _dot(mix(2), wrkv_ref[2])
    w_pre = w0_ref[...] + _dot(jnp.tanh(_dot(mix(3), w1_ref[...])), w2_ref[...])
    lw_ref[...] = -jnp.exp(-_softplus(-w_pre) - 0.5)
    a_ref[...] = _sigmoid(a0_ref[...] + _dot(_dot(mix(4), a1_ref[...]), a2_ref[...]))
    g_ref[...] = _dot(_sigmoid(_dot(mix(5), g1_ref[...])), g2_ref[...])


def _rwkv_proj(x, mu, w_rkv, w1, w2, w0, a1, a2, a0, g1, g2, tm=512):
    t, d = x.shape
    full = lambda a: pl.BlockSpec(a.shape, lambda i: (0,) * a.ndim)
    row = pl.BlockSpec((tm, d), lambda i: (i, 0))
    args = (mu, w_rkv, w1, w2, w0, a1, a2, a0, g1, g2)
    return pl.pallas_call(
        _rwkv_proj_kernel,
        grid=(t // tm,),
        in_specs=[row, _prev_rows_spec(tm, d)] + [full(a) for a in args],
        out_specs=[row] * 6,
        out_shape=[jax.ShapeDtypeStruct((t, d), F32)] * 6,
        compiler_params=_params(("parallel",)),
        name="rwkv_proj",
    )(x, x, *args)


def _rwkv_rec_kernel(r_ref, k_ref, v_ref, lw_ref, a_ref, g_ref, kk_ref, ka_ref, rk_ref, lg_ref, lb_ref,
                     o_ref, s_ref):
    c = pl.program_id(0)

    @pl.when(c == 0)
    def _():
        s_ref[...] = jnp.zeros_like(s_ref)

    n = RW_CHUNK
    m = 2 * n
    pairs = range(r_ref.shape[1] // LANES)
    cs_all = _dot_f32(_tri(n), lw_ref[...])
    lane = lax.broadcasted_iota(jnp.int32, (n, LANES), 1)
    first = lane < RW_HEAD
    r2 = lax.broadcasted_iota(jnp.int32, (m, LANES), 0)
    c2 = lax.broadcasted_iota(jnp.int32, (m, LANES), 1)
    own = (r2 < n) == (c2 < RW_HEAD)
    eye = (c2 == r2).astype(F32)
    r4 = lax.broadcasted_iota(jnp.int32, (2 * m, 2 * m), 0)
    c4 = lax.broadcasted_iota(jnp.int32, (2 * m, 2 * m), 1)
    rr = jnp.where(r4 < m, r4, r4 - m)
    cc = jnp.where(c4 < m, c4, c4 - m)
    keep = cc < rr + jnp.where(r4 < m, 0, 1)

    def stack(x):
        return jnp.where(own, jnp.concatenate([x, x], axis=0), 0.0)

    def head_sum(x):
        sa = jnp.sum(jnp.where(first, x, 0.0), axis=1, keepdims=True)
        sb = jnp.sum(jnp.where(first, 0.0, x), axis=1, keepdims=True)
        return jnp.where(first, sa, sb)

    ar, bk, bkh, vs, g_chunk, bonus = [], [], [], [], [], []
    for p in pairs:
        sl = slice(LANES * p, LANES * (p + 1))
        r = r_ref[:, sl]
        k = k_ref[:, sl]
        v = v_ref[:, sl]
        lw = lw_ref[:, sl]
        a = a_ref[:, sl]
        cs = cs_all[:, sl]
        cs_end = cs[n - 1:n, :]
        kk = k * kk_ref[:, sl]
        kk = kk / jnp.maximum(jnp.sqrt(head_sum(kk * kk)), 1e-12)
        kh = k * (1.0 + (a - 1.0) * ka_ref[:, sl])
        kka = kk * a
        g_inv = jnp.exp(-cs)
        g_end = jnp.exp(cs_end - cs)
        ar.append(jnp.concatenate([stack(-kk * jnp.exp(cs - lw)), stack(r * jnp.exp(cs))], axis=0).astype(BF16))
        bk.append(jnp.concatenate([stack(kka * g_inv), stack(kh * g_inv)], axis=0).astype(BF16))
        bkh.append(jnp.concatenate([stack(kka * g_end), stack(kh * g_end)], axis=0).astype(BF16))
        vs.append(stack(v))
        g_chunk.append(jnp.exp(cs_end))
        bonus.append(head_sum(r * kh * rk_ref[:, sl]) * v)

    gm = [jnp.where(keep, _dot_nt(ar[p], bk[p]), 0.0) for p in pairs]
    nm = [gm[p][:m, :m] for p in pairs]
    akv = [_dot(gm[p][:m, m:], vs[p]) for p in pairs]
    arbk = [gm[p][m:, :].astype(BF16) for p in pairs]

    tm = [eye + nm[p] for p in pairs]
    pw = [_dot(nm[p], nm[p]) for p in pairs]
    for _ in range(4):
        both = [_dot(jnp.concatenate([tm[p], pw[p]], axis=0), pw[p]) for p in pairs]
        tm = [tm[p] + both[p][:m] for p in pairs]
        pw = [both[p][m:] for p in pairs]
    tm = [tm[p] + _dot(tm[p], pw[p]) for p in pairs]

    ars = [_dot_nt(ar[p], s_ref[p]) for p in pairs]
    u = [_dot(tm[p], ars[p][:m] + akv[p]) for p in pairs]
    uv = [jnp.concatenate([u[p], vs[p]], axis=0).astype(BF16) for p in pairs]
    y = [ars[p][m:] + _dot(arbk[p], uv[p]) for p in pairs]
    for p in pairs:
        s_ref[p] = s_ref[p] * g_chunk[p] + _dot_tn(uv[p], bkh[p])

    for p in pairs:
        sl = slice(LANES * p, LANES * (p + 1))
        mu = jnp.sum(y[p], axis=1, keepdims=True) * (1.0 / RW_HEAD)
        yc = jnp.where(own, y[p] - mu, 0.0)
        var = jnp.sum(yc * yc, axis=1, keepdims=True) * (1.0 / RW_HEAD)
        yn = yc * lax.rsqrt(var + RW_GN_EPS)
        yn = yn[:n] + yn[n:]
        o_ref[:, sl] = (yn * lg_ref[:, sl] + lb_ref[:, sl] + bonus[p]) * g_ref[:, sl]


def _rwkv_rec(r, k, v, lw, a, g, k_k, k_a, r_k, lnx_g, lnx_b):
    t, d = r.shape
    row = pl.BlockSpec((RW_CHUNK, d), lambda c: (c, 0))
    vec = pl.BlockSpec((1, d), lambda c: (0, 0))
    return pl.pallas_call(
        _rwkv_rec_kernel,
        grid=(t // RW_CHUNK,),
        in_specs=[row] * 6 + [vec] * 5,
        out_specs=row,
        out_shape=jax.ShapeDtypeStruct((t, d), F32),
        scratch_shapes=[pltpu.VMEM((d // LANES, LANES, LANES), F32)],
        compiler_params=_params(("arbitrary",)),
        name="rwkv_rec",
    )(r, k, v, lw, a, g, k_k, k_a, r_k, lnx_g, lnx_b)


def _rwkv7_mix(h, ln_g, ln_b, mu, w_rkv, w0, w1, w2, a0, a1, a2, g1, g2, k_k, k_a, r_k, lnx_g, lnx_b, w_o,
               wr, br):
    t, d = h.shape
    lora = w1.shape[1]
    lp = -(-lora // LANES) * LANES
    gl = g1.shape[1]
    gp = -(-gl // LANES) * LANES
    r, k, v, lw, a, g = _rwkv_proj(
        h, _pad_rows(mu, SUBLANES), w_rkv.astype(BF16),
        _pad_cols(w1, lp).astype(BF16), _pad_rows(w2, lp).astype(BF16), _row(w0),
        _pad_cols(a1, lp).astype(BF16), _pad_rows(a2, lp).astype(BF16), _row(a0),
        _pad_cols(g1, gp).astype(BF16), _pad_rows(g2, gp).astype(BF16))
    y = _rwkv_rec(r, k, v, lw, a, g, _row(k_k), _row(k_a), _row(r_k), _row(lnx_g), _row(lnx_b))
    return _mm_res_ln(y, w_o.astype(BF16), h, _row(ln_g), _row(ln_b), wr, br)


def _mm_conv_kernel(x_ref, xp_ref, w_ref, cw_ref, cb_ref, o_ref, buf_ref):
    i = pl.program_id(0)
    tm = x_ref.shape[0]
    w = w_ref[...]
    pre = _dot(x_ref[...], w)
    buf_ref[0:SUBLANES, :] = jnp.where(i > 0, _dot(xp_ref[...], w), 0.0)
    buf_ref[SUBLANES:, :] = pre
    acc = cb_ref[...] + pre * cw_ref[MB_CONV - 1:MB_CONV, :]
    for s in range(1, MB_CONV):
        acc = acc + buf_ref[pl.ds(SUBLANES - s, tm), :] * cw_ref[MB_CONV - 1 - s:MB_CONV - s, :]
    o_ref[...] = _silu(acc)


def _mm_conv(x, w, conv_w, conv_b, tm=1024, tn=1024):
    t, k = x.shape
    n = w.shape[1]
    return pl.pallas_call(
        _mm_conv_kernel,
        grid=(t // tm, n // tn),
        in_specs=[pl.BlockSpec((tm, k), lambda i, j: (i, 0)),
                  pl.BlockSpec((SUBLANES, k), lambda i, j: (jnp.maximum(i * (tm // SUBLANES) - 1, 0), 0)),
                  pl.BlockSpec((k, tn), lambda i, j: (0, j)),
                  pl.BlockSpec((SUBLANES, tn), lambda i, j: (0, j)),
                  pl.BlockSpec((1, tn), lambda i, j: (0, j))],
        out_specs=pl.BlockSpec((tm, tn), lambda i, j: (i, j)),
        out_shape=jax.ShapeDtypeStruct((t, n), F32),
        scratch_shapes=[pltpu.VMEM((tm + SUBLANES, tn), F32)],
        compiler_params=_params(("parallel", "parallel")),
        name="mb_in_conv",
    )(x, x, w, _pad_rows(conv_w, SUBLANES), _row(conv_b))


def _mb_ssd_kernel(xs_ref, bm_ref, cm_ref, dt_ref, z_ref, dtb_ref, alog_ref, dsk_ref, ng_ref, o_ref, s_ref,
                   *, chunks):
    c = pl.program_id(0)

    @pl.when(c == 0)
    def _():
        s_ref[...] = jnp.zeros_like(s_ref)

    n = MB_CHUNK
    p_dim = MB_HEADDIM
    groups = s_ref.shape[0]
    hpg = s_ref.shape[1] // p_dim
    gw = hpg * p_dim
    grange = range(groups)
    heads = range(groups * hpg)
    r2 = lax.broadcasted_iota(jnp.int32, (n, n), 0)
    c2 = lax.broadcasted_iota(jnp.int32, (n, n), 1)
    incl = c2 <= r2
    tri = _tri(n)
    a_neg = -jnp.exp(alog_ref[...])

    def per_head_rows(rows_t, g):
        return jnp.concatenate([jnp.broadcast_to(rows_t[hd:hd + 1, :], (p_dim, n))
                                for hd in range(g * hpg, (g + 1) * hpg)], axis=0)

    parts = []
    for j in range(chunks):
        rs = slice(n * j, n * (j + 1))
        dt = _softplus(dt_ref[rs, :] + dtb_ref[...])
        cs = _dot_f32(tri, dt * a_neg)
        dt_t = dt.T
        cs_t = cs.T
        cs_end_t = cs_t[:, n - 1:n]
        w_end_t = dt_t * jnp.exp(cs_end_t - cs_t)
        from_start_t = jnp.exp(cs_t)
        chunk_decay_t = jnp.exp(cs_end_t)
        x_t = xs_ref[rs, :].T
        bgs = [bm_ref[rs, MB_STATE * g:MB_STATE * (g + 1)].astype(BF16) for g in grange]
        cgs = [cm_ref[rs, MB_STATE * g:MB_STATE * (g + 1)].astype(BF16) for g in grange]
        scores = [_dot_nt(cgs[g], bgs[g]) for g in grange]
        y_diag = []
        for hd in heads:
            seg = jnp.minimum(cs[:, hd:hd + 1] - cs_t[hd:hd + 1, :], 0.0)
            m = scores[hd // hpg] * jnp.where(incl, jnp.exp(seg), 0.0) * dt_t[hd:hd + 1, :]
            y_diag.append(_dot_nt(x_t[p_dim * hd:p_dim * (hd + 1), :], m))
        x_end = [(x_t[gw * g:gw * (g + 1), :] * per_head_rows(w_end_t, g)).astype(BF16) for g in grange]
        decay = [jnp.concatenate([jnp.broadcast_to(chunk_decay_t[hd:hd + 1, :], (p_dim, 1))
                                  for hd in range(g * hpg, (g + 1) * hpg)], axis=0) for g in grange]
        y_fixed = [jnp.concatenate(y_diag[g * hpg:(g + 1) * hpg], axis=0)
                   + x_t[gw * g:gw * (g + 1), :] * dsk_ref[gw * g:gw * (g + 1), :] for g in grange]
        scale = [per_head_rows(from_start_t, g) for g in grange]
        parts.append((bgs, cgs, x_end, decay, y_fixed, scale))

    for j, (bgs, cgs, x_end, decay, y_fixed, scale) in enumerate(parts):
        rs = slice(n * j, n * (j + 1))
        y_off = [_dot_nt(s_ref[g], cgs[g]) for g in grange]
        for g in grange:
            s_ref[g] = s_ref[g] * decay[g] + _dot(x_end[g], bgs[g])
        for g in grange:
            y_g = (y_fixed[g] + y_off[g] * scale[g]).T * _silu(z_ref[rs, gw * g:gw * (g + 1)])
            ms = jnp.mean(y_g * y_g, axis=-1, keepdims=True)
            o_ref[rs, gw * g:gw * (g + 1)] = y_g * lax.rsqrt(ms + MB_NORM_EPS) * ng_ref[:, gw * g:gw * (g + 1)]


def _mb_ssd(xbc, dt, z, dt_bias, a_log, d_skip, norm_g, chunks=1):
    t, di = z.shape
    gn = (xbc.shape[1] - di) // 2
    groups = gn // MB_STATE
    tm = MB_CHUNK * chunks
    blk = lambda w, col=0: pl.BlockSpec((tm, w), lambda c: (c, col))
    vec = lambda w: pl.BlockSpec((1, w), lambda c: (0, 0))
    return pl.pallas_call(
        functools.partial(_mb_ssd_kernel, chunks=chunks),
        grid=(t // tm,),
        in_specs=[blk(di), blk(gn, di // gn), blk(gn, di // gn + 1), blk(LANES), blk(di), vec(LANES), vec(LANES),
                  pl.BlockSpec((di, LANES), lambda c: (0, 0)), vec(di)],
        out_specs=blk(di),
        out_shape=jax.ShapeDtypeStruct((t, di), F32),
        scratch_shapes=[pltpu.VMEM((groups, di // groups, MB_STATE), F32)],
        compiler_params=_params(("arbitrary",)),
        name="mb_ssd",
    )(xbc, xbc, xbc, dt, z, dt_bias, a_log, d_skip, norm_g)


def _mamba2_mix(h, ln_g, ln_b, w_in, conv_w, conv_b, dt_bias, a_log, d_skip, norm_g, w_out, wr, br):
    t, d = h.shape
    heads = dt_bias.shape[0]
    di = heads * MB_HEADDIM
    gn = MB_GROUPS * MB_STATE
    cd = di + 2 * gn
    w_bf = w_in.astype(BF16)
    z = _mm(h, w_bf[:, :di], name="mb_in_z")
    xbc = _mm_conv(h, w_bf[:, di:di + cd], conv_w, conv_b)
    dt = _mm(h, _pad_cols(w_bf[:, di + cd:], LANES), name="mb_in_dt")
    pad_h = lambda v_: _pad_cols(_row(v_), LANES)
    y = _mb_ssd(xbc, dt, z, pad_h(dt_bias), pad_h(a_log),
                jnp.broadcast_to(jnp.repeat(d_skip, MB_HEADDIM)[:, None], (di, LANES)).astype(F32), _row(norm_g))
    return _mm_res_ln(y, w_out.astype(BF16), h, _row(ln_g), _row(ln_b), wr, br)


def _gla_kernel(qk_ref, vg_ref, lr_ref, w2_ref, b2_ref, ng_ref, o_ref, s_ref, *, chunks):
    c = pl.program_id(0)

    @pl.when(c == 0)
    def _():
        s_ref[...] = jnp.zeros_like(s_ref)

    n = GL_CHUNK
    heads, dv, dk = s_ref.shape
    kd = heads * dk
    vd = heads * dv
    tri = _tri(n)
    r2 = lax.broadcasted_iota(jnp.int32, (n, n), 0)
    c2 = lax.broadcasted_iota(jnp.int32, (n, n), 1)
    incl = c2 <= r2
    hrange = range(heads)
    pre = _dot(lr_ref[...], w2_ref[...]) + b2_ref[...]
    log_alpha = (jnp.minimum(pre, 0.0) - jnp.log(1.0 + jnp.exp(-jnp.abs(pre)))) * (1.0 / GL_GATE_NORM)
    q_in, k_end, v_bf, decay, intra = {}, {}, {}, {}, {}
    for j in range(chunks):
        rs = slice(n * j, n * (j + 1))
        b_all = _dot_f32(tri, log_alpha[rs, :])
        for hd in hrange:
            q = qk_ref[rs, dk * hd:dk * (hd + 1)] * (dk ** -0.5)
            k = qk_ref[rs, kd + dk * hd:kd + dk * (hd + 1)]
            b = b_all[:, dk * hd:dk * (hd + 1)]
            b_end = b[n - 1:n, :]
            q_in[j, hd] = (q * jnp.exp(b)).astype(BF16)
            k_end[j, hd] = (k * jnp.exp(b_end - b)).astype(BF16)
            v_bf[j, hd] = vg_ref[rs, dv * hd:dv * (hd + 1)].astype(BF16)
            decay[j, hd] = jnp.exp(b_end)
            attn = jnp.where(incl, _dot_nt(q_in[j, hd], k * jnp.exp(-b)), 0.0)
            intra[j, hd] = _dot(attn, v_bf[j, hd])
    for j in range(chunks):
        rs = slice(n * j, n * (j + 1))
        inter = [_dot_nt(q_in[j, hd], s_ref[hd]) for hd in hrange]
        for hd in hrange:
            s_ref[hd] = s_ref[hd] * decay[j, hd] + _dot_tn(v_bf[j, hd], k_end[j, hd])
        for hd in hrange:
            o = intra[j, hd] + inter[hd]
            o = o * lax.rsqrt(jnp.mean(o * o, axis=-1, keepdims=True) + GL_NORM_EPS) * ng_ref[...]
            o_ref[rs, dv * hd:dv * (hd + 1)] = o * _silu(vg_ref[rs, vd + dv * hd:vd + dv * (hd + 1)])


def _gla(qk, vg, lr, w2, b2, norm_g, chunks=4):
    t = qk.shape[0]
    kd = qk.shape[1] // 2
    vd = vg.shape[1] // 2
    tm = GL_CHUNK * chunks
    blk = lambda w: pl.BlockSpec((tm, w), lambda c: (c, 0))
    full = lambda a: pl.BlockSpec(a.shape, lambda c: (0, 0))
    return pl.pallas_call(
        functools.partial(_gla_kernel, chunks=chunks),
        grid=(t // tm,),
        in_specs=[blk(2 * kd), blk(2 * vd), blk(LANES), full(w2), full(b2), full(norm_g)],
        out_specs=blk(vd),
        out_shape=jax.ShapeDtypeStruct((t, vd), F32),
        scratch_shapes=[pltpu.VMEM((GL_HEADS, vd // GL_HEADS, kd // GL_HEADS), F32)],
        compiler_params=_params(("arbitrary",)),
        name="gla",
    )(qk, vg, lr, w2, b2, norm_g)


def _gla_mix(h, ln_g, ln_b, w_in, gk_w2, gk_b, norm_g, w_out, wr, br):
    kd = gk_w2.shape[1]
    vd = w_out.shape[0]
    w_bf = w_in.astype(BF16)
    qk = _mm(h, w_bf[:, :2 * kd], name="gl_in_qk")
    vg = _mm(h, w_bf[:, 2 * kd:2 * kd + 2 * vd], name="gl_in_vg")
    lr = _mm(h, _pad_cols(w_bf[:, 2 * kd + 2 * vd:], LANES), name="gl_in_lr")
    y = _gla(qk, vg, lr, _pad_rows(gk_w2, LANES).astype(BF16), _row(gk_b), _row(norm_g))
    return _mm_res_ln(y, w_out.astype(BF16), h, _row(ln_g), _row(ln_b), wr, br)


def _moe_dispatch_kernel(pe_ref, pd_ref, idx_ref, h_ref, x_ref, zero_ref, sem, zsem):
    i = pl.program_id(0)
    blk, ch, _ = zero_ref.shape
    tm = h_ref.shape[0] // ch
    n_exp = pe_ref.shape[0]

    def fill(e):
        return pltpu.make_async_copy(zero_ref, x_ref.at[pl.ds(pe_ref[e] - blk, blk)], zsem)

    def fill_block(b):
        return pltpu.make_async_copy(zero_ref, x_ref.at[pl.ds(b * blk, blk)], zsem)

    @pl.when(i == 0)
    def _():
        zero_ref[...] = jnp.zeros_like(zero_ref)
        first_unused = pe_ref[n_exp - 1] // blk
        n_blocks = x_ref.shape[0] // blk
        for e in range(n_exp):
            @pl.when(pd_ref[e] > 0)
            def _(e=e):
                fill(e).start()
        lax.fori_loop(first_unused, n_blocks, lambda b, c: (fill_block(b).start(), c)[1], 0)
        for e in range(n_exp):
            @pl.when(pd_ref[e] > 0)
            def _(e=e):
                fill(e).wait()
        lax.fori_loop(first_unused, n_blocks, lambda b, c: (fill_block(b).wait(), c)[1], 0)

    def row(t, dst):
        return pltpu.make_async_copy(h_ref.at[pl.ds(pl.multiple_of(t * ch, ch), ch)], x_ref.at[dst], sem)

    def issue(j, carry):
        for u in range(MOE_ISSUE):
            t = j * MOE_ISSUE + u
            row(t, idx_ref[0, 0, t]).start()
            row(t, idx_ref[0, 0, tm + t]).start()
        return carry

    lax.fori_loop(0, tm // MOE_ISSUE, issue, 0)
    for _ in range(MOE_TOPK * tm):
        row(0, 0).wait()


def _moe_dispatch(pad_end, padded, idx, h_tiles, n_rows, tm):
    t, ch, _ = h_tiles.shape
    grid_spec = pltpu.PrefetchScalarGridSpec(
        num_scalar_prefetch=2,
        grid=(t // tm,),
        in_specs=[pl.BlockSpec((1, 1, MOE_TOPK * tm), lambda i, pe, pd: (i, 0, 0), memory_space=pltpu.SMEM),
                  pl.BlockSpec((tm * ch, LANES), lambda i, pe, pd: (i, 0))],
        out_specs=pl.BlockSpec(memory_space=pl.ANY),
        scratch_shapes=[pltpu.VMEM((MOE_BLOCK, ch, LANES), F32),
                        pltpu.SemaphoreType.DMA(()), pltpu.SemaphoreType.DMA(())],
    )
    return pl.pallas_call(
        _moe_dispatch_kernel,
        grid_spec=grid_spec,
        out_shape=jax.ShapeDtypeStruct((n_rows, ch, LANES), F32),
        compiler_params=_params(("arbitrary",)),
        name="moe_dispatch",
    )(pad_end, padded, idx, h_tiles.reshape(t * ch, LANES))


def _moe_ffn_kernel(be_ref, nb_ref, x_ref, win_ref, wdn_ref, y_ref):
    s = pl.program_id(0)
    ch = wdn_ref.shape[3] // LANES
    blk = x_ref.shape[0] // ch
    ff = wdn_ref.shape[2]

    @pl.when(s < nb_ref[0])
    def _():
        x = _load_row_tiles(x_ref, (), blk, ch)
        hcat = jnp.dot(x.astype(BF16), win_ref[0, 0].astype(BF16), preferred_element_type=F32)
        act = (_silu(hcat[:, :ff]) * hcat[:, ff:]).astype(BF16)
        _store_row_tiles(y_ref, (), jnp.dot(act, wdn_ref[0, 0].astype(BF16), preferred_element_type=F32))

    @pl.when(s >= nb_ref[0])
    def _():
        y_ref[...] = jnp.zeros_like(y_ref)


def _moe_ffn(block_expert, n_used, x_tiles, w_in, w_down, layer):
    n_rows, ch, _ = x_tiles.shape
    d = ch * LANES
    ff2 = w_in.shape[3]
    used = lambda s, nu: jnp.minimum(s, nu[0] - 1)
    grid_spec = pltpu.PrefetchScalarGridSpec(
        num_scalar_prefetch=2,
        grid=(n_rows // MOE_BLOCK,),
        in_specs=[pl.BlockSpec((MOE_BLOCK * ch, LANES), lambda s, be, nu: (used(s, nu), 0)),
                  pl.BlockSpec((1, 1, d, ff2), lambda s, be, nu: (layer, be[used(s, nu)], 0, 0)),
                  pl.BlockSpec((1, 1, ff2 // 2, d), lambda s, be, nu: (layer, be[used(s, nu)], 0, 0))],
        out_specs=pl.BlockSpec((MOE_BLOCK * ch, LANES), lambda s, be, nu: (s, 0)),
    )
    return pl.pallas_call(
        _moe_ffn_kernel,
        grid_spec=grid_spec,
        out_shape=jax.ShapeDtypeStruct((n_rows * ch, LANES), F32),
        compiler_params=_params(("arbitrary",)),
        name="moe_ffn",
    )(block_expert, n_used, x_tiles.reshape(n_rows * ch, LANES), w_in, w_down).reshape(n_rows, ch, LANES)


def _first_argmax(x, rows):
    m = jnp.max(x, axis=0, keepdims=True)
    idx = jnp.min(jnp.where(x == m, rows, x.shape[0]), axis=0, keepdims=True)
    return m, idx


def _softmax_rows(x):
    e = jnp.exp(x - jnp.max(x, axis=0, keepdims=True))
    return e / jnp.sum(e, axis=0, keepdims=True)


def _route_kernel(lg_ref, ids_ref, gt_ref, cnt_ref, carry_ref):
    i = pl.program_id(0)

    @pl.when(i == 0)
    def _():
        carry_ref[...] = jnp.zeros_like(carry_ref)

    tm = lg_ref.shape[1]
    epg = MOE_PER_GROUP
    n_exp = MOE_GROUPS * epg
    rows8 = lax.broadcasted_iota(jnp.int32, (SUBLANES, tm), 0)
    group_probs = _softmax_rows(lg_ref[0:SUBLANES, :])
    p_group, g_idx = _first_argmax(group_probs, rows8)
    sel = jnp.zeros((epg, tm), F32)
    for gi in range(MOE_GROUPS):
        sel = sel + jnp.where(g_idx == gi, lg_ref[SUBLANES + epg * gi:SUBLANES + epg * (gi + 1), :], 0.0)
    p_exp = _softmax_rows(sel)
    p1, i1 = _first_argmax(p_exp, rows8)
    p2, i2 = _first_argmax(jnp.where(rows8 == i1, -1.0, p_exp), rows8)
    denom = p1 + p2
    e0 = g_idx * epg + i1
    e1 = g_idx * epg + i2

    rows_e = lax.broadcasted_iota(jnp.int32, (n_exp, tm), 0)
    oh0 = (rows_e == e0).astype(F32)
    oh1 = (rows_e == e1).astype(F32)
    oh = oh0 + oh1
    tr = lax.broadcasted_iota(jnp.int32, (tm, tm), 0)
    tc = lax.broadcasted_iota(jnp.int32, (tm, tm), 1)
    before = _dot(oh, (tr < tc).astype(F32))
    base = carry_ref[:, 0:1] + before
    rank0 = jnp.sum(oh0 * base, axis=0, keepdims=True)
    rank1 = jnp.sum(oh1 * base, axis=0, keepdims=True)
    carry_ref[...] = carry_ref[...] + jnp.sum(oh, axis=1, keepdims=True)
    cnt_ref[...] = carry_ref[...]

    zi = jnp.zeros((SUBLANES - 4, tm), jnp.int32)
    ids_ref[...] = jnp.concatenate([e0, e1, rank0.astype(jnp.int32), rank1.astype(jnp.int32), zi], axis=0)
    zf = jnp.zeros((SUBLANES - 2, tm), F32)
    gt_ref[...] = jnp.concatenate([p_group * p1 / denom, p_group * p2 / denom, zf], axis=0)


def _route(logits_t, tm=512):
    nr, t = logits_t.shape
    n_exp = MOE_GROUPS * MOE_PER_GROUP
    return pl.pallas_call(
        _route_kernel,
        grid=(t // tm,),
        in_specs=[pl.BlockSpec((nr, tm), lambda i: (0, i))],
        out_specs=[pl.BlockSpec((SUBLANES, tm), lambda i: (0, i)),
                   pl.BlockSpec((SUBLANES, tm), lambda i: (0, i)),
                   pl.BlockSpec((n_exp, LANES), lambda i: (0, 0))],
        out_shape=[jax.ShapeDtypeStruct((SUBLANES, t), jnp.int32),
                   jax.ShapeDtypeStruct((SUBLANES, t), F32),
                   jax.ShapeDtypeStruct((n_exp, LANES), F32)],
        scratch_shapes=[pltpu.VMEM((n_exp, LANES), F32)],
        compiler_params=_params(("arbitrary",)),
        name="moe_route",
    )(logits_t)


def _combine_ln_kernel(idx_ref, y_ref, gt_ref, res_ref, g_ref, b_ref, o_ref, ybuf, sem):
    tm, d = o_ref.shape
    ch = d // LANES

    def row(k, t, src):
        return pltpu.make_async_copy(y_ref.at[src], ybuf.at[k, pl.ds(pl.multiple_of(t * ch, ch), ch)], sem)

    def issue(j, carry):
        for u in range(MOE_ISSUE):
            t = j * MOE_ISSUE + u
            for k in range(MOE_TOPK):
                row(k, t, idx_ref[0, 0, k * tm + t]).start()
        return carry

    lax.fori_loop(0, tm // MOE_ISSUE, issue, 0)
    gates = jnp.concatenate([gt_ref[...], jnp.zeros((LANES - SUBLANES, tm), F32)], axis=0).T
    for _ in range(MOE_TOPK * tm):
        row(0, 0, 0).wait()
    y = (_load_row_tiles(ybuf, (0,), tm, ch) * gates[:, 0:1]
         + _load_row_tiles(ybuf, (1,), tm, ch) * gates[:, 1:2])
    o_ref[...] = _layer_norm(ALPHA * _load_row_tiles(res_ref, (), tm, ch) + y, g_ref[...], b_ref[...])


def _combine_ln(idx, y_tiles, gates, res_tiles, g, b, tm):
    t, ch, _ = res_tiles.shape
    d = ch * LANES
    row = pl.BlockSpec((tm, d), lambda i: (i, 0))
    vec = pl.BlockSpec((1, d), lambda i: (0, 0))
    return pl.pallas_call(
        _combine_ln_kernel,
        grid=(t // tm,),
        in_specs=[pl.BlockSpec((1, 1, MOE_TOPK * tm), lambda i: (i, 0, 0), memory_space=pltpu.SMEM),
                  pl.BlockSpec(memory_space=pl.ANY),
                  pl.BlockSpec((SUBLANES, tm), lambda i: (0, i)),
                  pl.BlockSpec((tm * ch, LANES), lambda i: (i, 0)), vec, vec],
        out_specs=row,
        out_shape=jax.ShapeDtypeStruct((t, d), F32),
        scratch_shapes=[pltpu.VMEM((MOE_TOPK, tm * ch, LANES), F32), pltpu.SemaphoreType.DMA(())],
        compiler_params=_params(("arbitrary",)),
        name="moe_combine_ln",
    )(idx, y_tiles, gates, res_tiles.reshape(t * ch, LANES), g, b)


def _hier_moe(h_tiles, logits_t, ln_g, ln_b, w_in, w_down, layer, tm=512):
    t = h_tiles.shape[0]
    n_exp = w_in.shape[1]
    ids, gates, counts = _route(logits_t)

    counts = counts[:, 0].astype(jnp.int32)
    padded = (counts + MOE_BLOCK - 1) // MOE_BLOCK * MOE_BLOCK
    pad_end = jnp.cumsum(padded)
    pad_start = pad_end - padded
    n_assign = t * MOE_TOPK
    n_blocks = (n_assign + n_exp * (MOE_BLOCK - 1) + MOE_BLOCK - 1) // MOE_BLOCK
    n_rows = n_blocks * MOE_BLOCK
    block_first_row = jnp.arange(n_blocks, dtype=jnp.int32) * MOE_BLOCK
    block_expert = jnp.minimum(jnp.sum(pad_end[None, :] <= block_first_row[:, None], axis=1),
                               n_exp - 1).astype(jnp.int32)
    n_used = (pad_end[-1:] // MOE_BLOCK).astype(jnp.int32)
    experts = jnp.arange(n_exp, dtype=jnp.int32)[:, None, None]
    seg_start = jnp.sum(jnp.where(ids[None, :MOE_TOPK] == experts, pad_start[:, None, None], 0), axis=0)
    dest = seg_start + ids[MOE_TOPK:2 * MOE_TOPK]
    idx = dest.reshape(MOE_TOPK, t // tm, tm).transpose(1, 0, 2).reshape(t // tm, 1, MOE_TOPK * tm)

    x_tiles = _moe_dispatch(pad_end, padded, idx, h_tiles, n_rows, tm)
    y_tiles = _moe_ffn(block_expert, n_used, x_tiles, w_in, w_down, layer)
    return _combine_ln(idx, y_tiles, gates, h_tiles, _row(ln_g), _row(ln_b), tm)


def kernel(x, ln_g, ln_b, rw_mu, rw_w_rkv, rw_w0, rw_w1, rw_w2, rw_a0, rw_a1, rw_a2, rw_g1, rw_g2, rw_k_k, rw_k_a, rw_r_k, rw_lnx_g, rw_lnx_b, rw_w_o, mb_w_in, mb_conv_w, mb_conv_b, mb_dt_bias, mb_a_log, mb_d, mb_norm_g, mb_w_out, gl_w_in, gl_gk_w2, gl_gk_b, gl_norm_g, gl_w_out, moe_w_group, moe_b_group, moe_w_route, moe_b_route, moe_w_in, moe_w_down):
    bsz, seq, d = x.shape
    assert bsz == 1
    h = x.reshape(seq, d)
    depth = ln_g.shape[0]
    for i in range(depth):
        kind, j = i % 3, i // 3
        spare = SUBLANES - MOE_GROUPS
        wr = jnp.concatenate([moe_w_group[i].T, jnp.zeros((spare, d), F32), moe_w_route[i].T], axis=0)
        br = jnp.concatenate([moe_b_group[i], jnp.full((spare,), -1e30, F32), moe_b_route[i]]).reshape(-1, 1)
        if kind == 0:
            h_tiles, logits =_rwkv7_mix(h, ln_g[i, 0], ln_b[i, 0], rw_mu[j], rw_w_rkv[j], rw_w0[j], rw_w1[j],
                                   rw_w2[j], rw_a0[j], rw_a1[j], rw_a2[j], rw_g1[j], rw_g2[j], rw_k_k[j],
                                   rw_k_a[j], rw_r_k[j], rw_lnx_g[j], rw_lnx_b[j], rw_w_o[j], wr, br)
        elif kind == 1:
            h_tiles, logits =_mamba2_mix(h, ln_g[i, 0], ln_b[i, 0], mb_w_in[j], mb_conv_w[j], mb_conv_b[j],
                                    mb_dt_bias[j], mb_a_log[j], mb_d[j], mb_norm_g[j], mb_w_out[j], wr, br)
        else:
            h_tiles, logits =_gla_mix(h, ln_g[i, 0], ln_b[i, 0], gl_w_in[j], gl_gk_w2[j], gl_gk_b[j],
                                 gl_norm_g[j], gl_w_out[j], wr, br)
        h = _hier_moe(h_tiles, logits, ln_g[i, 1], ln_b[i, 1], moe_w_in, moe_w_down, i)
    return h.reshape(bsz, seq, d)
```

```python
import functools

import jax
import jax.numpy as jnp
from jax import lax
from jax.experimental import pallas as pl
from jax.experimental.pallas import tpu as pltpu

F32 = jnp.float32
BF16 = jnp.bfloat16

LANES = 128
SUBLANES = 8
VMEM_LIMIT = 56 * 1024 * 1024

DEPTH = 4
ALPHA = (2.0 * DEPTH) ** 0.25
LN_EPS = 1e-5

RW_HEAD = 64
RW_CHUNK = 64
RW_GN_EPS = 64e-5

MB_HEADDIM = 64
MB_GROUPS = 4
MB_STATE = 128
MB_CHUNK = 128
MB_CONV = 4
MB_NORM_EPS = 1e-5

GL_HEADS = 4
GL_CHUNK = 64
GL_GATE_NORM = 16.0
GL_NORM_EPS = 1e-5

MOE_GROUPS = 4
MOE_PER_GROUP = 8
MOE_TOPK = 2
MOE_BLOCK = 256
MOE_ISSUE = 64


def _dot(a, b):
    return jnp.dot(a.astype(BF16), b.astype(BF16), preferred_element_type=F32)


def _dot_nt(a, b):
    return lax.dot_general(a.astype(BF16), b.astype(BF16), (((1,), (1,)), ((), ())),
                           preferred_element_type=F32)


def _dot_tn(a, b):
    return lax.dot_general(a.astype(BF16), b.astype(BF16), (((0,), (0,)), ((), ())),
                           preferred_element_type=F32)


def _dot_f32(a, b):
    return jnp.dot(a, b, preferred_element_type=F32, precision=lax.Precision.HIGHEST)


def _sigmoid(x):
    return 1.0 / (1.0 + jnp.exp(-x))


def _softplus(x):
    return jnp.maximum(x, 0.0) + jnp.log(1.0 + jnp.exp(-jnp.abs(x)))


def _silu(x):
    return x * _sigmoid(x)


def _tri(n, dtype=F32):
    r = lax.broadcasted_iota(jnp.int32, (n, n), 0)
    c = lax.broadcasted_iota(jnp.int32, (n, n), 1)
    return (c <= r).astype(dtype)


def _params(sem):
    return pltpu.CompilerParams(dimension_semantics=sem, vmem_limit_bytes=VMEM_LIMIT)


def _pad_cols(w, n):
    return jnp.pad(w, ((0, 0), (0, n - w.shape[1])))


def _pad_rows(w, n):
    return jnp.pad(w, ((0, n - w.shape[0]), (0, 0)))


def _row(v):
    return v.reshape(1, -1).astype(F32)


def _prev_rows_spec(tm, width):
    per = tm // SUBLANES
    return pl.BlockSpec((SUBLANES, width), lambda i: (jnp.maximum(i * per - 1, 0), 0))


def _mm_kernel(x_ref, w_ref, o_ref):
    o_ref[...] = _dot(x_ref[...], w_ref[...])


def _mm(x, w, tm=1024, tn=None, name="mm"):
    t, k = x.shape
    n = w.shape[1]
    tn = tn or min(n, 1024)
    return pl.pallas_call(
        _mm_kernel,
        grid=(t // tm, n // tn),
        in_specs=[pl.BlockSpec((tm, k), lambda i, j: (i, 0)),
                  pl.BlockSpec((k, tn), lambda i, j: (0, j))],
        out_specs=pl.BlockSpec((tm, tn), lambda i, j: (i, j)),
        out_shape=jax.ShapeDtypeStruct((t, n), F32),
        compiler_params=_params(("parallel", "parallel")),
        name=name,
    )(x, w)


def _layer_norm(z, g, b):
    mu = jnp.mean(z, axis=-1, keepdims=True)
    zc = z - mu
    var = jnp.mean(zc * zc, axis=-1, keepdims=True)
    return zc * lax.rsqrt(var + LN_EPS) * g + b


def _store_row_tiles(ref, lead, x):
    rows = x.shape[0]
    chunks = x.shape[1] // LANES
    for c in range(chunks):
        ref[lead + (pl.ds(c, rows, stride=chunks), slice(None))] = x[:, LANES * c:LANES * (c + 1)]


def _load_row_tiles(ref, lead, rows, chunks):
    return jnp.concatenate([ref[lead + (pl.ds(c, rows, stride=chunks), slice(None))] for c in range(chunks)],
                           axis=1)


def _mm_res_ln_kernel(x_ref, w_ref, res_ref, g_ref, b_ref, wr_ref, br_ref, ot_ref, lg_ref):
    y = _dot(x_ref[...], w_ref[...])
    h = _layer_norm(ALPHA * res_ref[...] + y, g_ref[...], b_ref[...])
    _store_row_tiles(ot_ref, (), h)
    lg_ref[...] = lax.dot_general(wr_ref[...], h, (((1,), (1,)), ((), ())), preferred_element_type=F32,
                                  precision=lax.Precision.HIGHEST) + br_ref[...]


def _mm_res_ln(x, w, res, g, b, wr, br, tm=512):
    t, k = x.shape
    d = w.shape[1]
    nr = wr.shape[0]
    ch = d // LANES
    h_tiles, logits_t = pl.pallas_call(
        _mm_res_ln_kernel,
        grid=(t // tm,),
        in_specs=[pl.BlockSpec((tm, k), lambda i: (i, 0)),
                  pl.BlockSpec((k, d), lambda i: (0, 0)),
                  pl.BlockSpec((tm, d), lambda i: (i, 0)),
                  pl.BlockSpec((1, d), lambda i: (0, 0)),
                  pl.BlockSpec((1, d), lambda i: (0, 0)),
                  pl.BlockSpec((nr, d), lambda i: (0, 0)),
                  pl.BlockSpec((nr, 1), lambda i: (0, 0))],
        out_specs=[pl.BlockSpec((tm * ch, LANES), lambda i: (i, 0)),
                   pl.BlockSpec((nr, tm), lambda i: (0, i))],
        out_shape=[jax.ShapeDtypeStruct((t * ch, LANES), F32), jax.ShapeDtypeStruct((nr, t), F32)],
        compiler_params=_params(("parallel",)),
        name="mm_res_ln",
    )(x, w, res, g, b, wr, br)
    return h_tiles.reshape(t, ch, LANES), logits_t


def _rwkv_proj_kernel(x_ref, xp_ref, mu_ref, wrkv_ref, w1_ref, w2_ref, w0_ref, a1_ref, a2_ref, a0_ref,
                      g1_ref, g2_ref, r_ref, k_ref, v_ref, lw_ref, a_ref, g_ref):
    i = pl.program_id(0)
    x = x_ref[...]
    prev = jnp.where(i > 0, xp_ref[SUBLANES - 1:SUBLANES, :], 0.0)
    rows = lax.broadcasted_iota(jnp.int32, x.shape, 0)
    shifted = jnp.where(rows == 0, prev, pltpu.roll(x, 1, 0))
    xx = shifted - x

    def mix(p):
        return x + xx * mu_ref[p:p + 1, :]

    r_ref[...] = _dot(mix(0), wrkv_ref[0])
    k_ref[...] = _dot(mix(1), wrkv_ref[1])
    v_ref[...] = _dot(mix(2), wrkv_ref[2])
    w_pre = w0_ref[...] + _dot(jnp.tanh(_dot(mix(3), w1_ref[...])), w2_ref[...])
    lw_ref[...] = -jnp.exp(-_softplus(-w_pre) - 0.5)
    a_ref[...] = _sigmoid(a0_ref[...] + _dot(_dot(mix(4), a1_ref[...]), a2_ref[...]))
    g_ref[...] = _dot(_sigmoid(_dot(mix(5), g1_ref[...])), g2_ref[...])


def _rwkv_proj(x, mu, w_rkv, w1, w2, w0, a1, a2, a0, g1, g2, tm=512):
    t, d = x.shape
    full = lambda a: pl.BlockSpec(a.shape, lambda i: (0,) * a.ndim)
    row = pl.BlockSpec((tm, d), lambda i: (i, 0))
    args = (mu, w_rkv, w1, w2, w0, a1, a2, a0, g1, g2)
    return pl.pallas_call(
        _rwkv_proj_kernel,
        grid=(t // tm,),
        in_specs=[row, _prev_rows_spec(tm, d)] + [full(a) for a in args],
        out_specs=[row] * 6,
        out_shape=[jax.ShapeDtypeStruct((t, d), F32)] * 6,
        compiler_params=_params(("parallel",)),
        name="rwkv_proj",
    )(x, x, *args)


def _rwkv_rec_kernel(r_ref, k_ref, v_ref, lw_ref, a_ref, g_ref, kk_ref, ka_ref, rk_ref, lg_ref, lb_ref,
                     o_ref, s_ref):
    c = pl.program_id(0)

    @pl.when(c == 0)
    def _():
        s_ref[...] = jnp.zeros_like(s_ref)

    n = RW_CHUNK
    m = 2 * n
    pairs = range(r_ref.shape[1] // LANES)
    chunks = r_ref.shape[0] // n
    tri = _tri(n)
    lane = lax.broadcasted_iota(jnp.int32, (n, LANES), 1)
    first = lane < RW_HEAD
    r2 = lax.broadcasted_iota(jnp.int32, (m, LANES), 0)
    c2 = lax.broadcasted_iota(jnp.int32, (m, LANES), 1)
    own = (r2 < n) == (c2 < RW_HEAD)
    eye = (c2 == r2).astype(F32)
    r4 = lax.broadcasted_iota(jnp.int32, (2 * m, 2 * m), 0)
    c4 = lax.broadcasted_iota(jnp.int32, (2 * m, 2 * m), 1)
    rr = jnp.where(r4 < m, r4, r4 - m)
    cc = jnp.where(c4 < m, c4, c4 - m)
    keep = cc < rr + jnp.where(r4 < m, 0, 1)

    def stack(x):
        return jnp.where(own, jnp.concatenate([x, x], axis=0), 0.0)

    def head_sum(x):
        sa = jnp.sum(jnp.where(first, x, 0.0), axis=1, keepdims=True)
        sb = jnp.sum(jnp.where(first, 0.0, x), axis=1, keepdims=True)
        return jnp.where(first, sa, sb)

    units = [(j, p) for j in range(chunks) for p in pairs]
    ar, bk, bkh, vs, g_chunk, bonus = {}, {}, {}, {}, {}, {}
    for j in range(chunks):
        rs = slice(n * j, n * (j + 1))
        cs_all = _dot_f32(tri, lw_ref[rs, :])
        for p in pairs:
            sl = slice(LANES * p, LANES * (p + 1))
            r = r_ref[rs, sl]
            k = k_ref[rs, sl]
            v = v_ref[rs, sl]
            lw = lw_ref[rs, sl]
            a = a_ref[rs, sl]
            cs = cs_all[:, sl]
            cs_end = cs[n - 1:n, :]
            kk = k * kk_ref[:, sl]
            kk = kk / jnp.maximum(jnp.sqrt(head_sum(kk * kk)), 1e-12)
            kh = k * (1.0 + (a - 1.0) * ka_ref[:, sl])
            kka = kk * a
            g_inv = jnp.exp(-cs)
            g_end = jnp.exp(cs_end - cs)
            ar[j, p] = jnp.concatenate([stack(-kk * jnp.exp(cs - lw)), stack(r * jnp.exp(cs))], axis=0).astype(BF16)
            bk[j, p] = jnp.concatenate([stack(kka * g_inv), stack(kh * g_inv)], axis=0).astype(BF16)
            bkh[j, p] = jnp.concatenate([stack(kka * g_end), stack(kh * g_end)], axis=0).astype(BF16)
            vs[j, p] = stack(v)
            g_chunk[j, p] = jnp.exp(cs_end)
            bonus[j, p] = head_sum(r * kh * rk_ref[:, sl]) * v

    gm = {q: jnp.where(keep, _dot_nt(ar[q], bk[q]), 0.0) for q in units}
    nm = {q: gm[q][:m, :m] for q in units}
    akv = {q: _dot(gm[q][:m, m:], vs[q]) for q in units}
    arbk = {q: gm[q][m:, :].astype(BF16) for q in units}

    tm = {q: eye + nm[q] for q in units}
    pw = {q: _dot(nm[q], nm[q]) for q in units}
    for _ in range(4):
        both = {q: _dot(jnp.concatenate([tm[q], pw[q]], axis=0), pw[q]) for q in units}
        tm = {q: tm[q] + both[q][:m] for q in units}
        pw = {q: both[q][m:] for q in units}
    tm = {q: tm[q] + _dot(tm[q], pw[q]) for q in units}

    for j in range(chunks):
        rs = slice(n * j, n * (j + 1))
        ars = [_dot_nt(ar[j, p], s_ref[p]) for p in pairs]
        u = [_dot(tm[j, p], ars[p][:m] + akv[j, p]) for p in pairs]
        uv = [jnp.concatenate([u[p], vs[j, p]], axis=0).astype(BF16) for p in pairs]
        y = [ars[p][m:] + _dot(arbk[j, p], uv[p]) for p in pairs]
        for p in pairs:
            s_ref[p] = s_ref[p] * g_chunk[j, p] + _dot_tn(uv[p], bkh[j, p])
        for p in pairs:
            sl = slice(LANES * p, LANES * (p + 1))
            mu = jnp.sum(y[p], axis=1, keepdims=True) * (1.0 / RW_HEAD)
            yc = jnp.where(own, y[p] - mu, 0.0)
            var = jnp.sum(yc * yc, axis=1, keepdims=True) * (1.0 / RW_HEAD)
            yn = yc * lax.rsqrt(var + RW_GN_EPS)
            yn = yn[:n] + yn[n:]
            o_ref[rs, sl] = (yn * lg_ref[:, sl] + lb_ref[:, sl] + bonus[j, p]) * g_ref[rs, sl]


def _rwkv_rec(r, k, v, lw, a, g, k_k, k_a, r_k, lnx_g, lnx_b, chunks=2):
    t, d = r.shape
    row = pl.BlockSpec((RW_CHUNK * chunks, d), lambda c: (c, 0))
    vec = pl.BlockSpec((1, d), lambda c: (0, 0))
    return pl.pallas_call(
        _rwkv_rec_kernel,
        grid=(t // (RW_CHUNK * chunks),),
        in_specs=[row] * 6 + [vec] * 5,
        out_specs=row,
        out_shape=jax.ShapeDtypeStruct((t, d), F32),
        scratch_shapes=[pltpu.VMEM((d // LANES, LANES, LANES), F32)],
        compiler_params=_params(("arbitrary",)),
        name="rwkv_rec",
    )(r, k, v, lw, a, g, k_k, k_a, r_k, lnx_g, lnx_b)


def _rwkv7_mix(h, ln_g, ln_b, mu, w_rkv, w0, w1, w2, a0, a1, a2, g1, g2, k_k, k_a, r_k, lnx_g, lnx_b, w_o,
               wr, br):
    t, d = h.shape
    lora = w1.shape[1]
    lp = -(-lora // LANES) * LANES
    gl = g1.shape[1]
    gp = -(-gl // LANES) * LANES
    r, k, v, lw, a, g = _rwkv_proj(
        h, _pad_rows(mu, SUBLANES), w_rkv.astype(BF16),
        _pad_cols(w1, lp).astype(BF16), _pad_rows(w2, lp).astype(BF16), _row(w0),
        _pad_cols(a1, lp).astype(BF16), _pad_rows(a2, lp).astype(BF16), _row(a0),
        _pad_cols(g1, gp).astype(BF16), _pad_rows(g2, gp).astype(BF16))
    y = _rwkv_rec(r, k, v, lw, a, g, _row(k_k), _row(k_a), _row(r_k), _row(lnx_g), _row(lnx_b))
    return _mm_res_ln(y, w_o.astype(BF16), h, _row(ln_g), _row(ln_b), wr, br)


def _mm_conv_kernel(x_ref, xp_ref, w_ref, cw_ref, cb_ref, o_ref, buf_ref):
    i = pl.program_id(0)
    tm = x_ref.shape[0]
    w = w_ref[...]
    pre = _dot(x_ref[...], w)
    buf_ref[0:SUBLANES, :] = jnp.where(i > 0, _dot(xp_ref[...], w), 0.0)
    buf_ref[SUBLANES:, :] = pre
    acc = cb_ref[...] + pre * cw_ref[MB_CONV - 1:MB_CONV, :]
    for s in range(1, MB_CONV):
        acc = acc + buf_ref[pl.ds(SUBLANES - s, tm), :] * cw_ref[MB_CONV - 1 - s:MB_CONV - s, :]
    o_ref[...] = _silu(acc)


def _mm_conv(x, w, conv_w, conv_b, tm=1024, tn=1024):
    t, k = x.shape
    n = w.shape[1]
    return pl.pallas_call(
        _mm_conv_kernel,
        grid=(t // tm, n // tn),
        in_specs=[pl.BlockSpec((tm, k), lambda i, j: (i, 0)),
                  pl.BlockSpec((SUBLANES, k), lambda i, j: (jnp.maximum(i * (tm // SUBLANES) - 1, 0), 0)),
                  pl.BlockSpec((k, tn), lambda i, j: (0, j)),
                  pl.BlockSpec((SUBLANES, tn), lambda i, j: (0, j)),
                  pl.BlockSpec((1, tn), lambda i, j: (0, j))],
        out_specs=pl.BlockSpec((tm, tn), lambda i, j: (i, j)),
        out_shape=jax.ShapeDtypeStruct((t, n), F32),
        scratch_shapes=[pltpu.VMEM((tm + SUBLANES, tn), F32)],
        compiler_params=_params(("parallel", "parallel")),
        name="mb_in_conv",
    )(x, x, w, _pad_rows(conv_w, SUBLANES), _row(conv_b))


def _mb_ssd_kernel(xs_ref, bm_ref, cm_ref, dt_ref, z_ref, dtb_ref, alog_ref, dsk_ref, ng_ref, o_ref, s_ref,
                   *, chunks):
    c = pl.program_id(0)

    @pl.when(c == 0)
    def _():
        s_ref[...] = jnp.zeros_like(s_ref)

    n = MB_CHUNK
    p_dim = MB_HEADDIM
    groups = s_ref.shape[0]
    hpg = s_ref.shape[1] // p_dim
    gw = hpg * p_dim
    grange = range(groups)
    heads = range(groups * hpg)
    r2 = lax.broadcasted_iota(jnp.int32, (n, n), 0)
    c2 = lax.broadcasted_iota(jnp.int32, (n, n), 1)
    incl = c2 <= r2
    tri = _tri(n)
    a_neg = -jnp.exp(alog_ref[...])

    def per_head_rows(rows_t, g):
        return jnp.concatenate([jnp.broadcast_to(rows_t[hd:hd + 1, :], (p_dim, n))
                                for hd in range(g * hpg, (g + 1) * hpg)], axis=0)

    parts = []
    for j in range(chunks):
        rs = slice(n * j, n * (j + 1))
        dt = _softplus(dt_ref[rs, :] + dtb_ref[...])
        cs = _dot_f32(tri, dt * a_neg)
        dt_t = dt.T
        cs_t = cs.T
        cs_end_t = cs_t[:, n - 1:n]
        w_end_t = dt_t * jnp.exp(cs_end_t - cs_t)
        from_start_t = jnp.exp(cs_t)
        chunk_decay_t = jnp.exp(cs_end_t)
        x_t = xs_ref[rs, :].T
        bgs = [bm_ref[rs, MB_STATE * g:MB_STATE * (g + 1)].astype(BF16) for g in grange]
        cgs = [cm_ref[rs, MB_STATE * g:MB_STATE * (g + 1)].astype(BF16) for g in grange]
        scores = [_dot_nt(cgs[g], bgs[g]) for g in grange]
        y_diag = []
        for hd in heads:
            seg = jnp.minimum(cs[:, hd:hd + 1] - cs_t[hd:hd + 1, :], 0.0)
            m = scores[hd // hpg] * jnp.where(incl, jnp.exp(seg), 0.0) * dt_t[hd:hd + 1, :]
            y_diag.append(_dot_nt(x_t[p_dim * hd:p_dim * (hd + 1), :], m))
        x_end = [(x_t[gw * g:gw * (g + 1), :] * per_head_rows(w_end_t, g)).astype(BF16) for g in grange]
        decay = [jnp.concatenate([jnp.broadcast_to(chunk_decay_t[hd:hd + 1, :], (p_dim, 1))
                                  for hd in range(g * hpg, (g + 1) * hpg)], axis=0) for g in grange]
        y_fixed = [jnp.concatenate(y_diag[g * hpg:(g + 1) * hpg], axis=0)
                   + x_t[gw * g:gw * (g + 1), :] * dsk_ref[gw * g:gw * (g + 1), :] for g in grange]
        scale = [per_head_rows(from_start_t, g) for g in grange]
        parts.append((bgs, cgs, x_end, decay, y_fixed, scale))

    for j, (bgs, cgs, x_end, decay, y_fixed, scale) in enumerate(parts):
        rs = slice(n * j, n * (j + 1))
        y_off = [_dot_nt(s_ref[g], cgs[g]) for g in grange]
        for g in grange:
            s_ref[g] = s_ref[g] * decay[g] + _dot(x_end[g], bgs[g])
        for g in grange:
            y_g = (y_fixed[g] + y_off[g] * scale[g]).T * _silu(z_ref[rs, gw * g:gw * (g + 1)])
            ms = jnp.mean(y_g * y_g, axis=-1, keepdims=True)
            o_ref[rs, gw * g:gw * (g + 1)] = y_g * lax.rsqrt(ms + MB_NORM_EPS) * ng_ref[:, gw * g:gw * (g + 1)]


def _mb_ssd(xbc, dt, z, dt_bias, a_log, d_skip, norm_g, chunks=1):
    t, di = z.shape
    gn = (xbc.shape[1] - di) // 2
    groups = gn // MB_STATE
    tm = MB_CHUNK * chunks
    blk = lambda w, col=0: pl.BlockSpec((tm, w), lambda c: (c, col))
    vec = lambda w: pl.BlockSpec((1, w), lambda c: (0, 0))
    return pl.pallas_call(
        functools.partial(_mb_ssd_kernel, chunks=chunks),
        grid=(t // tm,),
        in_specs=[blk(di), blk(gn, di // gn), blk(gn, di // gn + 1), blk(LANES), blk(di), vec(LANES), vec(LANES),
                  pl.BlockSpec((di, LANES), lambda c: (0, 0)), vec(di)],
        out_specs=blk(di),
        out_shape=jax.ShapeDtypeStruct((t, di), F32),
        scratch_shapes=[pltpu.VMEM((groups, di // groups, MB_STATE), F32)],
        compiler_params=_params(("arbitrary",)),
        name="mb_ssd",
    )(xbc, xbc, xbc, dt, z, dt_bias, a_log, d_skip, norm_g)


def _mamba2_mix(h, ln_g, ln_b, w_in, conv_w, conv_b, dt_bias, a_log, d_skip, norm_g, w_out, wr, br):
    t, d = h.shape
    heads = dt_bias.shape[0]
    di = heads * MB_HEADDIM
    gn = MB_GROUPS * MB_STATE
    cd = di + 2 * gn
    w_bf = w_in.astype(BF16)
    z = _mm(h, w_bf[:, :di], name="mb_in_z")
    xbc = _mm_conv(h, w_bf[:, di:di + cd], conv_w, conv_b)
    dt = _mm(h, _pad_cols(w_bf[:, di + cd:], LANES), name="mb_in_dt")
    pad_h = lambda v_: _pad_cols(_row(v_), LANES)
    y = _mb_ssd(xbc, dt, z, pad_h(dt_bias), pad_h(a_log),
                jnp.broadcast_to(jnp.repeat(d_skip, MB_HEADDIM)[:, None], (di, LANES)).astype(F32), _row(norm_g))
    return _mm_res_ln(y, w_out.astype(BF16), h, _row(ln_g), _row(ln_b), wr, br)


def _gla_kernel(qk_ref, vg_ref, lr_ref, w2_ref, b2_ref, ng_ref, o_ref, s_ref, *, chunks):
    c = pl.program_id(0)

    @pl.when(c == 0)
    def _():
        s_ref[...] = jnp.zeros_like(s_ref)

    n = GL_CHUNK
    heads, dv, dk = s_ref.shape
    kd = heads * dk
    vd = heads * dv
    tri = _tri(n)
    r2 = lax.broadcasted_iota(jnp.int32, (n, n), 0)
    c2 = lax.broadcasted_iota(jnp.int32, (n, n), 1)
    incl = c2 <= r2
    hrange = range(heads)
    pre = _dot(lr_ref[...], w2_ref[...]) + b2_ref[...]
    log_alpha = (jnp.minimum(pre, 0.0) - jnp.log(1.0 + jnp.exp(-jnp.abs(pre)))) * (1.0 / GL_GATE_NORM)
    q_in, k_end, v_bf, decay, intra = {}, {}, {}, {}, {}
    for j in range(chunks):
        rs = slice(n * j, n * (j + 1))
        b_all = _dot_f32(tri, log_alpha[rs, :])
        for hd in hrange:
            q = qk_ref[rs, dk * hd:dk * (hd + 1)] * (dk ** -0.5)
            k = qk_ref[rs, kd + dk * hd:kd + dk * (hd + 1)]
            b = b_all[:, dk * hd:dk * (hd + 1)]
            b_end = b[n - 1:n, :]
            q_in[j, hd] = (q * jnp.exp(b)).astype(BF16)
            k_end[j, hd] = (k * jnp.exp(b_end - b)).astype(BF16)
            v_bf[j, hd] = vg_ref[rs, dv * hd:dv * (hd + 1)].astype(BF16)
            decay[j, hd] = jnp.exp(b_end)
            attn = jnp.where(incl, _dot_nt(q_in[j, hd], k * jnp.exp(-b)), 0.0)
            intra[j, hd] = _dot(attn, v_bf[j, hd])
    for j in range(chunks):
        rs = slice(n * j, n * (j + 1))
        inter = [_dot_nt(q_in[j, hd], s_ref[hd]) for hd in hrange]
        for hd in hrange:
            s_ref[hd] = s_ref[hd] * decay[j, hd] + _dot_tn(v_bf[j, hd], k_end[j, hd])
        for hd in hrange:
            o = intra[j, hd] + inter[hd]
            o = o * lax.rsqrt(jnp.mean(o * o, axis=-1, keepdims=True) + GL_NORM_EPS) * ng_ref[...]
            o_ref[rs, dv * hd:dv * (hd + 1)] = o * _silu(vg_ref[rs, vd + dv * hd:vd + dv * (hd + 1)])


def _gla(qk, vg, lr, w2, b2, norm_g, chunks=4):
    t = qk.shape[0]
    kd = qk.shape[1] // 2
    vd = vg.shape[1] // 2
    tm = GL_CHUNK * chunks
    blk = lambda w: pl.BlockSpec((tm, w), lambda c: (c, 0))
    full = lambda a: pl.BlockSpec(a.shape, lambda c: (0, 0))
    return pl.pallas_call(
        functools.partial(_gla_kernel, chunks=chunks),
        grid=(t // tm,),
        in_specs=[blk(2 * kd), blk(2 * vd), blk(LANES), full(w2), full(b2), full(norm_g)],
        out_specs=blk(vd),
        out_shape=jax.ShapeDtypeStruct((t, vd), F32),
        scratch_shapes=[pltpu.VMEM((GL_HEADS, vd // GL_HEADS, kd // GL_HEADS), F32)],
        compiler_params=_params(("arbitrary",)),
        name="gla",
    )(qk, vg, lr, w2, b2, norm_g)


def _gla_mix(h, ln_g, ln_b, w_in, gk_w2, gk_b, norm_g, w_out, wr, br):
    kd = gk_w2.shape[1]
    vd = w_out.shape[0]
    w_bf = w_in.astype(BF16)
    qk = _mm(h, w_bf[:, :2 * kd], name="gl_in_qk")
    vg = _mm(h, w_bf[:, 2 * kd:2 * kd + 2 * vd], name="gl_in_vg")
    lr = _mm(h, _pad_cols(w_bf[:, 2 * kd + 2 * vd:], LANES), name="gl_in_lr")
    y = _gla(qk, vg, lr, _pad_rows(gk_w2, LANES).astype(BF16), _row(gk_b), _row(norm_g))
    return _mm_res_ln(y, w_out.astype(BF16), h, _row(ln_g), _row(ln_b), wr, br)


def _moe_dispatch_kernel(pe_ref, pd_ref, idx_ref, h_ref, x_ref, zero_ref, sem, zsem):
    i = pl.program_id(0)
    blk, ch, _ = zero_ref.shape
    tm = h_ref.shape[0] // ch
    n_exp = pe_ref.shape[0]

    def fill(e):
        return pltpu.make_async_copy(zero_ref, x_ref.at[pl.ds(pe_ref[e] - blk, blk)], zsem)

    def fill_block(b):
        return pltpu.make_async_copy(zero_ref, x_ref.at[pl.ds(b * blk, blk)], zsem)

    @pl.when(i == 0)
    def _():
        zero_ref[...] = jnp.zeros_like(zero_ref)
        first_unused = pe_ref[n_exp - 1] // blk
        n_blocks = x_ref.shape[0] // blk
        for e in range(n_exp):
            @pl.when(pd_ref[e] > 0)
            def _(e=e):
                fill(e).start()
        lax.fori_loop(first_unused, n_blocks, lambda b, c: (fill_block(b).start(), c)[1], 0)
        for e in range(n_exp):
            @pl.when(pd_ref[e] > 0)
            def _(e=e):
                fill(e).wait()
        lax.fori_loop(first_unused, n_blocks, lambda b, c: (fill_block(b).wait(), c)[1], 0)

    def row(t, dst):
        return pltpu.make_async_copy(h_ref.at[pl.ds(pl.multiple_of(t * ch, ch), ch)], x_ref.at[dst], sem)

    def issue(j, carry):
        for u in range(MOE_ISSUE):
            t = j * MOE_ISSUE + u
            row(t, idx_ref[0, 0, t]).start()
            row(t, idx_ref[0, 0, tm + t]).start()
        return carry

    lax.fori_loop(0, tm // MOE_ISSUE, issue, 0)
    for _ in range(MOE_TOPK * tm):
        row(0, 0).wait()


def _moe_dispatch(pad_end, padded, idx, h_tiles, n_rows, tm):
    t, ch, _ = h_tiles.shape
    grid_spec = pltpu.PrefetchScalarGridSpec(
        num_scalar_prefetch=2,
        grid=(t // tm,),
        in_specs=[pl.BlockSpec((1, 1, MOE_TOPK * tm), lambda i, pe, pd: (i, 0, 0), memory_space=pltpu.SMEM),
                  pl.BlockSpec((tm * ch, LANES), lambda i, pe, pd: (i, 0))],
        out_specs=pl.BlockSpec(memory_space=pl.ANY),
        scratch_shapes=[pltpu.VMEM((MOE_BLOCK, ch, LANES), F32),
                        pltpu.SemaphoreType.DMA(()), pltpu.SemaphoreType.DMA(())],
    )
    return pl.pallas_call(
        _moe_dispatch_kernel,
        grid_spec=grid_spec,
        out_shape=jax.ShapeDtypeStruct((n_rows, ch, LANES), F32),
        compiler_params=_params(("arbitrary",)),
        name="moe_dispatch",
    )(pad_end, padded, idx, h_tiles.reshape(t * ch, LANES))


def _moe_ffn_kernel(be_ref, nb_ref, x_ref, win_ref, wdn_ref, y_ref):
    s = pl.program_id(0)
    ch = wdn_ref.shape[3] // LANES
    blk = x_ref.shape[0] // ch
    ff = wdn_ref.shape[2]

    @pl.when(s < nb_ref[0])
    def _():
        x = _load_row_tiles(x_ref, (), blk, ch)
        hcat = jnp.dot(x.astype(BF16), win_ref[0, 0].astype(BF16), preferred_element_type=F32)
        act = (_silu(hcat[:, :ff]) * hcat[:, ff:]).astype(BF16)
        _store_row_tiles(y_ref, (), jnp.dot(act, wdn_ref[0, 0].astype(BF16), preferred_element_type=F32))

    @pl.when(s >= nb_ref[0])
    def _():
        y_ref[...] = jnp.zeros_like(y_ref)


def _moe_ffn(block_expert, n_used, x_tiles, w_in, w_down, layer):
    n_rows, ch, _ = x_tiles.shape
    d = ch * LANES
    ff2 = w_in.shape[3]
    used = lambda s, nu: jnp.minimum(s, nu[0] - 1)
    grid_spec = pltpu.PrefetchScalarGridSpec(
        num_scalar_prefetch=2,
        grid=(n_rows // MOE_BLOCK,),
        in_specs=[pl.BlockSpec((MOE_BLOCK * ch, LANES), lambda s, be, nu: (used(s, nu), 0)),
                  pl.BlockSpec((1, 1, d, ff2), lambda s, be, nu: (layer, be[used(s, nu)], 0, 0)),
                  pl.BlockSpec((1, 1, ff2 // 2, d), lambda s, be, nu: (layer, be[used(s, nu)], 0, 0))],
        out_specs=pl.BlockSpec((MOE_BLOCK * ch, LANES), lambda s, be, nu: (s, 0)),
    )
    return pl.pallas_call(
        _moe_ffn_kernel,
        grid_spec=grid_spec,
        out_shape=jax.ShapeDtypeStruct((n_rows * ch, LANES), F32),
        compiler_params=_params(("arbitrary",)),
        name="moe_ffn",
    )(block_expert, n_used, x_tiles.reshape(n_rows * ch, LANES), w_in, w_down).reshape(n_rows, ch, LANES)


def _first_argmax(x, rows):
    m = jnp.max(x, axis=0, keepdims=True)
    idx = jnp.min(jnp.where(x == m, rows, x.shape[0]), axis=0, keepdims=True)
    return m, idx


def _softmax_rows(x):
    e = jnp.exp(x - jnp.max(x, axis=0, keepdims=True))
    return e / jnp.sum(e, axis=0, keepdims=True)


def _route_kernel(lg_ref, ids_ref, gt_ref, cnt_ref, carry_ref):
    i = pl.program_id(0)

    @pl.when(i == 0)
    def _():
        carry_ref[...] = jnp.zeros_like(carry_ref)

    tm = lg_ref.shape[1]
    epg = MOE_PER_GROUP
    n_exp = MOE_GROUPS * epg
    rows8 = lax.broadcasted_iota(jnp.int32, (SUBLANES, tm), 0)
    group_probs = _softmax_rows(lg_ref[0:SUBLANES, :])
    p_group, g_idx = _first_argmax(group_probs, rows8)
    sel = jnp.zeros((epg, tm), F32)
    for gi in range(MOE_GROUPS):
        sel = sel + jnp.where(g_idx == gi, lg_ref[SUBLANES + epg * gi:SUBLANES + epg * (gi + 1), :], 0.0)
    p_exp = _softmax_rows(sel)
    p1, i1 = _first_argmax(p_exp, rows8)
    p2, i2 = _first_argmax(jnp.where(rows8 == i1, -1.0, p_exp), rows8)
    denom = p1 + p2
    e0 = g_idx * epg + i1
    e1 = g_idx * epg + i2

    rows_e = lax.broadcasted_iota(jnp.int32, (n_exp, tm), 0)
    oh0 = (rows_e == e0).astype(F32)
    oh1 = (rows_e == e1).astype(F32)
    oh = oh0 + oh1
    tr = lax.broadcasted_iota(jnp.int32, (tm, tm), 0)
    tc = lax.broadcasted_iota(jnp.int32, (tm, tm), 1)
    before = _dot(oh, (tr < tc).astype(F32))
    base = carry_ref[:, 0:1] + before
    rank0 = jnp.sum(oh0 * base, axis=0, keepdims=True)
    rank1 = jnp.sum(oh1 * base, axis=0, keepdims=True)
    carry_ref[...] = carry_ref[...] + jnp.sum(oh, axis=1, keepdims=True)
    cnt_ref[...] = carry_ref[...]

    zi = jnp.zeros((SUBLANES - 4, tm), jnp.int32)
    ids_ref[...] = jnp.concatenate([e0, e1, rank0.astype(jnp.int32), rank1.astype(jnp.int32), zi], axis=0)
    zf = jnp.zeros((SUBLANES - 2, tm), F32)
    gt_ref[...] = jnp.concatenate([p_group * p1 / denom, p_group * p2 / denom, zf], axis=0)


def _route(logits_t, tm=512):
    nr, t = logits_t.shape
    n_exp = MOE_GROUPS * MOE_PER_GROUP
    return pl.pallas_call(
        _route_kernel,
        grid=(t // tm,),
        in_specs=[pl.BlockSpec((nr, tm), lambda i: (0, i))],
        out_specs=[pl.BlockSpec((SUBLANES, tm), lambda i: (0, i)),
                   pl.BlockSpec((SUBLANES, tm), lambda i: (0, i)),
                   pl.BlockSpec((n_exp, LANES), lambda i: (0, 0))],
        out_shape=[jax.ShapeDtypeStruct((SUBLANES, t), jnp.int32),
                   jax.ShapeDtypeStruct((SUBLANES, t), F32),
                   jax.ShapeDtypeStruct((n_exp, LANES), F32)],
        scratch_shapes=[pltpu.VMEM((n_exp, LANES), F32)],
        compiler_params=_params(("arbitrary",)),
        name="moe_route",
    )(logits_t)


def _combine_ln_kernel(idx_ref, y_ref, gt_ref, res_ref, g_ref, b_ref, o_ref, ybuf, sem):
    tm, d = o_ref.shape
    ch = d // LANES

    def row(k, t, src):
        return pltpu.make_async_copy(y_ref.at[src], ybuf.at[k, pl.ds(pl.multiple_of(t * ch, ch), ch)], sem)

    def issue(j, carry):
        for u in range(MOE_ISSUE):
            t = j * MOE_ISSUE + u
            for k in range(MOE_TOPK):
                row(k, t, idx_ref[0, 0, k * tm + t]).start()
        return carry

    lax.fori_loop(0, tm // MOE_ISSUE, issue, 0)
    gates = jnp.concatenate([gt_ref[...], jnp.zeros((LANES - SUBLANES, tm), F32)], axis=0).T
    for _ in range(MOE_TOPK * tm):
        row(0, 0, 0).wait()
    y = (_load_row_tiles(ybuf, (0,), tm, ch) * gates[:, 0:1]
         + _load_row_tiles(ybuf, (1,), tm, ch) * gates[:, 1:2])
    o_ref[...] = _layer_norm(ALPHA * _load_row_tiles(res_ref, (), tm, ch) + y, g_ref[...], b_ref[...])


def _combine_ln(idx, y_tiles, gates, res_tiles, g, b, tm):
    t, ch, _ = res_tiles.shape
    d = ch * LANES
    row = pl.BlockSpec((tm, d), lambda i: (i, 0))
    vec = pl.BlockSpec((1, d), lambda i: (0, 0))
    return pl.pallas_call(
        _combine_ln_kernel,
        grid=(t // tm,),
        in_specs=[pl.BlockSpec((1, 1, MOE_TOPK * tm), lambda i: (i, 0, 0), memory_space=pltpu.SMEM),
                  pl.BlockSpec(memory_space=pl.ANY),
                  pl.BlockSpec((SUBLANES, tm), lambda i: (0, i)),
                  pl.BlockSpec((tm * ch, LANES), lambda i: (i, 0)), vec, vec],
        out_specs=row,
        out_shape=jax.ShapeDtypeStruct((t, d), F32),
        scratch_shapes=[pltpu.VMEM((MOE_TOPK, tm * ch, LANES), F32), pltpu.SemaphoreType.DMA(())],
        compiler_params=_params(("arbitrary",)),
        name="moe_combine_ln",
    )(idx, y_tiles, gates, res_tiles.reshape(t * ch, LANES), g, b)


def _hier_moe(h_tiles, logits_t, ln_g, ln_b, w_in, w_down, layer, tm=512):
    t = h_tiles.shape[0]
    n_exp = w_in.shape[1]
    ids, gates, counts = _route(logits_t)

    counts = counts[:, 0].astype(jnp.int32)
    padded = (counts + MOE_BLOCK - 1) // MOE_BLOCK * MOE_BLOCK
    pad_end = jnp.cumsum(padded)
    pad_start = pad_end - padded
    n_assign = t * MOE_TOPK
    n_blocks = (n_assign + n_exp * (MOE_BLOCK - 1) + MOE_BLOCK - 1) // MOE_BLOCK
    n_rows = n_blocks * MOE_BLOCK
    block_first_row = jnp.arange(n_blocks, dtype=jnp.int32) * MOE_BLOCK
    block_expert = jnp.minimum(jnp.sum(pad_end[None, :] <= block_first_row[:, None], axis=1),
                               n_exp - 1).astype(jnp.int32)
    n_used = (pad_end[-1:] // MOE_BLOCK).astype(jnp.int32)
    experts = jnp.arange(n_exp, dtype=jnp.int32)[:, None, None]
    seg_start = jnp.sum(jnp.where(ids[None, :MOE_TOPK] == experts, pad_start[:, None, None], 0), axis=0)
    dest = seg_start + ids[MOE_TOPK:2 * MOE_TOPK]
    idx = dest.reshape(MOE_TOPK, t // tm, tm).transpose(1, 0, 2).reshape(t // tm, 1, MOE_TOPK * tm)

    x_tiles = _moe_dispatch(pad_end, padded, idx, h_tiles, n_rows, tm)
    y_tiles = _moe_ffn(block_expert, n_used, x_tiles, w_in, w_down, layer)
    return _combine_ln(idx, y_tiles, gates, h_tiles, _row(ln_g), _row(ln_b), tm)


def kernel(x, ln_g, ln_b, rw_mu, rw_w_rkv, rw_w0, rw_w1, rw_w2, rw_a0, rw_a1, rw_a2, rw_g1, rw_g2, rw_k_k, rw_k_a, rw_r_k, rw_lnx_g, rw_lnx_b, rw_w_o, mb_w_in, mb_conv_w, mb_conv_b, mb_dt_bias, mb_a_log, mb_d, mb_norm_g, mb_w_out, gl_w_in, gl_gk_w2, gl_gk_b, gl_norm_g, gl_w_out, moe_w_group, moe_b_group, moe_w_route, moe_b_route, moe_w_in, moe_w_down):
    bsz, seq, d = x.shape
    assert bsz == 1
    h = x.reshape(seq, d)
    depth = ln_g.shape[0]
    for i in range(depth):
        kind, j = i % 3, i // 3
        spare = SUBLANES - MOE_GROUPS
        wr = jnp.concatenate([moe_w_group[i].T, jnp.zeros((spare, d), F32), moe_w_route[i].T], axis=0)
        br = jnp.concatenate([moe_b_group[i], jnp.full((spare,), -1e30, F32), moe_b_route[i]]).reshape(-1, 1)
        if kind == 0:
            h_tiles, logits =_rwkv7_mix(h, ln_g[i, 0], ln_b[i, 0], rw_mu[j], rw_w_rkv[j], rw_w0[j], rw_w1[j],
                                   rw_w2[j], rw_a0[j], rw_a1[j], rw_a2[j], rw_g1[j], rw_g2[j], rw_k_k[j],
                                   rw_k_a[j], rw_r_k[j], rw_lnx_g[j], rw_lnx_b[j], rw_w_o[j], wr, br)
        elif kind == 1:
            h_tiles, logits =_mamba2_mix(h, ln_g[i, 0], ln_b[i, 0], mb_w_in[j], mb_conv_w[j], mb_conv_b[j],
                                    mb_dt_bias[j], mb_a_log[j], mb_d[j], mb_norm_g[j], mb_w_out[j], wr, br)
        else:
            h_tiles, logits =_gla_mix(h, ln_g[i, 0], ln_b[i, 0], gl_w_in[j], gl_gk_w2[j], gl_gk_b[j],
                                 gl_norm_g[j], gl_w_out[j], wr, br)
        h = _hier_moe(h_tiles, logits, ln_g[i, 1], ln_b[i, 1], moe_w_in, moe_w_down, i)
    return h.reshape(bsz, seq, d)
```
